```python
import math
import jax, jax.numpy as jnp
from jax import lax
import numpy as np

D_MODEL = 1024
BATCH = 8
SEQ = 4096
DEPTH = 4

GRID_W = 64
CTX_LEN = 256
HEAD_DIM = 64
N_MIXERS = 4
GROUP_HEADS = D_MODEL // (N_MIXERS * HEAD_DIM)
GROUP_W = GROUP_HEADS * HEAD_DIM
MIX_W = N_MIXERS * GROUP_W
ROPE_BASE = 10000.0
RET_CHUNK = 128
DIFF_QK_DIM = HEAD_DIM // 2
Q_BLOCK = 128
GDN_CHUNK = 64
SHORT_CONV = 3
WINDOW = 128
SWA_KV_HEADS = GROUP_HEADS // 2
N_EXPERTS = 16
CAPACITY_FACTOR = 2
EXPERT_FF = 2 * D_MODEL
LN_EPS = 1e-5
DEEPNORM_ALPHA = (2 * DEPTH) ** 0.25
DEEPNORM_BETA = (8 * DEPTH) ** -0.25

IN_COLS = (
    ('ret_q', GROUP_W), ('ret_k', GROUP_W), ('ret_v', GROUP_W), ('ret_g', GROUP_W),
    ('diff_q', GROUP_W), ('diff_k', GROUP_W), ('diff_v', GROUP_W),
    ('gdn_qkv', 3 * GROUP_W), ('gdn_g', GROUP_W), ('gdn_a', 2 * GROUP_HEADS), ('gdn_b', 2 * GROUP_HEADS),
    ('swa_q', GROUP_W), ('swa_k', SWA_KV_HEADS * HEAD_DIM), ('swa_v', SWA_KV_HEADS * HEAD_DIM),
)
IN_W = sum(w for _, w in IN_COLS)

kernel_name = 'hybrid_parallel_heads_diffusion_block'

F32 = jnp.float32


def split_cols(p):
    out, off = {}, 0
    for name, w in IN_COLS:
        out[name] = p[..., off:off + w]
        off += w
    return out


def layer_norm(x, g, b):
    xf = x.astype(F32)
    mu = jnp.mean(xf, -1, keepdims=True)
    var = jnp.mean(jnp.square(xf - mu), -1, keepdims=True)
    return ((xf - mu) * lax.rsqrt(var + LN_EPS) * g.astype(F32) + b.astype(F32)).astype(x.dtype)


def rms_norm(x, g=None, eps=1e-6):
    xf = x.astype(F32)
    y = xf * lax.rsqrt(jnp.mean(jnp.square(xf), -1, keepdims=True) + eps)
    if g is not None:
        y = y * g.astype(F32)
    return y


def l2_normalize(x, eps=1e-6):
    return x * lax.rsqrt(jnp.sum(jnp.square(x), -1, keepdims=True) + eps)


def axial_rope_angles(n, rot_dim):
    rows = n // GRID_W
    row = jnp.repeat(jnp.arange(rows, dtype=F32), GRID_W)
    col = jnp.tile(jnp.arange(GRID_W, dtype=F32), rows)
    n_freq = rot_dim // 4
    inv = ROPE_BASE ** (-jnp.arange(n_freq, dtype=F32) / n_freq)
    return jnp.concatenate([row[:, None] * inv, col[:, None] * inv], -1)


def retention_angles(n, dim):
    theta = 1.0 / (ROPE_BASE ** jnp.linspace(0.0, 1.0, dim // 2, dtype=F32))
    return jnp.arange(n, dtype=F32)[:, None] * theta


def rotate(x, ang):
    half = x.shape[-1] // 2
    cos, sin = jnp.cos(ang).astype(x.dtype), jnp.sin(ang).astype(x.dtype)
    x1, x2 = x[..., :half], x[..., half:]
    return jnp.concatenate([x1 * cos - x2 * sin, x1 * sin + x2 * cos], -1)


def flip_seq(t):
    return jnp.flip(t, axis=2)


def sink_softmax(s, sink):
    full = jnp.concatenate([s, jnp.broadcast_to(sink, s.shape[:-1] + (1,)).astype(s.dtype)], -1)
    return jax.nn.softmax(full, -1)[..., :-1]


def retention_scan(q, k, v, log_g, s0, with_out):
    b, h, n, dk = q.shape
    dv = v.shape[-1]
    c = RET_CHUNK
    nc = n // c
    qc, kc, vc = (t.reshape(b, h, nc, c, t.shape[-1]) for t in (q, k, v))
    pos = jnp.arange(c, dtype=F32)
    lg = log_g[:, None]
    k_decay = jnp.exp(lg * (c - 1 - pos))
    kv = jnp.einsum('bhncd,hc,bhnce->nbhde', kc, k_decay, vc)
    chunk_decay = jnp.exp(log_g * c)[None, :, None, None]

    def step(s, kv_n):
        return s * chunk_decay + kv_n, (s if with_out else None)

    s_last, s_prev = lax.scan(step, s0, kv)
    if not with_out:
        return None, s_last
    rel = pos[:, None] - pos[None, :]
    d_mat = jnp.where(rel >= 0, jnp.exp(lg[:, :, None] * jnp.maximum(rel, 0.0)), 0.0)
    q_decay = jnp.exp(lg * (pos + 1.0))
    scores = jnp.einsum('bhnid,bhnjd->bhnij', qc, kc) * d_mat[:, None]
    o = (jnp.einsum('bhnij,bhnje->bhnie', scores, vc)
         + jnp.einsum('bhnid,hi,nbhde->bhnie', qc, q_decay, s_prev))
    return o.reshape(b, h, n, dv), s_last


def retention_group(pl, pc, ret_decay, with_ctx_out):
    def heads(t):
        return t.reshape(t.shape[0], t.shape[1], GROUP_HEADS, HEAD_DIM).transpose(0, 2, 1, 3).astype(F32)
    scale = HEAD_DIM ** -0.5
    ang = retention_angles(pl['ret_q'].shape[1], HEAD_DIM)
    ql = rotate(heads(pl['ret_q']) * scale, ang)
    kl = rotate(heads(pl['ret_k']), ang)
    vl = heads(pl['ret_v'])
    qc, kc, vc = heads(pc['ret_q']) * scale, heads(pc['ret_k']), heads(pc['ret_v'])
    log_g = jax.nn.log_sigmoid(ret_decay.astype(F32))
    s0 = jnp.zeros((ql.shape[0], GROUP_HEADS, HEAD_DIM, HEAD_DIM), F32)
    o_l, o_c = 0.0, 0.0
    for dr in range(2):
        f = flip_seq if dr else (lambda t: t)
        oc, s_ctx = retention_scan(f(qc), f(kc), f(vc), log_g[dr], s0, with_ctx_out)
        ol, _ = retention_scan(f(ql), f(kl), f(vl), log_g[dr], s_ctx, True)
        o_l = o_l + f(ol)
        if with_ctx_out:
            o_c = o_c + f(oc)

    def finish(o, gate):
        y = rms_norm(o).transpose(0, 2, 1, 3).reshape(gate.shape)
        return (jax.nn.silu(gate.astype(F32)) * y).astype(gate.dtype)

    return finish(o_l, pl['ret_g']), (finish(o_c, pc['ret_g']) if with_ctx_out else None)


def diff_attention_group(pl, pc, lam_params, norm_g, layer_idx, ang_axial, with_ctx_out):
    h, dq = GROUP_HEADS, DIFF_QK_DIM
    scale = dq ** -0.5
    qk = lambda t: t.reshape(t.shape[0], t.shape[1], h, 2, dq)
    vv = lambda t: t.reshape(t.shape[0], t.shape[1], h, HEAD_DIM)
    ang = ang_axial[:, None, None, :]
    ql, kl = rotate(qk(pl['diff_q']), ang), rotate(qk(pl['diff_k']), ang)
    qc, kc = qk(pc['diff_q']), qk(pc['diff_k'])
    vl, vc = vv(pl['diff_v']), vv(pc['diff_v'])
    lam_init = 0.8 - 0.6 * math.exp(-0.3 * layer_idx)
    lp = lam_params.astype(F32)
    lam = jnp.exp(jnp.sum(lp[0] * lp[1])) - jnp.exp(jnp.sum(lp[2] * lp[3])) + lam_init

    def attend(q, k, v):
        s = jnp.einsum('bqhcd,bkhcd->bhcqk', q, k, preferred_element_type=F32) * scale
        p = jax.nn.softmax(s, -1)
        a = p[:, :, 0] - lam * p[:, :, 1]
        return jnp.einsum('bhqk,bkhe->bqhe', a.astype(v.dtype), v)

    k_all = jnp.concatenate([kl, kc], 1)
    v_all = jnp.concatenate([vl, vc], 1)
    b, n = ql.shape[:2]
    nb = n // Q_BLOCK
    q_blocks = ql.reshape(b, nb, Q_BLOCK, h, 2, dq).swapaxes(0, 1)
    o_l = lax.map(lambda qb: attend(qb, k_all, v_all), q_blocks)
    o_l = o_l.swapaxes(0, 1).reshape(b, n, h, HEAD_DIM)

    def finish(o):
        y = rms_norm(o, norm_g) * (1.0 - lam_init)
        return y.reshape(o.shape[0], o.shape[1], GROUP_W).astype(pl['diff_v'].dtype)

    return finish(o_l), (finish(attend(qc, kc, vc)) if with_ctx_out else None)


def short_conv(x, w):
    return lax.conv_general_dilated(
        x, w[:, None, :].astype(x.dtype), window_strides=(1,),
        padding=[(SHORT_CONV // 2, SHORT_CONV // 2)],
        dimension_numbers=('NWC', 'WIO', 'NWC'), feature_group_count=x.shape[-1])


def gdn_scan(q, k, v, log_a, beta, s0, with_out):
    b, h, n, dk = q.shape
    dv = v.shape[-1]
    c = GDN_CHUNK
    nc = n // c
    q, k, v = (t.reshape(b, h, nc, c, t.shape[-1]) for t in (q, k, v))
    log_a, beta = (t.reshape(b, h, nc, c) for t in (log_a, beta))
    g = jnp.cumsum(log_a, -1)
    idx = jnp.arange(c)
    incl = idx[:, None] >= idx[None, :]
    strict = idx[:, None] > idx[None, :]
    decay = jnp.exp(jnp.where(incl, g[..., :, None] - g[..., None, :], -jnp.inf))
    a_mat = jnp.where(strict, beta[..., :, None] * jnp.einsum('bhnid,bhnjd->bhnij', k, k) * decay, 0.0)
    rhs = jnp.concatenate([beta[..., None] * v, (beta * jnp.exp(g))[..., None] * k], -1)
    sol = lax.linalg.triangular_solve(a_mat + jnp.eye(c, dtype=a_mat.dtype), rhs, left_side=True, lower=True)
    u_base, w = sol[..., :dv], sol[..., dv:]
    k_tail = k * jnp.exp(g[..., -1:] - g)[..., None]
    chunk_decay = jnp.exp(g[..., -1])
    mv = lambda t: jnp.moveaxis(t, 2, 0)
    xs = (mv(u_base), mv(w), mv(k_tail), mv(chunk_decay))
    if with_out:
        qk = jnp.einsum('bhnid,bhnjd->bhnij', q, k) * decay
        xs = xs + (mv(qk), mv(q * jnp.exp(g)[..., None]))

    def step(s, xc):
        u = xc[0] - jnp.einsum('bhck,bhkv->bhcv', xc[1], s)
        s_new = s * xc[3][..., None, None] + jnp.einsum('bhck,bhcv->bhkv', xc[2], u)
        if not with_out:
            return s_new, None
        o = jnp.einsum('bhck,bhkv->bhcv', xc[5], s) + jnp.einsum('bhcj,bhjv->bhcv', xc[4], u)
        return s_new, o

    s_last, o = lax.scan(step, s0, xs)
    if not with_out:
        return None, s_last
    return jnp.moveaxis(o, 0, 2).reshape(b, h, n, dv), s_last


def gdn_group(pl, pc, conv_w, a_log, dt_bias, norm_g, with_ctx_out):
    def prep(p):
        b, n, _ = p['gdn_qkv'].shape
        qkv = jax.nn.silu(short_conv(p['gdn_qkv'], conv_w)).astype(F32)
        qkv = qkv.reshape(b, n, 3, GROUP_HEADS, HEAD_DIM).transpose(2, 0, 3, 1, 4)
        q = l2_normalize(qkv[0]) * HEAD_DIM ** -0.5
        k = l2_normalize(qkv[1])
        a = p['gdn_a'].astype(F32).reshape(b, n, 2, GROUP_HEADS).transpose(2, 0, 3, 1)
        log_a = -jnp.exp(a_log.astype(F32))[:, None, :, None] * jax.nn.softplus(
            a + dt_bias.astype(F32)[:, None, :, None])
        beta = jax.nn.sigmoid(p['gdn_b'].astype(F32).reshape(b, n, 2, GROUP_HEADS).transpose(2, 0, 3, 1))
        return q, k, qkv[2], log_a, beta

    ql, kl, vl, la_l, bt_l = prep(pl)
    qc, kc, vc, la_c, bt_c = prep(pc)
    s0 = jnp.zeros((ql.shape[0], GROUP_HEADS, HEAD_DIM, HEAD_DIM), F32)
    o_l, o_c = 0.0, 0.0
    for dr in range(2):
        f = flip_seq if dr else (lambda t: t)
        oc, s_ctx = gdn_scan(f(qc), f(kc), f(vc), f(la_c[dr]), f(bt_c[dr]), s0, with_ctx_out)
        ol, _ = gdn_scan(f(ql), f(kl), f(vl), f(la_l[dr]), f(bt_l[dr]), s_ctx, True)
        o_l = o_l + f(ol)
        if with_ctx_out:
            o_c = o_c + f(oc)

    def finish(o, gate):
        o = o.transpose(0, 2, 1, 3)
        gt = gate.astype(F32).reshape(o.shape)
        return (rms_norm(o, norm_g) * jax.nn.silu(gt)).reshape(gate.shape).astype(gate.dtype)

    return finish(o_l, pl['gdn_g']), (finish(o_c, pc['gdn_g']) if with_ctx_out else None)


def swa_group(pl, pc, sink, ang_axial, with_ctx_out):
    hkv = SWA_KV_HEADS
    grp = GROUP_HEADS // hkv
    scale = HEAD_DIM ** -0.5
    b, n, _ = pl['swa_q'].shape
    lc = pc['swa_q'].shape[1]
    ql = rotate(pl['swa_q'].reshape(b, n, hkv, grp, HEAD_DIM), ang_axial[:, None, None, :])
    kl = rotate(pl['swa_k'].reshape(b, n, hkv, HEAD_DIM), ang_axial[:, None, :])
    vl = pl['swa_v'].reshape(b, n, hkv, HEAD_DIM)
    qc = pc['swa_q'].reshape(b, lc, hkv, grp, HEAD_DIM)
    kc = pc['swa_k'].reshape(b, lc, hkv, HEAD_DIM)
    vc = pc['swa_v'].reshape(b, lc, hkv, HEAD_DIM)
    sink = sink.astype(F32).reshape(hkv, grp)[:, :, None, None]
    blk = WINDOW
    nb = n // blk

    def band(t):
        tp = jnp.pad(t, ((0, 0), (blk, blk), (0, 0), (0, 0))).reshape(b, nb + 2, blk, hkv, HEAD_DIM)
        return jnp.concatenate([tp[:, :-2], tp[:, 1:-1], tp[:, 2:]], axis=2)

    kb, vb = band(kl), band(vl)
    qb = ql.reshape(b, nb, blk, hkv, grp, HEAD_DIM)
    qpos = jnp.arange(nb)[:, None] * blk + jnp.arange(blk)[None]
    kpos = jnp.arange(nb)[:, None] * blk - blk + jnp.arange(3 * blk)[None]
    valid = ((jnp.abs(qpos[:, :, None] - kpos[:, None, :]) <= WINDOW)
             & (kpos[:, None, :] >= 0) & (kpos[:, None, :] < n))
    s_band = jnp.einsum('bnqhgd,bnkhd->bnhgqk', qb, kb, preferred_element_type=F32) * scale
    s_band = jnp.where(valid[None, :, None, None], s_band, -jnp.inf)
    s_ctx = jnp.einsum('bnqhgd,bkhd->bnhgqk', qb, kc, preferred_element_type=F32) * scale
    p = sink_softmax(jnp.concatenate([s_band, s_ctx], -1), sink)
    o = (jnp.einsum('bnhgqk,bnkhd->bnqhgd', p[..., :3 * blk].astype(vb.dtype), vb)
         + jnp.einsum('bnhgqk,bkhd->bnqhgd', p[..., 3 * blk:].astype(vc.dtype), vc))
    y_l = o.reshape(b, n, GROUP_W)
    y_c = None
    if with_ctx_out:
        sc = jnp.einsum('bqhgd,bkhd->bhgqk', qc, kc, preferred_element_type=F32) * scale
        pcx = sink_softmax(sc, sink)
        y_c = jnp.einsum('bhgqk,bkhd->bqhgd', pcx.astype(vc.dtype), vc).reshape(b, lc, GROUP_W)
    return y_l, y_c


def expert_choice_ffn(u, router_w, w_gate, w_up, w_down):
    b, n, d = u.shape
    cap = CAPACITY_FACTOR * n // N_EXPERTS
    aff = jax.nn.softmax(jnp.einsum('bnd,de->bne', u, router_w, preferred_element_type=F32), -1)
    weight, idx = lax.top_k(jnp.swapaxes(aff, 1, 2), cap)
    xin = jax.vmap(lambda ub, ib: ub[ib])(u, idx)
    hid = jax.nn.silu(jnp.einsum('becd,edf->becf', xin, w_gate)) * jnp.einsum('becd,edf->becf', xin, w_up)
    y = jnp.einsum('becf,efd->becd', hid, w_down) * weight[..., None].astype(u.dtype)
    flat = (jnp.arange(b)[:, None, None] * n + idx).reshape(-1)
    return jnp.zeros((b * n, d), u.dtype).at[flat].add(y.reshape(-1, d)).reshape(b, n, d)


def setup_inputs(seed: int = 0) -> dict:
    key = jax.random.key(seed)
    ks = jax.random.split(key, 24)
    D = D_MODEL
    nrm = lambda k, shape, s: jax.random.normal(k, shape, F32) * s
    x = nrm(ks[0], (BATCH, SEQ, D), 1.0)
    c = nrm(ks[1], (BATCH, D), 1.0)
    ctx = nrm(ks[2], (BATCH, CTX_LEN, D), 1.0)
    c_ctx = nrm(ks[3], (D,), 1.0)
    ada_w = nrm(ks[4], (DEPTH, D, 6 * D), 0.5 * D ** -0.5)
    ada_b = nrm(ks[5], (DEPTH, 6 * D), 0.02)
    w_in = nrm(ks[6], (DEPTH, D, IN_W), D ** -0.5)
    w_out = nrm(ks[7], (DEPTH, MIX_W, D), DEEPNORM_BETA * MIX_W ** -0.5)
    gamma_logit = jnp.log(2.0 ** (5.0 + jnp.arange(GROUP_HEADS, dtype=F32)) - 1.0)
    ret_decay = gamma_logit + nrm(ks[8], (DEPTH, 2, GROUP_HEADS), 0.1)
    diff_lambda = nrm(ks[9], (DEPTH, 4, DIFF_QK_DIM), 0.1)
    diff_norm = 1.0 + nrm(ks[10], (DEPTH, HEAD_DIM), 0.02)
    gdn_conv = nrm(ks[11], (DEPTH, SHORT_CONV, 3 * GROUP_W), SHORT_CONV ** -0.5)
    gdn_a_log = jnp.log(jax.random.uniform(ks[12], (DEPTH, 2, GROUP_HEADS), F32, 1.0, 16.0))
    dt = jnp.exp(jax.random.uniform(ks[13], (DEPTH, 2, GROUP_HEADS), F32, math.log(1e-3), math.log(1e-1)))
    gdn_dt_bias = dt + jnp.log(-jnp.expm1(-dt))
    gdn_norm = 1.0 + nrm(ks[14], (DEPTH, HEAD_DIM), 0.02)
    swa_sink = nrm(ks[15], (DEPTH, GROUP_HEADS), 0.5)
    ln_g = 1.0 + nrm(ks[16], (DEPTH, 2, D), 0.02)
    ln_b = nrm(ks[17], (DEPTH, 2, D), 0.02)
    router_w = nrm(ks[18], (DEPTH, D, N_EXPERTS), D ** -0.5)
    w_gate = nrm(ks[19], (DEPTH, N_EXPERTS, D, EXPERT_FF), D ** -0.5)
    w_up = nrm(ks[20], (DEPTH, N_EXPERTS, D, EXPERT_FF), D ** -0.5)
    w_down = nrm(ks[21], (DEPTH, N_EXPERTS, EXPERT_FF, D), DEEPNORM_BETA * EXPERT_FF ** -0.5)
    return {'x': x, 'c': c, 'ctx': ctx, 'c_ctx': c_ctx, 'ada_w': ada_w, 'ada_b': ada_b,
            'w_in': w_in, 'w_out': w_out, 'ret_decay': ret_decay, 'diff_lambda': diff_lambda,
            'diff_norm': diff_norm, 'gdn_conv': gdn_conv, 'gdn_a_log': gdn_a_log,
            'gdn_dt_bias': gdn_dt_bias, 'gdn_norm': gdn_norm, 'swa_sink': swa_sink,
            'ln_g': ln_g, 'ln_b': ln_b, 'router_w': router_w, 'w_gate': w_gate,
            'w_up': w_up, 'w_down': w_down}


def reference(x, c, ctx, c_ctx, ada_w, ada_b, w_in, w_out, ret_decay, diff_lambda, diff_norm,
              gdn_conv, gdn_a_log, gdn_dt_bias, gdn_norm, swa_sink, ln_g, ln_b,
              router_w, w_gate, w_up, w_down):
    b, n, d = x.shape
    ang_diff = axial_rope_angles(n, DIFF_QK_DIM)
    ang_swa = axial_rope_angles(n, HEAD_DIM)
    cond_l = jax.nn.silu(c)
    cond_c = jax.nn.silu(c_ctx)
    h, hc = x, ctx
    for layer in range(DEPTH):
        full_ctx = layer < DEPTH - 1
        mod_l = (cond_l @ ada_w[layer] + ada_b[layer]).reshape(b, 1, 6, d)
        mod_c = (cond_c @ ada_w[layer] + ada_b[layer]).reshape(1, 1, 6, d)
        u_l = h * (1.0 + mod_l[:, :, 1]) + mod_l[:, :, 0]
        u_c = hc * (1.0 + mod_c[:, :, 1]) + mod_c[:, :, 0]
        pl = split_cols(u_l @ w_in[layer])
        pc = split_cols(u_c @ w_in[layer])
        ya_l, ya_c = retention_group(pl, pc, ret_decay[layer], full_ctx)
        yb_l, yb_c = diff_attention_group(pl, pc, diff_lambda[layer], diff_norm[layer], layer, ang_diff, full_ctx)
        yc_l, yc_c = gdn_group(pl, pc, gdn_conv[layer], gdn_a_log[layer], gdn_dt_bias[layer], gdn_norm[layer], full_ctx)
        yd_l, yd_c = swa_group(pl, pc, swa_sink[layer], ang_swa, full_ctx)
        mix_l = jnp.concatenate([ya_l, yb_l, yc_l, yd_l], -1) @ w_out[layer]
        h = layer_norm(DEEPNORM_ALPHA * h + mod_l[:, :, 2] * mix_l, ln_g[layer, 0], ln_b[layer, 0])
        if full_ctx:
            mix_c = jnp.concatenate([ya_c, yb_c, yc_c, yd_c], -1) @ w_out[layer]
            hc = layer_norm(DEEPNORM_ALPHA * hc + mod_c[:, :, 2] * mix_c, ln_g[layer, 0], ln_b[layer, 0])
        f_l = expert_choice_ffn(h * (1.0 + mod_l[:, :, 4]) + mod_l[:, :, 3],
                                router_w[layer], w_gate[layer], w_up[layer], w_down[layer])
        h = layer_norm(DEEPNORM_ALPHA * h + mod_l[:, :, 5] * f_l, ln_g[layer, 1], ln_b[layer, 1])
        if full_ctx:
            f_c = expert_choice_ffn(hc * (1.0 + mod_c[:, :, 4]) + mod_c[:, :, 3],
                                    router_w[layer], w_gate[layer], w_up[layer], w_down[layer])
            hc = layer_norm(DEEPNORM_ALPHA * hc + mod_c[:, :, 5] * f_c, ln_g[layer, 1], ln_b[layer, 1])
    return h
```

```python
import functools
import math

import jax
import jax.numpy as jnp
from jax import lax
from jax.experimental import pallas as pl
from jax.experimental.pallas import tpu as pltpu

F32 = jnp.float32
BF16 = jnp.bfloat16

GRID_W = 64
HEAD_DIM = 64
N_MIXERS = 4
ROPE_BASE = 10000.0
RET_CHUNK = 128
Q_BLOCK = 128
GDN_CHUNK = 64
SHORT_CONV = 3
WINDOW = 128
N_EXPERTS = 16
CAPACITY_FACTOR = 2
LN_EPS = 1e-5

VMEM_LIMIT_BYTES = 56 * 1024 * 1024


def _cparams(*sem):
    return pltpu.CompilerParams(dimension_semantics=sem, vmem_limit_bytes=VMEM_LIMIT_BYTES)


def _inproj_kernel(h_ref, mod_ref, w_ref, o_ref, *, tn):
    u = (h_ref[...] * (1.0 + mod_ref[1:2, :]) + mod_ref[0:1, :]).astype(BF16)
    for j in range(o_ref.shape[-1] // tn):
        o_ref[:, j * tn:(j + 1) * tn] = jnp.dot(
            u, w_ref[:, j * tn:(j + 1) * tn], preferred_element_type=F32)


def inproj(h, mod, w_bf16, tm):
    b, n, d = h.shape
    nw = w_bf16.shape[1]
    tn = 384 if nw % 384 == 0 else nw
    return pl.pallas_call(
        functools.partial(_inproj_kernel, tn=tn),
        grid=(b, n // tm),
        in_specs=[
            pl.BlockSpec((None, tm, d), lambda i, j: (i, j, 0)),
            pl.BlockSpec((None, 6, d), lambda i, j: (i, 0, 0)),
            pl.BlockSpec((d, nw), lambda i, j: (0, 0)),
        ],
        out_specs=pl.BlockSpec((None, tm, nw), lambda i, j: (i, j, 0)),
        out_shape=jax.ShapeDtypeStruct((b, n, nw), F32),
        compiler_params=_cparams("parallel", "parallel"),
    )(h, mod, w_bf16)


def _layer_norm(x, g, b):
    mu = jnp.mean(x, -1, keepdims=True)
    xc = x - mu
    var = jnp.mean(xc * xc, -1, keepdims=True)
    return xc * lax.rsqrt(var + LN_EPS) * g + b


def _outproj_kernel(y_ref, h_ref, mod_ref, w_ref, ln_ref, rw_ref, hn_ref, u2_ref, aff_ref, *, alpha):
    mix = jnp.dot(y_ref[...], w_ref[...], preferred_element_type=F32)
    hn = _layer_norm(alpha * h_ref[...] + mod_ref[2:3, :] * mix, ln_ref[0:1, :], ln_ref[1:2, :])
    hn_ref[...] = hn
    u2 = (hn * (1.0 + mod_ref[4:5, :]) + mod_ref[3:4, :]).astype(BF16)
    u2_ref[...] = u2
    logits = lax.dot_general(rw_ref[...], u2, (((1,), (1,)), ((), ())), preferred_element_type=F32)
    m = jnp.max(logits, 0, keepdims=True)
    e = jnp.exp(logits - m)
    aff_ref[...] = e / jnp.sum(e, 0, keepdims=True)


def outproj(y, h, mod, w_bf16, ln, rw_t_bf16, alpha, tm):
    b, n, d = h.shape
    mixw = y.shape[-1]
    ne = rw_t_bf16.shape[0]
    return pl.pallas_call(
        functools.partial(_outproj_kernel, alpha=alpha),
        grid=(b, n // tm),
        in_specs=[
            pl.BlockSpec((None, tm, mixw), lambda i, j: (i, j, 0)),
            pl.BlockSpec((None, tm, d), lambda i, j: (i, j, 0)),
            pl.BlockSpec((None, 6, d), lambda i, j: (i, 0, 0)),
            pl.BlockSpec((mixw, d), lambda i, j: (0, 0)),
            pl.BlockSpec((2, d), lambda i, j: (0, 0)),
            pl.BlockSpec((ne, d), lambda i, j: (0, 0)),
        ],
        out_specs=[
            pl.BlockSpec((None, tm, d), lambda i, j: (i, j, 0)),
            pl.BlockSpec((None, tm, d), lambda i, j: (i, j, 0)),
            pl.BlockSpec((None, ne, tm), lambda i, j: (i, 0, j)),
        ],
        out_shape=[
            jax.ShapeDtypeStruct((b, n, d), F32),
            jax.ShapeDtypeStruct((b, n, d), BF16),
            jax.ShapeDtypeStruct((b, ne, n), F32),
        ],
        compiler_params=_cparams("parallel", "parallel"),
    )(y, h, mod, w_bf16, ln, rw_t_bf16)


def _ffn_kernel(x_ref, wg_ref, wu_ref, wd_ref, o_ref, *, rows):
    f = pl.program_id(2)
    tb, cap, d = x_ref.shape

    @pl.when(f == 0)
    def _():
        o_ref[...] = jnp.zeros_like(o_ref)

    wg = wg_ref[...].astype(BF16)
    wu = wu_ref[...].astype(BF16)
    wd = wd_ref[...].astype(BF16)
    for bi in range(tb):
        for r in range(cap // rows):
            x = x_ref[bi, r * rows:(r + 1) * rows, :]
            g = jnp.dot(x, wg, preferred_element_type=F32)
            u = jnp.dot(x, wu, preferred_element_type=F32)
            hid = (g * jax.nn.sigmoid(g) * u).astype(BF16)
            o_ref[bi, r * rows:(r + 1) * rows, :] += jnp.dot(hid, wd, preferred_element_type=F32)


def expert_ffn(xin, w_gate, w_up, w_down, tb, tf, rows):
    b, ne, cap, d = xin.shape
    ff = w_gate.shape[-1]
    return pl.pallas_call(
        functools.partial(_ffn_kernel, rows=rows),
        grid=(ne, b // tb, ff // tf),
        in_specs=[
            pl.BlockSpec((tb, None, cap, d), lambda e, i, f: (i, e, 0, 0)),
            pl.BlockSpec((None, d, tf), lambda e, i, f: (e, 0, f)),
            pl.BlockSpec((None, d, tf), lambda e, i, f: (e, 0, f)),
            pl.BlockSpec((None, tf, d), lambda e, i, f: (e, f, 0)),
        ],
        out_specs=pl.BlockSpec((tb, None, cap, d), lambda e, i, f: (i, e, 0, 0)),
        out_shape=jax.ShapeDtypeStruct((b, ne, cap, d), F32),
        compiler_params=_cparams("parallel", "parallel", "arbitrary"),
    )(xin, w_gate, w_up, w_down)


def layer_norm(x, g, b):
    mu = jnp.mean(x, -1, keepdims=True)
    var = jnp.mean(jnp.square(x - mu), -1, keepdims=True)
    return (x - mu) * lax.rsqrt(var + LN_EPS) * g + b


def rms_norm(x, g=None, eps=1e-6):
    y = x * lax.rsqrt(jnp.mean(jnp.square(x), -1, keepdims=True) + eps)
    if g is not None:
        y = y * g
    return y


def l2_normalize(x, eps=1e-6):
    return x * lax.rsqrt(jnp.sum(jnp.square(x), -1, keepdims=True) + eps)


def axial_rope_angles(n, rot_dim):
    rows = n // GRID_W
    row = jnp.repeat(jnp.arange(rows, dtype=F32), GRID_W)
    col = jnp.tile(jnp.arange(GRID_W, dtype=F32), rows)
    n_freq = rot_dim // 4
    inv = ROPE_BASE ** (-jnp.arange(n_freq, dtype=F32) / n_freq)
    return jnp.concatenate([row[:, None] * inv, col[:, None] * inv], -1)


def retention_angles(n, dim):
    theta = 1.0 / (ROPE_BASE ** jnp.linspace(0.0, 1.0, dim // 2, dtype=F32))
    return jnp.arange(n, dtype=F32)[:, None] * theta


def rotate(x, ang):
    half = x.shape[-1] // 2
    cos, sin = jnp.cos(ang).astype(x.dtype), jnp.sin(ang).astype(x.dtype)
    x1, x2 = x[..., :half], x[..., half:]
    return jnp.concatenate([x1 * cos - x2 * sin, x1 * sin + x2 * cos], -1)


def flip_seq(t):
    return jnp.flip(t, axis=2)


def sink_softmax(s, sink):
    full = jnp.concatenate([s, jnp.broadcast_to(sink, s.shape[:-1] + (1,)).astype(s.dtype)], -1)
    return jax.nn.softmax(full, -1)[..., :-1]


def retention_scan(q, k, v, log_g, s0, with_out):
    b, h, n, dk = q.shape
    dv = v.shape[-1]
    c = RET_CHUNK
    nc = n // c
    qc, kc, vc = (t.reshape(b, h, nc, c, t.shape[-1]) for t in (q, k, v))
    pos = jnp.arange(c, dtype=F32)
    lg = log_g[:, None]
    k_decay = jnp.exp(lg * (c - 1 - pos))
    kv = jnp.einsum('bhncd,hc,bhnce->nbhde', kc, k_decay, vc)
    chunk_decay = jnp.exp(log_g * c)[None, :, None, None]

    def step(s, kv_n):
        return s * chunk_decay + kv_n, (s if with_out else None)

    s_last, s_prev = lax.scan(step, s0, kv)
    if not with_out:
        return None, s_last
    rel = pos[:, None] - pos[None, :]
    d_mat = jnp.where(rel >= 0, jnp.exp(lg[:, :, None] * jnp.maximum(rel, 0.0)), 0.0)
    q_decay = jnp.exp(lg * (pos + 1.0))
    scores = jnp.einsum('bhnid,bhnjd->bhnij', qc, kc) * d_mat[:, None]
    o = (jnp.einsum('bhnij,bhnje->bhnie', scores, vc)
         + jnp.einsum('bhnid,hi,nbhde->bhnie', qc, q_decay, s_prev))
    return o.reshape(b, h, n, dv), s_last


def retention_group(p_l, p_c, ret_decay, with_ctx_out, gh):
    def heads(t):
        return t.reshape(t.shape[0], t.shape[1], gh, HEAD_DIM).transpose(0, 2, 1, 3)
    scale = HEAD_DIM ** -0.5
    ang = retention_angles(p_l['ret_q'].shape[1], HEAD_DIM)
    ql = rotate(heads(p_l['ret_q']) * scale, ang)
    kl = rotate(heads(p_l['ret_k']), ang)
    vl = heads(p_l['ret_v'])
    qc, kc, vc = heads(p_c['ret_q']) * scale, heads(p_c['ret_k']), heads(p_c['ret_v'])
    log_g = jax.nn.log_sigmoid(ret_decay)
    s0 = jnp.zeros((ql.shape[0], gh, HEAD_DIM, HEAD_DIM), F32)
    o_l, o_c = 0.0, 0.0
    for dr in range(2):
        f = flip_seq if dr else (lambda t: t)
        oc, s_ctx = retention_scan(f(qc), f(kc), f(vc), log_g[dr], s0, with_ctx_out)
        ol, _ = retention_scan(f(ql), f(kl), f(vl), log_g[dr], s_ctx, True)
        o_l = o_l + f(ol)
        if with_ctx_out:
            o_c = o_c + f(oc)

    def finish(o, gate):
        y = rms_norm(o).transpose(0, 2, 1, 3).reshape(gate.shape)
        return jax.nn.silu(gate) * y

    return finish(o_l, p_l['ret_g']), (finish(o_c, p_c['ret_g']) if with_ctx_out else None)


def diff_attention_group(p_l, p_c, lam_params, norm_g, layer_idx, ang_axial, with_ctx_out, gh):
    h, dq = gh, HEAD_DIM // 2
    scale = dq ** -0.5
    qk = lambda t: t.reshape(t.shape[0], t.shape[1], h, 2, dq)
    vv = lambda t: t.reshape(t.shape[0], t.shape[1], h, HEAD_DIM)
    ang = ang_axial[:, None, None, :]
    ql, kl = rotate(qk(p_l['diff_q']), ang), rotate(qk(p_l['diff_k']), ang)
    qc, kc = qk(p_c['diff_q']), qk(p_c['diff_k'])
    vl, vc = vv(p_l['diff_v']), vv(p_c['diff_v'])
    lam_init = 0.8 - 0.6 * math.exp(-0.3 * layer_idx)
    lp = lam_params
    lam = jnp.exp(jnp.sum(lp[0] * lp[1])) - jnp.exp(jnp.sum(lp[2] * lp[3])) + lam_init

    def attend(q, k, v):
        s = jnp.einsum('bqhcd,bkhcd->bhcqk', q, k, preferred_element_type=F32) * scale
        p = jax.nn.softmax(s, -1)
        a = p[:, :, 0] - lam * p[:, :, 1]
        return jnp.einsum('bhqk,bkhe->bqhe', a, v)

    k_all = jnp.concatenate([kl, kc], 1)
    v_all = jnp.concatenate([vl, vc], 1)
    b, n = ql.shape[:2]
    nb = n // Q_BLOCK
    q_blocks = ql.reshape(b, nb, Q_BLOCK, h, 2, dq).swapaxes(0, 1)
    o_l = lax.map(lambda qb: attend(qb, k_all, v_all), q_blocks)
    o_l = o_l.swapaxes(0, 1).reshape(b, n, h, HEAD_DIM)

    def finish(o):
        y = rms_norm(o, norm_g) * (1.0 - lam_init)
        return y.reshape(o.shape[0], o.shape[1], h * HEAD_DIM)

    return finish(o_l), (finish(attend(qc, kc, vc)) if with_ctx_out else None)


def short_conv(x, w):
    return lax.conv_general_dilated(
        x, w[:, None, :], window_strides=(1,),
        padding=[(SHORT_CONV // 2, SHORT_CONV // 2)],
        dimension_numbers=('NWC', 'WIO', 'NWC'), feature_group_count=x.shape[-1])


def gdn_scan(q, k, v, log_a, beta, s0, with_out):
    b, h, n, dk = q.shape
    dv = v.shape[-1]
    c = GDN_CHUNK
    nc = n // c
    q, k, v = (t.reshape(b, h, nc, c, t.shape[-1]) for t in (q, k, v))
    log_a, beta = (t.reshape(b, h, nc, c) for t in (log_a, beta))
    g = jnp.cumsum(log_a, -1)
    idx = jnp.arange(c)
    incl = idx[:, None] >= idx[None, :]
    strict = idx[:, None] > idx[None, :]
    decay = jnp.exp(jnp.where(incl, g[..., :, None] - g[..., None, :], -jnp.inf))
    a_mat = jnp.where(strict, beta[..., :, None] * jnp.einsum('bhnid,bhnjd->bhnij', k, k) * decay, 0.0)
    rhs = jnp.concatenate([beta[..., None] * v, (beta * jnp.exp(g))[..., None] * k], -1)
    sol = lax.linalg.triangular_solve(a_mat + jnp.eye(c, dtype=a_mat.dtype), rhs, left_side=True, lower=True)
    u_base, w = sol[..., :dv], sol[..., dv:]
    k_tail = k * jnp.exp(g[..., -1:] - g)[..., None]
    chunk_decay = jnp.exp(g[..., -1])
    mv = lambda t: jnp.moveaxis(t, 2, 0)
    xs = (mv(u_base), mv(w), mv(k_tail), mv(chunk_decay))
    if with_out:
        qk = jnp.einsum('bhnid,bhnjd->bhnij', q, k) * decay
        xs = xs + (mv(qk), mv(q * jnp.exp(g)[..., None]))

    def step(s, xc):
        u = xc[0] - jnp.einsum('bhck,bhkv->bhcv', xc[1], s)
        s_new = s * xc[3][..., None, None] + jnp.einsum('bhck,bhcv->bhkv', xc[2], u)
        if not with_out:
            return s_new, None
        o = jnp.einsum('bhck,bhkv->bhcv', xc[5], s) + jnp.einsum('bhcj,bhjv->bhcv', xc[4], u)
        return s_new, o

    s_last, o = lax.scan(step, s0, xs)
    if not with_out:
        return None, s_last
    return jnp.moveaxis(o, 0, 2).reshape(b, h, n, dv), s_last


def gdn_group(p_l, p_c, conv_w, a_log, dt_bias, norm_g, with_ctx_out, gh):
    def prep(p):
        b, n, _ = p['gdn_qkv'].shape
        qkv = jax.nn.silu(short_conv(p['gdn_qkv'], conv_w))
        qkv = qkv.reshape(b, n, 3, gh, HEAD_DIM).transpose(2, 0, 3, 1, 4)
        q = l2_normalize(qkv[0]) * HEAD_DIM ** -0.5
        k = l2_normalize(qkv[1])
        a = p['gdn_a'].reshape(b, n, 2, gh).transpose(2, 0, 3, 1)
        log_a = -jnp.exp(a_log)[:, None, :, None] * jax.nn.softplus(a + dt_bias[:, None, :, None])
        beta = jax.nn.sigmoid(p['gdn_b'].reshape(b, n, 2, gh).transpose(2, 0, 3, 1))
        return q, k, qkv[2], log_a, beta

    ql, kl, vl, la_l, bt_l = prep(p_l)
    qc, kc, vc, la_c, bt_c = prep(p_c)
    s0 = jnp.zeros((ql.shape[0], gh, HEAD_DIM, HEAD_DIM), F32)
    o_l, o_c = 0.0, 0.0
    for dr in range(2):
        f = flip_seq if dr else (lambda t: t)
        oc, s_ctx = gdn_scan(f(qc), f(kc), f(vc), f(la_c[dr]), f(bt_c[dr]), s0, with_ctx_out)
        ol, _ = gdn_scan(f(ql), f(kl), f(vl), f(la_l[dr]), f(bt_l[dr]), s_ctx, True)
        o_l = o_l + f(ol)
        if with_ctx_out:
            o_c = o_c + f(oc)

    def finish(o, gate):
        o = o.transpose(0, 2, 1, 3)
        gt = gate.reshape(o.shape)
        return (rms_norm(o, norm_g) * jax.nn.silu(gt)).reshape(gate.shape)

    return finish(o_l, p_l['gdn_g']), (finish(o_c, p_c['gdn_g']) if with_ctx_out else None)


def swa_group(p_l, p_c, sink, ang_axial, with_ctx_out, gh):
    hkv = gh // 2
    grp = gh // hkv
    scale = HEAD_DIM ** -0.5
    b, n, _ = p_l['swa_q'].shape
    lc = p_c['swa_q'].shape[1]
    ql = rotate(p_l['swa_q'].reshape(b, n, hkv, grp, HEAD_DIM), ang_axial[:, None, None, :])
    kl = rotate(p_l['swa_k'].reshape(b, n, hkv, HEAD_DIM), ang_axial[:, None, :])
    vl = p_l['swa_v'].reshape(b, n, hkv, HEAD_DIM)
    qc = p_c['swa_q'].reshape(b, lc, hkv, grp, HEAD_DIM)
    kc = p_c['swa_k'].reshape(b, lc, hkv, HEAD_DIM)
    vc = p_c['swa_v'].reshape(b, lc, hkv, HEAD_DIM)
    sink = sink.reshape(hkv, grp)[:, :, None, None]
    blk = WINDOW
    nb = n // blk

    def band(t):
        tp = jnp.pad(t, ((0, 0), (blk, blk), (0, 0), (0, 0))).reshape(b, nb + 2, blk, hkv, HEAD_DIM)
        return jnp.concatenate([tp[:, :-2], tp[:, 1:-1], tp[:, 2:]], axis=2)

    kb, vb = band(kl), band(vl)
    qb = ql.reshape(b, nb, blk, hkv, grp, HEAD_DIM)
    qpos = jnp.arange(nb)[:, None] * blk + jnp.arange(blk)[None]
    kpos = jnp.arange(nb)[:, None] * blk - blk + jnp.arange(3 * blk)[None]
    valid = ((jnp.abs(qpos[:, :, None] - kpos[:, None, :]) <= WINDOW)
             & (kpos[:, None, :] >= 0) & (kpos[:, None, :] < n))
    s_band = jnp.einsum('bnqhgd,bnkhd->bnhgqk', qb, kb, preferred_element_type=F32) * scale
    s_band = jnp.where(valid[None, :, None, None], s_band, -jnp.inf)
    s_ctx = jnp.einsum('bnqhgd,bkhd->bnhgqk', qb, kc, preferred_element_type=F32) * scale
    p = sink_softmax(jnp.concatenate([s_band, s_ctx], -1), sink)
    o = (jnp.einsum('bnhgqk,bnkhd->bnqhgd', p[..., :3 * blk], vb)
         + jnp.einsum('bnhgqk,bkhd->bnqhgd', p[..., 3 * blk:], vc))
    y_l = o.reshape(b, n, gh * HEAD_DIM)
    y_c = None
    if with_ctx_out:
        sc = jnp.einsum('bqhgd,bkhd->bhgqk', qc, kc, preferred_element_type=F32) * scale
        pcx = sink_softmax(sc, sink)
        y_c = jnp.einsum('bhgqk,bkhd->bqhgd', pcx, vc).reshape(b, lc, gh * HEAD_DIM)
    return y_l, y_c


def expert_choice_ffn(u2, aff_t, w_gate, w_up, w_down):
    b, n, d = u2.shape
    ne = aff_t.shape[1]
    cap = CAPACITY_FACTOR * n // ne
    weight, idx = lax.top_k(aff_t, cap)
    xin = jax.vmap(lambda ub, ib: ub[ib])(u2, idx)
    tb = 4 if cap >= 512 else b
    y = expert_ffn(xin, w_gate, w_up, w_down, tb=tb, tf=512, rows=min(cap, 256))
    y = y * weight[..., None]
    flat = (jnp.arange(b)[:, None, None] * n + idx).reshape(-1)
    return jnp.zeros((b * n, d), F32).at[flat].add(y.reshape(-1, d)).reshape(b, n, d)


def _split_cols(p, gw, gh):
    cols = (('ret_q', gw), ('ret_k', gw), ('ret_v', gw), ('ret_g', gw),
            ('diff_q', gw), ('diff_k', gw), ('diff_v', gw),
            ('gdn_qkv', 3 * gw), ('gdn_g', gw), ('gdn_a', 2 * gh), ('gdn_b', 2 * gh),
            ('swa_q', gw), ('swa_k', gw // 2), ('swa_v', gw // 2))
    out, off = {}, 0
    for name, w in cols:
        out[name] = p[..., off:off + w]
        off += w
    return out


def kernel(x, c, ctx, c_ctx, ada_w, ada_b, w_in, w_out, ret_decay, diff_lambda, diff_norm, gdn_conv,
           gdn_a_log, gdn_dt_bias, gdn_norm, swa_sink, ln_g, ln_b, router_w, w_gate, w_up, w_down):
    b, n, d = x.shape
    lc = ctx.shape[1]
    depth = ada_w.shape[0]
    gh = d // (N_MIXERS * HEAD_DIM)
    gw = gh * HEAD_DIM
    alpha = (2 * depth) ** 0.25
    ang_diff = axial_rope_angles(n, HEAD_DIM // 2)
    ang_swa = axial_rope_angles(n, HEAD_DIM)
    cond_l = jax.nn.silu(c)
    cond_c = jax.nn.silu(c_ctx)
    h, hc = x, ctx
    for layer in range(depth):
        full_ctx = layer < depth - 1
        mod_l = (cond_l @ ada_w[layer] + ada_b[layer]).reshape(b, 6, d)
        mod_c = jnp.broadcast_to((cond_c @ ada_w[layer] + ada_b[layer]).reshape(1, 6, d), (b, 6, d))
        w_in_b = w_in[layer].astype(BF16)
        w_out_b = w_out[layer].astype(BF16)
        rw_t = router_w[layer].T.astype(BF16)
        ln1 = jnp.stack([ln_g[layer, 0], ln_b[layer, 0]])
        ln2g, ln2b = ln_g[layer, 1], ln_b[layer, 1]
        p_l = _split_cols(inproj(h, mod_l, w_in_b, tm=512), gw, gh)
        p_c = _split_cols(inproj(hc, mod_c, w_in_b, tm=lc), gw, gh)
        ya_l, ya_c = retention_group(p_l, p_c, ret_decay[layer], full_ctx, gh)
        yb_l, yb_c = diff_attention_group(p_l, p_c, diff_lambda[layer], diff_norm[layer], layer, ang_diff, full_ctx, gh)
        yc_l, yc_c = gdn_group(p_l, p_c, gdn_conv[layer], gdn_a_log[layer], gdn_dt_bias[layer], gdn_norm[layer], full_ctx, gh)
        yd_l, yd_c = swa_group(p_l, p_c, swa_sink[layer], ang_swa, full_ctx, gh)
        y_l = jnp.concatenate([ya_l, yb_l, yc_l, yd_l], -1).astype(BF16)
        h, u2, aff = outproj(y_l, h, mod_l, w_out_b, ln1, rw_t, alpha, tm=512)
        f_l = expert_choice_ffn(u2, aff, w_gate[layer], w_up[layer], w_down[layer])
        h = layer_norm(alpha * h + mod_l[:, None, 5] * f_l, ln2g, ln2b)
        if full_ctx:
            y_c = jnp.concatenate([ya_c, yb_c, yc_c, yd_c], -1).astype(BF16)
            hc, u2c, affc = outproj(y_c, hc, mod_c, w_out_b, ln1, rw_t, alpha, tm=lc)
            f_c = expert_choice_ffn(u2c, affc, w_gate[layer], w_up[layer], w_down[layer])
            hc = layer_norm(alpha * hc + mod_c[:, None, 5] * f_c, ln2g, ln2b)
    return h
```

```python
import functools
import math

import numpy as np
import jax
import jax.numpy as jnp
from jax import lax
from jax.experimental import pallas as pl
from jax.experimental.pallas import tpu as pltpu

F32 = jnp.float32
BF16 = jnp.bfloat16

GRID_W = 64
HEAD_DIM = 64
N_MIXERS = 4
GROUP_HEADS = 4
GROUP_W = GROUP_HEADS * HEAD_DIM
ROPE_BASE = 10000.0
RET_CHUNK = 128
GDN_CHUNK = 64
SHORT_CONV = 3
WINDOW = 128
N_EXPERTS = 16
CAPACITY_FACTOR = 2
LN_EPS = 1e-5
RMS_EPS = 1e-6
LANES = 128

VMEM_LIMIT_BYTES = 56 * 1024 * 1024

CB_RET_Q, CB_RET_K, CB_RET_V, CB_RET_G = 0, 2, 4, 6
CB_DIFF_Q, CB_DIFF_K, CB_DIFF_V = 8, 10, 12
CB_GDN_QKV, CB_GDN_G = 14, 20
CB_SWA_Q, CB_SWA_K, CB_SWA_V = 22, 24, 25
MAIN_W = 26 * LANES


def _cparams(*sem):
    return pltpu.CompilerParams(dimension_semantics=sem, vmem_limit_bytes=VMEM_LIMIT_BYTES)


def _in_col_perm():
    gw, gh, hd = GROUP_W, GROUP_HEADS, HEAD_DIM
    off = {}
    o = 0
    for name, w in (('ret_q', gw), ('ret_k', gw), ('ret_v', gw), ('ret_g', gw), ('diff_q', gw), ('diff_k', gw),
                    ('diff_v', gw), ('gdn_qkv', 3 * gw), ('gdn_g', gw), ('gdn_a', 2 * gh), ('gdn_b', 2 * gh),
                    ('swa_q', gw), ('swa_k', gw // 2), ('swa_v', gw // 2)):
        off[name] = o
        o += w
    ident = lambda name, w: off[name] + np.arange(w)
    ret = np.zeros(gw, np.int64)
    for h in range(gh):
        for j in range(hd):
            ret[(j // 32) * 128 + h * 32 + j % 32] = h * hd + j
    dif = np.zeros(gw, np.int64)
    for h in range(gh):
        for c in range(2):
            for j in range(32):
                dif[(j // 16) * 128 + (h * 2 + c) * 16 + j % 16] = h * hd + c * 32 + j
    swq = np.zeros(gw, np.int64)
    for hk in range(2):
        for g in range(2):
            for j in range(hd):
                swq[g * 128 + (j // 32) * 64 + hk * 32 + j % 32] = hk * 128 + g * hd + j
    swk = np.zeros(gw // 2, np.int64)
    for hk in range(2):
        for j in range(hd):
            swk[(j // 32) * 64 + hk * 32 + j % 32] = hk * hd + j
    main = np.concatenate([
        off['ret_q'] + ret, off['ret_k'] + ret, ident('ret_v', gw), ident('ret_g', gw),
        off['diff_q'] + dif, off['diff_k'] + dif, ident('diff_v', gw),
        ident('gdn_qkv', 3 * gw), ident('gdn_g', gw),
        off['swa_q'] + swq, off['swa_k'] + swk, ident('swa_v', gw // 2)])
    tail = np.concatenate([ident('gdn_a', 2 * gh), ident('gdn_b', 2 * gh)])
    return main, tail, off


def _inproj_kernel(h_ref, mod_ref, w_ref, o_ref, ab_ref, *, tn):
    u = (h_ref[...] * (1.0 + mod_ref[1:2, :]) + mod_ref[0:1, :]).astype(BF16)
    nmain = o_ref.shape[-1]
    for j in range(nmain // tn):
        o_ref[:, j * tn:(j + 1) * tn] = jnp.dot(
            u, w_ref[:, j * tn:(j + 1) * tn], preferred_element_type=F32).astype(BF16)
    ab_ref[...] = jnp.dot(u, w_ref[:, nmain:], preferred_element_type=F32)


def inproj(h, mod, w_bf16, tm):
    b, n, d = h.shape
    nw = w_bf16.shape[1]
    return pl.pallas_call(
        functools.partial(_inproj_kernel, tn=256),
        grid=(b, n // tm),
        in_specs=[
            pl.BlockSpec((None, tm, d), lambda i, j: (i, j, 0)),
            pl.BlockSpec((None, 6, d), lambda i, j: (i, 0, 0)),
            pl.BlockSpec((d, nw), lambda i, j: (0, 0)),
        ],
        out_specs=[pl.BlockSpec((None, tm, MAIN_W), lambda i, j: (i, j, 0)),
                   pl.BlockSpec((None, tm, LANES), lambda i, j: (i, j, 0))],
        out_shape=[jax.ShapeDtypeStruct((b, n, MAIN_W), BF16),
                   jax.ShapeDtypeStruct((b, n, LANES), F32)],
        compiler_params=_cparams("parallel", "parallel"),
    )(h, mod, w_bf16)


def _head_mean_matrix():
    r = lax.broadcasted_iota(jnp.int32, (GROUP_W, GROUP_W), 0) // HEAD_DIM
    c = lax.broadcasted_iota(jnp.int32, (GROUP_W, GROUP_W), 1) // HEAD_DIM
    return jnp.where(r == c, 1.0 / HEAD_DIM, 0.0).astype(F32)


def _head_rms_norm(o):
    ms = jnp.dot(o * o, _head_mean_matrix(), preferred_element_type=F32, precision=lax.Precision.HIGHEST)
    return o * lax.rsqrt(ms + RMS_EPS)


def _diff_kernel(*refs, latent, lam_init, scale, nl, nc):
    if latent:
        (q_ref, kl_ref, kc_ref, vl_ref, vc_ref, cq_ref, sq_ref, ck_ref, sk_ref, lp_ref, ng_ref,
         o_ref, kt_s, vx_s, acc_s) = refs
    else:
        q_ref, kc_ref, vc_ref, lp_ref, ng_ref, o_ref, kt_s, vx_s, acc_s = refs
    half = GROUP_W // 2
    lane = lax.broadcasted_iota(jnp.int32, (1, GROUP_W), 1)

    def rot(x, cos, sin):
        x1, x2 = x[:, :half], x[:, half:]
        return jnp.concatenate([x1 * cos - x2 * sin, x1 * sin + x2 * cos], axis=1)

    def init_keys():
        if latent:
            ck = 512 if nl % 512 == 0 else nl
            for c0 in range(0, nl, ck):
                kr = rot(kl_ref[c0:c0 + ck, :].astype(F32), ck_ref[c0:c0 + ck, :], sk_ref[c0:c0 + ck, :])
                kt_s[:, c0:c0 + ck] = kr.T.astype(BF16)
                v = vl_ref[c0:c0 + ck, :]
                for h in range(GROUP_HEADS):
                    vx_s[h, c0:c0 + ck, :] = jnp.where(lane // HEAD_DIM == h, v, jnp.ones_like(v))
        off = nl if latent else 0
        kt_s[:, off:off + nc] = kc_ref[...].astype(F32).T.astype(BF16)
        v = vc_ref[...]
        for h in range(GROUP_HEADS):
            vx_s[h, off:off + nc, :] = jnp.where(lane // HEAD_DIM == h, v, jnp.ones_like(v))

    if latent:
        pl.when(pl.program_id(1) == 0)(init_keys)
    else:
        init_keys()

    lp = lp_ref[...]
    lam = (jnp.exp(jnp.sum(lp[0:1] * lp[1:2], axis=1, keepdims=True))
           - jnp.exp(jnp.sum(lp[2:3] * lp[3:4], axis=1, keepdims=True)) + lam_init)
    q = q_ref[...].astype(F32)
    if latent:
        q = rot(q, cq_ref[...], sq_ref[...])
    q = q * scale
    acc_s[...] = jnp.zeros_like(acc_s)

    def body(hc, carry):
        h, c = hc // 2, hc % 2
        qm = jnp.where((lane % half) // 16 == hc, q, 0.0).astype(BF16)
        s = jnp.dot(qm, kt_s[...], preferred_element_type=F32)
        m = jnp.max(s, axis=1, keepdims=True)
        e = jnp.exp(s - m).astype(BF16)
        r = jnp.dot(e, vx_s[h], preferred_element_type=F32)
        on = r / pltpu.roll(r, HEAD_DIM, 1)
        coef = jnp.where(c == 0, 1.0, -lam)
        acc_s[...] += jnp.where(lane // HEAD_DIM == h, on, 0.0) * coef
        return carry

    lax.fori_loop(0, 2 * GROUP_HEADS, body, 0)
    y = _head_rms_norm(acc_s[...]) * ng_ref[...] * (1.0 - lam_init)
    o_ref[...] = y.astype(o_ref.dtype)


def diff_attention(p_l, p_c, cos_l, sin_l, lam_params, norm_g, layer_idx, with_ctx_out, tq):
    b, n, _ = p_l.shape
    lc = p_c.shape[1]
    lam_init = 0.8 - 0.6 * math.exp(-0.3 * layer_idx)
    scale = (HEAD_DIM // 2) ** -0.5
    ng = jnp.tile(norm_g, GROUP_HEADS)[None, :]
    gw = GROUP_W
    blk = gw // LANES
    kw = dict(lam_init=lam_init, scale=scale, nl=n, nc=lc)
    y_l = pl.pallas_call(
        functools.partial(_diff_kernel, latent=True, **kw),
        grid=(b, n // tq),
        in_specs=[
            pl.BlockSpec((None, tq, gw), lambda i, j: (i, j, CB_DIFF_Q // blk)),
            pl.BlockSpec((None, n, gw), lambda i, j: (i, 0, CB_DIFF_K // blk)),
            pl.BlockSpec((None, lc, gw), lambda i, j: (i, 0, CB_DIFF_K // blk)),
            pl.BlockSpec((None, n, gw), lambda i, j: (i, 0, CB_DIFF_V // blk)),
            pl.BlockSpec((None, lc, gw), lambda i, j: (i, 0, CB_DIFF_V // blk)),
            pl.BlockSpec((tq, LANES), lambda i, j: (j, 0)),
            pl.BlockSpec((tq, LANES), lambda i, j: (j, 0)),
            pl.BlockSpec((n, LANES), lambda i, j: (0, 0)),
            pl.BlockSpec((n, LANES), lambda i, j: (0, 0)),
            pl.BlockSpec((4, HEAD_DIM // 2), lambda i, j: (0, 0)),
            pl.BlockSpec((1, gw), lambda i, j: (0, 0)),
        ],
        out_specs=pl.BlockSpec((None, tq, gw), lambda i, j: (i, j, 0)),
        out_shape=jax.ShapeDtypeStruct((b, n, gw), BF16),
        scratch_shapes=[pltpu.VMEM((gw, n + lc), BF16), pltpu.VMEM((GROUP_HEADS, n + lc, gw), BF16),
                        pltpu.VMEM((tq, gw), F32)],
        compiler_params=_cparams("parallel", "arbitrary"),
    )(p_l, p_l, p_c, p_l, p_c, cos_l, sin_l, cos_l, sin_l, lam_params, ng)
    y_c = None
    if with_ctx_out:
        y_c = pl.pallas_call(
            functools.partial(_diff_kernel, latent=False, **kw),
            grid=(b,),
            in_specs=[
                pl.BlockSpec((None, lc, gw), lambda i: (i, 0, CB_DIFF_Q // blk)),
                pl.BlockSpec((None, lc, gw), lambda i: (i, 0, CB_DIFF_K // blk)),
                pl.BlockSpec((None, lc, gw), lambda i: (i, 0, CB_DIFF_V // blk)),
                pl.BlockSpec((4, HEAD_DIM // 2), lambda i: (0, 0)),
                pl.BlockSpec((1, gw), lambda i: (0, 0)),
            ],
            out_specs=pl.BlockSpec((None, lc, gw), lambda i: (i, 0, 0)),
            out_shape=jax.ShapeDtypeStruct((b, lc, gw), BF16),
            scratch_shapes=[pltpu.VMEM((gw, lc), BF16), pltpu.VMEM((GROUP_HEADS, lc, gw), BF16),
                            pltpu.VMEM((lc, gw), F32)],
            compiler_params=_cparams("parallel"),
        )(p_c, p_c, p_c, lam_params, ng)
    return y_l, y_c


def _swa_kernel(*refs, latent, nl, tq, win):
    if latent:
        q_ref, kl_ref, kc_ref, vl_ref, vc_ref, cq_ref, sq_ref, ck_ref, sk_ref, sink_ref, o_ref, kr_s = refs
    else:
        q_ref, kc_ref, vc_ref, sink_ref, o_ref = refs
    hk_w = LANES // 2
    lane = lax.broadcasted_iota(jnp.int32, (1, LANES), 1)
    scale = HEAD_DIM ** -0.5

    def rot(x, cos, sin_signed):
        return x * cos + pltpu.roll(x, hk_w, 1) * sin_signed

    if latent:
        @pl.when(pl.program_id(1) == 0)
        def _():
            kr_s[...] = rot(kl_ref[...].astype(F32), ck_ref[...], sk_ref[...]).astype(BF16)

        j = pl.program_id(1)
        ws = pl.multiple_of(jnp.clip(j * tq - WINDOW, 0, nl - win), WINDOW)
        kwin = kr_s[pl.ds(ws, win), :]
        vwin = vl_ref[pl.ds(ws, win), :]
        qpos = j * tq + lax.broadcasted_iota(jnp.int32, (tq, 1), 0)
        kpos = ws + lax.broadcasted_iota(jnp.int32, (1, win), 1)
        valid = jnp.abs(qpos - kpos) <= WINDOW
    kc = kc_ref[...]
    vc = vc_ref[...]
    nt = (((1,), (1,)), ((), ()))
    for hk in range(2):
        pieces = []
        for g in range(2):
            qg = q_ref[:, g * LANES:(g + 1) * LANES].astype(F32)
            if latent:
                qg = rot(qg, cq_ref[...], sq_ref[...])
            qm = jnp.where((lane % hk_w) // 32 == hk, qg * scale, 0.0).astype(BF16)
            snk = sink_ref[0:1, hk * 2 + g:hk * 2 + g + 1]
            sc = lax.dot_general(qm, kc, nt, preferred_element_type=F32)
            m = jnp.maximum(jnp.max(sc, axis=1, keepdims=True), snk)
            if latent:
                sb = lax.dot_general(qm, kwin, nt, preferred_element_type=F32)
                sb = jnp.where(valid, sb, -jnp.inf)
                m = jnp.maximum(m, jnp.max(sb, axis=1, keepdims=True))
            ec = jnp.exp(sc - m)
            den = jnp.sum(ec, axis=1, keepdims=True) + jnp.exp(snk - m)
            o = jnp.dot(ec.astype(BF16), vc, preferred_element_type=F32)
            if latent:
                eb = jnp.exp(sb - m)
                den = den + jnp.sum(eb, axis=1, keepdims=True)
                o = o + jnp.dot(eb.astype(BF16), vwin, preferred_element_type=F32)
            o = o / den
            if hk != g:
                o = pltpu.roll(o, HEAD_DIM, 1)
            pieces.append(o)
        o_ref[:, hk * LANES:(hk + 1) * LANES] = jnp.where(lane < HEAD_DIM, pieces[0], pieces[1]).astype(o_ref.dtype)


def swa_attention(p_l, p_c, cos_l, sin_l, sink, with_ctx_out, tq):
    b, n, _ = p_l.shape
    lc = p_c.shape[1]
    gw = GROUP_W
    win = tq + 2 * WINDOW
    snk = jnp.zeros((1, LANES), F32).at[0, :GROUP_HEADS].set(sink)
    y_l = pl.pallas_call(
        functools.partial(_swa_kernel, latent=True, nl=n, tq=tq, win=win),
        grid=(b, n // tq),
        in_specs=[
            pl.BlockSpec((None, tq, gw), lambda i, j: (i, j, CB_SWA_Q // 2)),
            pl.BlockSpec((None, n, LANES), lambda i, j: (i, 0, CB_SWA_K)),
            pl.BlockSpec((None, lc, LANES), lambda i, j: (i, 0, CB_SWA_K)),
            pl.BlockSpec((None, n, LANES), lambda i, j: (i, 0, CB_SWA_V)),
            pl.BlockSpec((None, lc, LANES), lambda i, j: (i, 0, CB_SWA_V)),
            pl.BlockSpec((tq, LANES), lambda i, j: (j, 0)),
            pl.BlockSpec((tq, LANES), lambda i, j: (j, 0)),
            pl.BlockSpec((n, LANES), lambda i, j: (0, 0)),
            pl.BlockSpec((n, LANES), lambda i, j: (0, 0)),
            pl.BlockSpec((1, LANES), lambda i, j: (0, 0)),
        ],
        out_specs=pl.BlockSpec((None, tq, gw), lambda i, j: (i, j, 0)),
        out_shape=jax.ShapeDtypeStruct((b, n, gw), BF16),
        scratch_shapes=[pltpu.VMEM((n, LANES), BF16)],
        compiler_params=_cparams("parallel", "arbitrary"),
    )(p_l, p_l, p_c, p_l, p_c, cos_l, sin_l, cos_l, sin_l, snk)
    y_c = None
    if with_ctx_out:
        y_c = pl.pallas_call(
            functools.partial(_swa_kernel, latent=False, nl=n, tq=lc, win=win),
            grid=(b,),
            in_specs=[
                pl.BlockSpec((None, lc, gw), lambda i: (i, 0, CB_SWA_Q // 2)),
                pl.BlockSpec((None, lc, LANES), lambda i: (i, 0, CB_SWA_K)),
                pl.BlockSpec((None, lc, LANES), lambda i: (i, 0, CB_SWA_V)),
                pl.BlockSpec((1, LANES), lambda i: (0, 0)),
            ],
            out_specs=pl.BlockSpec((None, lc, gw), lambda i: (i, 0, 0)),
            out_shape=jax.ShapeDtypeStruct((b, lc, gw), BF16),
            compiler_params=_cparams("parallel"),
        )(p_c, p_c, p_c, snk)
    return y_l, y_c


def _layer_norm(x, g, b):
    mu = jnp.mean(x, -1, keepdims=True)
    xc = x - mu
    var = jnp.mean(xc * xc, -1, keepdims=True)
    return xc * lax.rsqrt(var + LN_EPS) * g + b


def _outproj_kernel(y_ref, h_ref, mod_ref, w_ref, ln_ref, rw_ref, hn_ref, u2_ref, aff_ref, *, alpha):
    mix = jnp.dot(y_ref[...], w_ref[...], preferred_element_type=F32)
    hn = _layer_norm(alpha * h_ref[...] + mod_ref[2:3, :] * mix, ln_ref[0:1, :], ln_ref[1:2, :])
    hn_ref[...] = hn
    u2 = (hn * (1.0 + mod_ref[4:5, :]) + mod_ref[3:4, :]).astype(BF16)
    u2_ref[...] = u2
    logits = lax.dot_general(rw_ref[...], u2, (((1,), (1,)), ((), ())), preferred_element_type=F32)
    m = jnp.max(logits, 0, keepdims=True)
    e = jnp.exp(logits - m)
    aff_ref[...] = e / jnp.sum(e, 0, keepdims=True)


def outproj(y, h, mod, w_bf16, ln, rw_t_bf16, alpha, tm):
    b, n, d = h.shape
    mixw = y.shape[-1]
    ne = rw_t_bf16.shape[0]
    return pl.pallas_call(
        functools.partial(_outproj_kernel, alpha=alpha),
        grid=(b, n // tm),
        in_specs=[
            pl.BlockSpec((None, tm, mixw), lambda i, j: (i, j, 0)),
            pl.BlockSpec((None, tm, d), lambda i, j: (i, j, 0)),
            pl.BlockSpec((None, 6, d), lambda i, j: (i, 0, 0)),
            pl.BlockSpec((mixw, d), lambda i, j: (0, 0)),
            pl.BlockSpec((2, d), lambda i, j: (0, 0)),
            pl.BlockSpec((ne, d), lambda i, j: (0, 0)),
        ],
        out_specs=[
            pl.BlockSpec((None, tm, d), lambda i, j: (i, j, 0)),
            pl.BlockSpec((None, tm, d), lambda i, j: (i, j, 0)),
            pl.BlockSpec((None, ne, tm), lambda i, j: (i, 0, j)),
        ],
        out_shape=[
            jax.ShapeDtypeStruct((b, n, d), F32),
            jax.ShapeDtypeStruct((b, n, d), BF16),
            jax.ShapeDtypeStruct((b, ne, n), F32),
        ],
        compiler_params=_cparams("parallel", "parallel"),
    )(y, h, mod, w_bf16, ln, rw_t_bf16)


def _ffn_kernel(x_ref, wg_ref, wu_ref, wd_ref, o_ref, *, rows):
    f = pl.program_id(2)
    tb, cap, d = x_ref.shape

    @pl.when(f == 0)
    def _():
        o_ref[...] = jnp.zeros_like(o_ref)

    wg = wg_ref[...].astype(BF16)
    wu = wu_ref[...].astype(BF16)
    wd = wd_ref[...].astype(BF16)
    for bi in range(tb):
        for r in range(cap // rows):
            x = x_ref[bi, r * rows:(r + 1) * rows, :]
            g = jnp.dot(x, wg, preferred_element_type=F32)
            u = jnp.dot(x, wu, preferred_element_type=F32)
            hid = (g * jax.nn.sigmoid(g) * u).astype(BF16)
            o_ref[bi, r * rows:(r + 1) * rows, :] += jnp.dot(hid, wd, preferred_element_type=F32)


def expert_ffn(xin, w_gate, w_up, w_down, tb, tf, rows):
    b, ne, cap, d = xin.shape
    ff = w_gate.shape[-1]
    return pl.pallas_call(
        functools.partial(_ffn_kernel, rows=rows),
        grid=(ne, b // tb, ff // tf),
        in_specs=[
            pl.BlockSpec((tb, None, cap, d), lambda e, i, f: (i, e, 0, 0)),
            pl.BlockSpec((None, d, tf), lambda e, i, f: (e, 0, f)),
            pl.BlockSpec((None, d, tf), lambda e, i, f: (e, 0, f)),
            pl.BlockSpec((None, tf, d), lambda e, i, f: (e, f, 0)),
        ],
        out_specs=pl.BlockSpec((tb, None, cap, d), lambda e, i, f: (i, e, 0, 0)),
        out_shape=jax.ShapeDtypeStruct((b, ne, cap, d), F32),
        compiler_params=_cparams("parallel", "parallel", "arbitrary"),
    )(xin, w_gate, w_up, w_down)


def layer_norm(x, g, b):
    mu = jnp.mean(x, -1, keepdims=True)
    var = jnp.mean(jnp.square(x - mu), -1, keepdims=True)
    return (x - mu) * lax.rsqrt(var + LN_EPS) * g + b


def rms_norm(x, g=None, eps=RMS_EPS):
    y = x * lax.rsqrt(jnp.mean(jnp.square(x), -1, keepdims=True) + eps)
    if g is not None:
        y = y * g
    return y


def l2_normalize(x, eps=1e-6):
    return x * lax.rsqrt(jnp.sum(jnp.square(x), -1, keepdims=True) + eps)


def axial_rope_angles(n, rot_dim):
    rows = n // GRID_W
    row = jnp.repeat(jnp.arange(rows, dtype=F32), GRID_W)
    col = jnp.tile(jnp.arange(GRID_W, dtype=F32), rows)
    n_freq = rot_dim // 4
    inv = ROPE_BASE ** (-jnp.arange(n_freq, dtype=F32) / n_freq)
    return jnp.concatenate([row[:, None] * inv, col[:, None] * inv], -1)


def retention_angles(n, dim):
    theta = 1.0 / (ROPE_BASE ** jnp.linspace(0.0, 1.0, dim // 2, dtype=F32))
    return jnp.arange(n, dtype=F32)[:, None] * theta


def rotate(x, ang):
    half = x.shape[-1] // 2
    cos, sin = jnp.cos(ang).astype(x.dtype), jnp.sin(ang).astype(x.dtype)
    x1, x2 = x[..., :half], x[..., half:]
    return jnp.concatenate([x1 * cos - x2 * sin, x1 * sin + x2 * cos], -1)


def flip_seq(t):
    return jnp.flip(t, axis=2)


def retention_scan(q, k, v, log_g, s0, with_out):
    b, h, n, dk = q.shape
    dv = v.shape[-1]
    c = RET_CHUNK
    nc = n // c
    qc, kc, vc = (t.reshape(b, h, nc, c, t.shape[-1]) for t in (q, k, v))
    pos = jnp.arange(c, dtype=F32)
    lg = log_g[:, None]
    k_decay = jnp.exp(lg * (c - 1 - pos))
    kv = jnp.einsum('bhncd,hc,bhnce->nbhde', kc, k_decay, vc)
    chunk_decay = jnp.exp(log_g * c)[None, :, None, None]

    def step(s, kv_n):
        return s * chunk_decay + kv_n, (s if with_out else None)

    s_last, s_prev = lax.scan(step, s0, kv)
    if not with_out:
        return None, s_last
    rel = pos[:, None] - pos[None, :]
    d_mat = jnp.where(rel >= 0, jnp.exp(lg[:, :, None] * jnp.maximum(rel, 0.0)), 0.0)
    q_decay = jnp.exp(lg * (pos + 1.0))
    scores = jnp.einsum('bhnid,bhnjd->bhnij', qc, kc) * d_mat[:, None]
    o = (jnp.einsum('bhnij,bhnje->bhnie', scores, vc)
         + jnp.einsum('bhnid,hi,nbhde->bhnie', qc, q_decay, s_prev))
    return o.reshape(b, h, n, dv), s_last


def retention_group(p_l, p_c, ret_decay, with_ctx_out, gh):
    def heads(t):
        return t.reshape(t.shape[0], t.shape[1], gh, HEAD_DIM).transpose(0, 2, 1, 3)
    scale = HEAD_DIM ** -0.5
    ang = retention_angles(p_l['ret_q'].shape[1], HEAD_DIM)
    ql = rotate(heads(p_l['ret_q']) * scale, ang)
    kl = rotate(heads(p_l['ret_k']), ang)
    vl = heads(p_l['ret_v'])
    qc, kc, vc = heads(p_c['ret_q']) * scale, heads(p_c['ret_k']), heads(p_c['ret_v'])
    log_g = jax.nn.log_sigmoid(ret_decay)
    s0 = jnp.zeros((ql.shape[0], gh, HEAD_DIM, HEAD_DIM), F32)
    o_l, o_c = 0.0, 0.0
    for dr in range(2):
        f = flip_seq if dr else (lambda t: t)
        oc, s_ctx = retention_scan(f(qc), f(kc), f(vc), log_g[dr], s0, with_ctx_out)
        ol, _ = retention_scan(f(ql), f(kl), f(vl), log_g[dr], s_ctx, True)
        o_l = o_l + f(ol)
        if with_ctx_out:
            o_c = o_c + f(oc)

    def finish(o, gate):
        y = rms_norm(o).transpose(0, 2, 1, 3).reshape(gate.shape)
        return jax.nn.silu(gate) * y

    return finish(o_l, p_l['ret_g']), (finish(o_c, p_c['ret_g']) if with_ctx_out else None)


def short_conv(x, w):
    return lax.conv_general_dilated(
        x, w[:, None, :], window_strides=(1,),
        padding=[(SHORT_CONV // 2, SHORT_CONV // 2)],
        dimension_numbers=('NWC', 'WIO', 'NWC'), feature_group_count=x.shape[-1])


def gdn_scan(q, k, v, log_a, beta, s0, with_out):
    b, h, n, dk = q.shape
    dv = v.shape[-1]
    c = GDN_CHUNK
    nc = n // c
    q, k, v = (t.reshape(b, h, nc, c, t.shape[-1]) for t in (q, k, v))
    log_a, beta = (t.reshape(b, h, nc, c) for t in (log_a, beta))
    g = jnp.cumsum(log_a, -1)
    idx = jnp.arange(c)
    incl = idx[:, None] >= idx[None, :]
    strict = idx[:, None] > idx[None, :]
    decay = jnp.exp(jnp.where(incl, g[..., :, None] - g[..., None, :], -jnp.inf))
    a_mat = jnp.where(strict, beta[..., :, None] * jnp.einsum('bhnid,bhnjd->bhnij', k, k) * decay, 0.0)
    rhs = jnp.concatenate([beta[..., None] * v, (beta * jnp.exp(g))[..., None] * k], -1)
    sol = lax.linalg.triangular_solve(a_mat + jnp.eye(c, dtype=a_mat.dtype), rhs, left_side=True, lower=True)
    u_base, w = sol[..., :dv], sol[..., dv:]
    k_tail = k * jnp.exp(g[..., -1:] - g)[..., None]
    chunk_decay = jnp.exp(g[..., -1])
    mv = lambda t: jnp.moveaxis(t, 2, 0)
    xs = (mv(u_base), mv(w), mv(k_tail), mv(chunk_decay))
    if with_out:
        qk = jnp.einsum('bhnid,bhnjd->bhnij', q, k) * decay
        xs = xs + (mv(qk), mv(q * jnp.exp(g)[..., None]))

    def step(s, xc):
        u = xc[0] - jnp.einsum('bhck,bhkv->bhcv', xc[1], s)
        s_new = s * xc[3][..., None, None] + jnp.einsum('bhck,bhcv->bhkv', xc[2], u)
        if not with_out:
            return s_new, None
        o = jnp.einsum('bhck,bhkv->bhcv', xc[5], s) + jnp.einsum('bhcj,bhjv->bhcv', xc[4], u)
        return s_new, o

    s_last, o = lax.scan(step, s0, xs)
    if not with_out:
        return None, s_last
    return jnp.moveaxis(o, 0, 2).reshape(b, h, n, dv), s_last


def gdn_group(p_l, p_c, conv_w, a_log, dt_bias, norm_g, with_ctx_out, gh):
    def prep(p):
        b, n, _ = p['gdn_qkv'].shape
        qkv = jax.nn.silu(short_conv(p['gdn_qkv'], conv_w))
        qkv = qkv.reshape(b, n, 3, gh, HEAD_DIM).transpose(2, 0, 3, 1, 4)
        q = l2_normalize(qkv[0]) * HEAD_DIM ** -0.5
        k = l2_normalize(qkv[1])
        a = p['gdn_a'].reshape(b, n, 2, gh).transpose(2, 0, 3, 1)
        log_a = -jnp.exp(a_log)[:, None, :, None] * jax.nn.softplus(a + dt_bias[:, None, :, None])
        beta = jax.nn.sigmoid(p['gdn_b'].reshape(b, n, 2, gh).transpose(2, 0, 3, 1))
        return q, k, qkv[2], log_a, beta

    ql, kl, vl, la_l, bt_l = prep(p_l)
    qc, kc, vc, la_c, bt_c = prep(p_c)
    s0 = jnp.zeros((ql.shape[0], gh, HEAD_DIM, HEAD_DIM), F32)
    o_l, o_c = 0.0, 0.0
    for dr in range(2):
        f = flip_seq if dr else (lambda t: t)
        oc, s_ctx = gdn_scan(f(qc), f(kc), f(vc), f(la_c[dr]), f(bt_c[dr]), s0, with_ctx_out)
        ol, _ = gdn_scan(f(ql), f(kl), f(vl), f(la_l[dr]), f(bt_l[dr]), s_ctx, True)
        o_l = o_l + f(ol)
        if with_ctx_out:
            o_c = o_c + f(oc)

    def finish(o, gate):
        o = o.transpose(0, 2, 1, 3)
        gt = gate.reshape(o.shape)
        return (rms_norm(o, norm_g) * jax.nn.silu(gt)).reshape(gate.shape)

    return finish(o_l, p_l['gdn_g']), (finish(o_c, p_c['gdn_g']) if with_ctx_out else None)


def expert_choice_ffn(u2, aff_t, w_gate, w_up, w_down):
    b, n, d = u2.shape
    ne = aff_t.shape[1]
    cap = CAPACITY_FACTOR * n // ne
    weight, idx = lax.top_k(aff_t, cap)
    xin = jax.vmap(lambda ub, ib: ub[ib])(u2, idx)
    tb = 4 if cap >= 512 else b
    y = expert_ffn(xin, w_gate, w_up, w_down, tb=tb, tf=512, rows=min(cap, 256))
    y = y * weight[..., None]
    flat = (jnp.arange(b)[:, None, None] * n + idx).reshape(-1)
    return jnp.zeros((b * n, d), F32).at[flat].add(y.reshape(-1, d)).reshape(b, n, d)


def _jax_views(p, ab, inv_ret):
    blk = lambda cb, w: p[..., cb * LANES:cb * LANES + w].astype(F32)
    gh = GROUP_HEADS
    return {'ret_q': blk(CB_RET_Q, GROUP_W)[..., inv_ret], 'ret_k': blk(CB_RET_K, GROUP_W)[..., inv_ret],
            'ret_v': blk(CB_RET_V, GROUP_W), 'ret_g': blk(CB_RET_G, GROUP_W),
            'gdn_qkv': blk(CB_GDN_QKV, 3 * GROUP_W), 'gdn_g': blk(CB_GDN_G, GROUP_W),
            'gdn_a': ab[..., :2 * gh], 'gdn_b': ab[..., 2 * gh:4 * gh]}


def kernel(x, c, ctx, c_ctx, ada_w, ada_b, w_in, w_out, ret_decay, diff_lambda, diff_norm, gdn_conv,
           gdn_a_log, gdn_dt_bias, gdn_norm, swa_sink, ln_g, ln_b, router_w, w_gate, w_up, w_down):
    b, n, d = x.shape
    lc = ctx.shape[1]
    depth = ada_w.shape[0]
    gh = GROUP_HEADS
    alpha = (2 * depth) ** 0.25
    main_perm, tail_perm, off = _in_col_perm()
    inv_ret = np.argsort(main_perm[:GROUP_W] - off['ret_q'])
    ang_diff = axial_rope_angles(n, HEAD_DIM // 2)
    ang_swa = axial_rope_angles(n, HEAD_DIM)
    cos_diff, sin_diff = jnp.tile(jnp.cos(ang_diff), (1, 8)), jnp.tile(jnp.sin(ang_diff), (1, 8))
    cos_swa = jnp.tile(jnp.cos(ang_swa), (1, 4))
    sin_swa = jnp.tile(jnp.sin(ang_swa), (1, 4)) * jnp.where(jnp.arange(LANES) < LANES // 2, -1.0, 1.0)
    cond_l = jax.nn.silu(c)
    cond_c = jax.nn.silu(c_ctx)
    h, hc = x, ctx
    for layer in range(depth):
        full_ctx = layer < depth - 1
        mod_l = (cond_l @ ada_w[layer] + ada_b[layer]).reshape(b, 6, d)
        mod_c = jnp.broadcast_to((cond_c @ ada_w[layer] + ada_b[layer]).reshape(1, 6, d), (b, 6, d))
        wl = w_in[layer]
        w_in_b = jnp.concatenate([wl[:, main_perm], wl[:, tail_perm],
                                  jnp.zeros((d, LANES - tail_perm.size), F32)], axis=1).astype(BF16)
        w_out_b = w_out[layer].astype(BF16)
        rw_t = router_w[layer].T.astype(BF16)
        ln1 = jnp.stack([ln_g[layer, 0], ln_b[layer, 0]])
        ln2g, ln2b = ln_g[layer, 1], ln_b[layer, 1]
        p_l, ab_l = inproj(h, mod_l, w_in_b, tm=512)
        p_c, ab_c = inproj(hc, mod_c, w_in_b, tm=lc)
        jl, jc = _jax_views(p_l, ab_l, inv_ret), _jax_views(p_c, ab_c, inv_ret)
        ya_l, ya_c = retention_group(jl, jc, ret_decay[layer], full_ctx, gh)
        yb_l, yb_c = diff_attention(p_l, p_c, cos_diff, sin_diff, diff_lambda[layer], diff_norm[layer],
                                    layer, full_ctx, tq=256)
        yc_l, yc_c = gdn_group(jl, jc, gdn_conv[layer], gdn_a_log[layer], gdn_dt_bias[layer], gdn_norm[layer], full_ctx, gh)
        yd_l, yd_c = swa_attention(p_l, p_c, cos_swa, sin_swa, swa_sink[layer], full_ctx, tq=256)
        y_l = jnp.concatenate([ya_l.astype(BF16), yb_l, yc_l.astype(BF16), yd_l], -1)
        h, u2, aff = outproj(y_l, h, mod_l, w_out_b, ln1, rw_t, alpha, tm=512)
        f_l = expert_choice_ffn(u2, aff, w_gate[layer], w_up[layer], w_down[layer])
        h = layer_norm(alpha * h + mod_l[:, None, 5] * f_l, ln2g, ln2b)
        if full_ctx:
            y_c = jnp.concatenate([ya_c.astype(BF16), yb_c, yc_c.astype(BF16), yd_c], -1)
            hc, u2c, affc = outproj(y_c, hc, mod_c, w_out_b, ln1, rw_t, alpha, tm=lc)
            f_c = expert_choice_ffn(u2c, affc, w_gate[layer], w_up[layer], w_down[layer])
            hc = layer_norm(alpha * hc + mod_c[:, None, 5] * f_c, ln2g, ln2b)
    return h
```

```python
import functools
import math

import numpy as np
import jax
import jax.numpy as jnp
from jax import lax
from jax.experimental import pallas as pl
from jax.experimental.pallas import tpu as pltpu

F32 = jnp.float32
BF16 = jnp.bfloat16

GRID_W = 64
HEAD_DIM = 64
N_MIXERS = 4
GROUP_HEADS = 4
GROUP_W = GROUP_HEADS * HEAD_DIM
ROPE_BASE = 10000.0
RET_CHUNK = 128
GDN_CHUNK = 64
SHORT_CONV = 3
WINDOW = 128
N_EXPERTS = 16
CAPACITY_FACTOR = 2
LN_EPS = 1e-5
RMS_EPS = 1e-6
LANES = 128

VMEM_LIMIT_BYTES = 56 * 1024 * 1024

CB_RET_Q, CB_RET_K, CB_RET_V, CB_RET_G = 0, 2, 4, 6
CB_DIFF_Q, CB_DIFF_K, CB_DIFF_V = 8, 10, 12
CB_GDN_QKV, CB_GDN_G = 14, 20
CB_SWA_Q, CB_SWA_K, CB_SWA_V = 22, 24, 25
MAIN_W = 26 * LANES


def _cparams(*sem):
    return pltpu.CompilerParams(dimension_semantics=sem, vmem_limit_bytes=VMEM_LIMIT_BYTES)


def _in_col_perm():
    gw, gh, hd = GROUP_W, GROUP_HEADS, HEAD_DIM
    off = {}
    o = 0
    for name, w in (('ret_q', gw), ('ret_k', gw), ('ret_v', gw), ('ret_g', gw), ('diff_q', gw), ('diff_k', gw),
                    ('diff_v', gw), ('gdn_qkv', 3 * gw), ('gdn_g', gw), ('gdn_a', 2 * gh), ('gdn_b', 2 * gh),
                    ('swa_q', gw), ('swa_k', gw // 2), ('swa_v', gw // 2)):
        off[name] = o
        o += w
    ident = lambda name, w: off[name] + np.arange(w)
    ret = np.zeros(gw, np.int64)
    for h in range(gh):
        for j in range(hd):
            ret[(j // 32) * 128 + h * 32 + j % 32] = h * hd + j
    dif = np.zeros(gw, np.int64)
    for h in range(gh):
        for c in range(2):
            for j in range(32):
                dif[(j // 16) * 128 + (h * 2 + c) * 16 + j % 16] = h * hd + c * 32 + j
    swq = np.zeros(gw, np.int64)
    for hk in range(2):
        for g in range(2):
            for j in range(hd):
                swq[g * 128 + (j // 32) * 64 + hk * 32 + j % 32] = hk * 128 + g * hd + j
    swk = np.zeros(gw // 2, np.int64)
    for hk in range(2):
        for j in range(hd):
            swk[(j // 32) * 64 + hk * 32 + j % 32] = hk * hd + j
    main = np.concatenate([
        off['ret_q'] + ret, off['ret_k'] + ret, ident('ret_v', gw), ident('ret_g', gw),
        off['diff_q'] + dif, off['diff_k'] + dif, ident('diff_v', gw),
        ident('gdn_qkv', 3 * gw), ident('gdn_g', gw),
        off['swa_q'] + swq, off['swa_k'] + swk, ident('swa_v', gw // 2)])
    tail = np.concatenate([ident('gdn_a', 2 * gh), ident('gdn_b', 2 * gh)])
    return main, tail, off


def _inproj_kernel(h_ref, mod_ref, w_ref, o_ref, ab_ref, *, tn):
    u = (h_ref[...] * (1.0 + mod_ref[1:2, :]) + mod_ref[0:1, :]).astype(BF16)
    nmain = o_ref.shape[-1]
    for j in range(nmain // tn):
        o_ref[:, j * tn:(j + 1) * tn] = jnp.dot(
            u, w_ref[:, j * tn:(j + 1) * tn], preferred_element_type=F32).astype(BF16)
    ab_ref[...] = jnp.dot(u, w_ref[:, nmain:], preferred_element_type=F32)


def inproj(h, mod, w_bf16, tm):
    b, n, d = h.shape
    nw = w_bf16.shape[1]
    return pl.pallas_call(
        functools.partial(_inproj_kernel, tn=256),
        grid=(b, n // tm),
        in_specs=[
            pl.BlockSpec((None, tm, d), lambda i, j: (i, j, 0)),
            pl.BlockSpec((None, 6, d), lambda i, j: (i, 0, 0)),
            pl.BlockSpec((d, nw), lambda i, j: (0, 0)),
        ],
        out_specs=[pl.BlockSpec((None, tm, MAIN_W), lambda i, j: (i, j, 0)),
                   pl.BlockSpec((None, tm, LANES), lambda i, j: (i, j, 0))],
        out_shape=[jax.ShapeDtypeStruct((b, n, MAIN_W), BF16),
                   jax.ShapeDtypeStruct((b, n, LANES), F32)],
        compiler_params=_cparams("parallel", "parallel"),
    )(h, mod, w_bf16)


def _head_mean_matrix():
    r = lax.broadcasted_iota(jnp.int32, (GROUP_W, GROUP_W), 0) // HEAD_DIM
    c = lax.broadcasted_iota(jnp.int32, (GROUP_W, GROUP_W), 1) // HEAD_DIM
    return jnp.where(r == c, 1.0 / HEAD_DIM, 0.0).astype(F32)


def _head_rms_norm(o):
    ms = jnp.dot(o * o, _head_mean_matrix(), preferred_element_type=F32, precision=lax.Precision.HIGHEST)
    return o * lax.rsqrt(ms + RMS_EPS)


def _diff_kernel(*refs, latent, lam_init, scale, nl, nc):
    if latent:
        (q_ref, kl_ref, kc_ref, vl_ref, vc_ref, cq_ref, sq_ref, ck_ref, sk_ref, lp_ref, ng_ref,
         o_ref, kt_s, vx_s, acc_s) = refs
    else:
        q_ref, kc_ref, vc_ref, lp_ref, ng_ref, o_ref, kt_s, vx_s, acc_s = refs
    half = GROUP_W // 2
    lane = lax.broadcasted_iota(jnp.int32, (1, GROUP_W), 1)

    def rot(x, cos, sin):
        x1, x2 = x[:, :half], x[:, half:]
        return jnp.concatenate([x1 * cos - x2 * sin, x1 * sin + x2 * cos], axis=1)

    def init_keys():
        if latent:
            ck = 512 if nl % 512 == 0 else nl
            for c0 in range(0, nl, ck):
                kr = rot(kl_ref[c0:c0 + ck, :].astype(F32), ck_ref[c0:c0 + ck, :], sk_ref[c0:c0 + ck, :])
                kt_s[:, c0:c0 + ck] = kr.T.astype(BF16)
                v = vl_ref[c0:c0 + ck, :]
                for h in range(GROUP_HEADS):
                    vx_s[h, c0:c0 + ck, :] = jnp.where(lane // HEAD_DIM == h, v, jnp.ones_like(v))
        off = nl if latent else 0
        kt_s[:, off:off + nc] = kc_ref[...].astype(F32).T.astype(BF16)
        v = vc_ref[...]
        for h in range(GROUP_HEADS):
            vx_s[h, off:off + nc, :] = jnp.where(lane // HEAD_DIM == h, v, jnp.ones_like(v))

    if latent:
        pl.when(pl.program_id(1) == 0)(init_keys)
    else:
        init_keys()

    lp = lp_ref[...]
    lam = (jnp.exp(jnp.sum(lp[0:1] * lp[1:2], axis=1, keepdims=True))
           - jnp.exp(jnp.sum(lp[2:3] * lp[3:4], axis=1, keepdims=True)) + lam_init)
    q = q_ref[...].astype(F32)
    if latent:
        q = rot(q, cq_ref[...], sq_ref[...])
    q = q * scale
    acc_s[...] = jnp.zeros_like(acc_s)

    def body(hc, carry):
        h, c = hc // 2, hc % 2
        qm = jnp.where((lane % half) // 16 == hc, q, 0.0).astype(BF16)
        s = jnp.dot(qm, kt_s[...], preferred_element_type=F32)
        m = jnp.max(s, axis=1, keepdims=True)
        e = jnp.exp(s - m).astype(BF16)
        r = jnp.dot(e, vx_s[h], preferred_element_type=F32)
        on = r / pltpu.roll(r, HEAD_DIM, 1)
        coef = jnp.where(c == 0, 1.0, -lam)
        acc_s[...] += jnp.where(lane // HEAD_DIM == h, on, 0.0) * coef
        return carry

    lax.fori_loop(0, 2 * GROUP_HEADS, body, 0)
    y = _head_rms_norm(acc_s[...]) * ng_ref[...] * (1.0 - lam_init)
    o_ref[...] = y.astype(o_ref.dtype)


def diff_attention(p_l, p_c, cos_l, sin_l, lam_params, norm_g, layer_idx, with_ctx_out, tq):
    b, n, _ = p_l.shape
    lc = p_c.shape[1]
    lam_init = 0.8 - 0.6 * math.exp(-0.3 * layer_idx)
    scale = (HEAD_DIM // 2) ** -0.5
    ng = jnp.tile(norm_g, GROUP_HEADS)[None, :]
    gw = GROUP_W
    blk = gw // LANES
    kw = dict(lam_init=lam_init, scale=scale, nl=n, nc=lc)
    y_l = pl.pallas_call(
        functools.partial(_diff_kernel, latent=True, **kw),
        grid=(b, n // tq),
        in_specs=[
            pl.BlockSpec((None, tq, gw), lambda i, j: (i, j, CB_DIFF_Q // blk)),
            pl.BlockSpec((None, n, gw), lambda i, j: (i, 0, CB_DIFF_K // blk)),
            pl.BlockSpec((None, lc, gw), lambda i, j: (i, 0, CB_DIFF_K // blk)),
            pl.BlockSpec((None, n, gw), lambda i, j: (i, 0, CB_DIFF_V // blk)),
            pl.BlockSpec((None, lc, gw), lambda i, j: (i, 0, CB_DIFF_V // blk)),
            pl.BlockSpec((tq, LANES), lambda i, j: (j, 0)),
            pl.BlockSpec((tq, LANES), lambda i, j: (j, 0)),
            pl.BlockSpec((n, LANES), lambda i, j: (0, 0)),
            pl.BlockSpec((n, LANES), lambda i, j: (0, 0)),
            pl.BlockSpec((4, HEAD_DIM // 2), lambda i, j: (0, 0)),
            pl.BlockSpec((1, gw), lambda i, j: (0, 0)),
        ],
        out_specs=pl.BlockSpec((None, tq, gw), lambda i, j: (i, j, 0)),
        out_shape=jax.ShapeDtypeStruct((b, n, gw), BF16),
        scratch_shapes=[pltpu.VMEM((gw, n + lc), BF16), pltpu.VMEM((GROUP_HEADS, n + lc, gw), BF16),
                        pltpu.VMEM((tq, gw), F32)],
        compiler_params=_cparams("parallel", "arbitrary"),
    )(p_l, p_l, p_c, p_l, p_c, cos_l, sin_l, cos_l, sin_l, lam_params, ng)
    y_c = None
    if with_ctx_out:
        y_c = pl.pallas_call(
            functools.partial(_diff_kernel, latent=False, **kw),
            grid=(b,),
            in_specs=[
                pl.BlockSpec((None, lc, gw), lambda i: (i, 0, CB_DIFF_Q // blk)),
                pl.BlockSpec((None, lc, gw), lambda i: (i, 0, CB_DIFF_K // blk)),
                pl.BlockSpec((None, lc, gw), lambda i: (i, 0, CB_DIFF_V // blk)),
                pl.BlockSpec((4, HEAD_DIM // 2), lambda i: (0, 0)),
                pl.BlockSpec((1, gw), lambda i: (0, 0)),
            ],
            out_specs=pl.BlockSpec((None, lc, gw), lambda i: (i, 0, 0)),
            out_shape=jax.ShapeDtypeStruct((b, lc, gw), BF16),
            scratch_shapes=[pltpu.VMEM((gw, lc), BF16), pltpu.VMEM((GROUP_HEADS, lc, gw), BF16),
                            pltpu.VMEM((lc, gw), F32)],
            compiler_params=_cparams("parallel"),
        )(p_c, p_c, p_c, lam_params, ng)
    return y_l, y_c


def _swa_kernel(*refs, latent, nl, tq, win):
    if latent:
        q_ref, kl_ref, kc_ref, vl_ref, vc_ref, cq_ref, sq_ref, ck_ref, sk_ref, sink_ref, o_ref, kr_s = refs
    else:
        q_ref, kc_ref, vc_ref, sink_ref, o_ref = refs
    hk_w = LANES // 2
    lane = lax.broadcasted_iota(jnp.int32, (1, LANES), 1)
    scale = HEAD_DIM ** -0.5

    def rot(x, cos, sin_signed):
        return x * cos + pltpu.roll(x, hk_w, 1) * sin_signed

    if latent:
        @pl.when(pl.program_id(1) == 0)
        def _():
            kr_s[...] = rot(kl_ref[...].astype(F32), ck_ref[...], sk_ref[...]).astype(BF16)

        j = pl.program_id(1)
        ws = pl.multiple_of(jnp.clip(j * tq - WINDOW, 0, nl - win), WINDOW)
        kwin = kr_s[pl.ds(ws, win), :]
        vwin = vl_ref[pl.ds(ws, win), :]
        qpos = j * tq + lax.broadcasted_iota(jnp.int32, (tq, 1), 0)
        kpos = ws + lax.broadcasted_iota(jnp.int32, (1, win), 1)
        valid = jnp.abs(qpos - kpos) <= WINDOW
    kc = kc_ref[...]
    vc = vc_ref[...]
    nt = (((1,), (1,)), ((), ()))
    for hk in range(2):
        pieces = []
        for g in range(2):
            qg = q_ref[:, g * LANES:(g + 1) * LANES].astype(F32)
            if latent:
                qg = rot(qg, cq_ref[...], sq_ref[...])
            qm = jnp.where((lane % hk_w) // 32 == hk, qg * scale, 0.0).astype(BF16)
            snk = sink_ref[0:1, hk * 2 + g:hk * 2 + g + 1]
            sc = lax.dot_general(qm, kc, nt, preferred_element_type=F32)
            m = jnp.maximum(jnp.max(sc, axis=1, keepdims=True), snk)
            if latent:
                sb = lax.dot_general(qm, kwin, nt, preferred_element_type=F32)
                sb = jnp.where(valid, sb, -jnp.inf)
                m = jnp.maximum(m, jnp.max(sb, axis=1, keepdims=True))
            ec = jnp.exp(sc - m)
            den = jnp.sum(ec, axis=1, keepdims=True) + jnp.exp(snk - m)
            o = jnp.dot(ec.astype(BF16), vc, preferred_element_type=F32)
            if latent:
                eb = jnp.exp(sb - m)
                den = den + jnp.sum(eb, axis=1, keepdims=True)
                o = o + jnp.dot(eb.astype(BF16), vwin, preferred_element_type=F32)
            o = o / den
            if hk != g:
                o = pltpu.roll(o, HEAD_DIM, 1)
            pieces.append(o)
        o_ref[:, hk * LANES:(hk + 1) * LANES] = jnp.where(lane < HEAD_DIM, pieces[0], pieces[1]).astype(o_ref.dtype)


def swa_attention(p_l, p_c, cos_l, sin_l, sink, with_ctx_out, tq):
    b, n, _ = p_l.shape
    lc = p_c.shape[1]
    gw = GROUP_W
    win = tq + 2 * WINDOW
    snk = jnp.zeros((1, LANES), F32).at[0, :GROUP_HEADS].set(sink)
    y_l = pl.pallas_call(
        functools.partial(_swa_kernel, latent=True, nl=n, tq=tq, win=win),
        grid=(b, n // tq),
        in_specs=[
            pl.BlockSpec((None, tq, gw), lambda i, j: (i, j, CB_SWA_Q // 2)),
            pl.BlockSpec((None, n, LANES), lambda i, j: (i, 0, CB_SWA_K)),
            pl.BlockSpec((None, lc, LANES), lambda i, j: (i, 0, CB_SWA_K)),
            pl.BlockSpec((None, n, LANES), lambda i, j: (i, 0, CB_SWA_V)),
            pl.BlockSpec((None, lc, LANES), lambda i, j: (i, 0, CB_SWA_V)),
            pl.BlockSpec((tq, LANES), lambda i, j: (j, 0)),
            pl.BlockSpec((tq, LANES), lambda i, j: (j, 0)),
            pl.BlockSpec((n, LANES), lambda i, j: (0, 0)),
            pl.BlockSpec((n, LANES), lambda i, j: (0, 0)),
            pl.BlockSpec((1, LANES), lambda i, j: (0, 0)),
        ],
        out_specs=pl.BlockSpec((None, tq, gw), lambda i, j: (i, j, 0)),
        out_shape=jax.ShapeDtypeStruct((b, n, gw), BF16),
        scratch_shapes=[pltpu.VMEM((n, LANES), BF16)],
        compiler_params=_cparams("parallel", "arbitrary"),
    )(p_l, p_l, p_c, p_l, p_c, cos_l, sin_l, cos_l, sin_l, snk)
    y_c = None
    if with_ctx_out:
        y_c = pl.pallas_call(
            functools.partial(_swa_kernel, latent=False, nl=n, tq=lc, win=win),
            grid=(b,),
            in_specs=[
                pl.BlockSpec((None, lc, gw), lambda i: (i, 0, CB_SWA_Q // 2)),
                pl.BlockSpec((None, lc, LANES), lambda i: (i, 0, CB_SWA_K)),
                pl.BlockSpec((None, lc, LANES), lambda i: (i, 0, CB_SWA_V)),
                pl.BlockSpec((1, LANES), lambda i: (0, 0)),
            ],
            out_specs=pl.BlockSpec((None, lc, gw), lambda i: (i, 0, 0)),
            out_shape=jax.ShapeDtypeStruct((b, lc, gw), BF16),
            compiler_params=_cparams("parallel"),
        )(p_c, p_c, p_c, snk)
    return y_l, y_c


def _ret_tables(ret_decay):
    c = RET_CHUNK
    lg = jax.nn.log_sigmoid(ret_decay.astype(F32))
    pos = jnp.arange(c, dtype=F32)
    head_k = (np.arange(GROUP_W) % LANES) // (HEAD_DIM // 2)
    head_v = np.arange(GROUP_W) // HEAD_DIM
    qd, kd, dm, cd = [], [], [], []
    for d in range(2):
        l = lg[d]
        qpow = (pos + 1.0) if d == 0 else (c - pos)
        kpow = (c - 1.0 - pos) if d == 0 else pos
        qd.append(jnp.exp(qpow[:, None] * l[head_k][None, :]))
        kd.append(jnp.exp(kpow[:, None] * l[head_k][None, :]))
        rel = pos[:, None] - pos[None, :]
        rel = rel if d == 0 else -rel
        dm.append(jnp.where(rel[None] >= 0, jnp.exp(l[:, None, None] * jnp.maximum(rel, 0.0)[None]), 0.0))
        cd.append(jnp.exp(l * c)[head_v][None, :])
    return jnp.stack(qd), jnp.stack(kd), jnp.stack(dm), jnp.stack(cd)


def _ret_kernel(ql_ref, kl_ref, vl_ref, gl_ref, qc_ref, kc_ref, vc_ref, gc_ref, cos_ref, sin_ref,
                qd_ref, kd_ref, dm_ref, cd_ref, ol_ref, oc_ref, s_s, accl_s, accc_s, *, nl, nc, with_ctx_out):
    c = RET_CHUNK
    half = GROUP_W // 2
    scale = HEAD_DIM ** -0.5
    lane = lax.broadcasted_iota(jnp.int32, (1, GROUP_W), 1)
    head_k = (lane % LANES) // (HEAD_DIM // 2)
    head_v = lane // HEAD_DIM
    row_hk = (lax.broadcasted_iota(jnp.int32, (GROUP_W, 1), 0) % LANES) // (HEAD_DIM // 2)
    same_head = row_hk == head_v
    nt = (((1,), (1,)), ((), ()))

    def rot(x, cos, sin):
        x1, x2 = x[:, :half], x[:, half:]
        return jnp.concatenate([x1 * cos - x2 * sin, x1 * sin + x2 * cos], axis=1)

    def step(d, refs, row, rotate, want_out):
        q_ref, k_ref, v_ref = refs
        q = q_ref[pl.ds(row, c), :].astype(F32)
        k = k_ref[pl.ds(row, c), :].astype(F32)
        v = v_ref[pl.ds(row, c), :]
        if rotate:
            cos, sin = cos_ref[pl.ds(row, c), :], sin_ref[pl.ds(row, c), :]
            q, k = rot(q, cos, sin), rot(k, cos, sin)
        q = q * scale
        s = s_s[...]
        o = None
        if want_out:
            o = jnp.dot((q * qd_ref[d]).astype(BF16), s.astype(BF16), preferred_element_type=F32)
            kb = k.astype(BF16)
            for h in range(GROUP_HEADS):
                qm = jnp.where(head_k == h, q, 0.0).astype(BF16)
                sc = lax.dot_general(qm, kb, nt, preferred_element_type=F32) * dm_ref[d, h]
                o = o + jnp.where(head_v == h, jnp.dot(sc.astype(BF16), v, preferred_element_type=F32), 0.0)
        kv = jnp.dot((k * kd_ref[d]).T.astype(BF16), v, preferred_element_type=F32)
        s_s[...] = s * cd_ref[d] + jnp.where(same_head, kv, 0.0)
        return o

    def finish(o, g_ref, row, o_ref):
        y = _head_rms_norm(o)
        g = g_ref[pl.ds(row, c), :].astype(F32)
        o_ref[pl.ds(row, c), :] = (g * jax.nn.sigmoid(g) * y).astype(o_ref.dtype)

    lat = (ql_ref, kl_ref, vl_ref)
    ctx = (qc_ref, kc_ref, vc_ref)
    ncl, ncc = nl // c, nc // c

    s_s[...] = jnp.zeros_like(s_s)
    for cc in range(ncc):
        o = step(0, ctx, cc * c, False, with_ctx_out)
        if with_ctx_out:
            accc_s[cc * c:(cc + 1) * c, :] = o

    def fwd(i, carry):
        row = pl.multiple_of(i * c, c)
        accl_s[pl.ds(row, c), :] = step(0, lat, row, True, True)
        return carry

    lax.fori_loop(0, ncl, fwd, 0)

    s_s[...] = jnp.zeros_like(s_s)
    for cc in reversed(range(ncc)):
        o = step(1, ctx, cc * c, False, with_ctx_out)
        if with_ctx_out:
            finish(accc_s[cc * c:(cc + 1) * c, :] + o, gc_ref, cc * c, oc_ref)
    if not with_ctx_out:
        oc_ref[...] = jnp.zeros_like(oc_ref)

    def bwd(t, carry):
        row = pl.multiple_of((ncl - 1 - t) * c, c)
        o = step(1, lat, row, True, True)
        finish(accl_s[pl.ds(row, c), :] + o, gl_ref, row, ol_ref)
        return carry

    lax.fori_loop(0, ncl, bwd, 0)


def retention(p_l, p_c, cos_l, sin_l, ret_decay, with_ctx_out):
    b, n, _ = p_l.shape
    lc = p_c.shape[1]
    gw = GROUP_W
    qd, kd, dm, cd = _ret_tables(ret_decay)
    c = RET_CHUNK
    lat = lambda cb: pl.BlockSpec((None, n, gw), lambda i: (i, 0, cb // 2))
    ctx = lambda cb: pl.BlockSpec((None, lc, gw), lambda i: (i, 0, cb // 2))
    full = lambda shape: pl.BlockSpec(shape, lambda i: (0,) * len(shape))
    y_l, y_c = pl.pallas_call(
        functools.partial(_ret_kernel, nl=n, nc=lc, with_ctx_out=with_ctx_out),
        grid=(b,),
        in_specs=[lat(CB_RET_Q), lat(CB_RET_K), lat(CB_RET_V), lat(CB_RET_G),
                  ctx(CB_RET_Q), ctx(CB_RET_K), ctx(CB_RET_V), ctx(CB_RET_G),
                  full((n, LANES)), full((n, LANES)),
                  full((2, c, gw)), full((2, c, gw)), full((2, GROUP_HEADS, c, c)), full((2, 1, gw))],
        out_specs=[pl.BlockSpec((None, n, gw), lambda i: (i, 0, 0)),
                   pl.BlockSpec((None, lc, gw), lambda i: (i, 0, 0))],
        out_shape=[jax.ShapeDtypeStruct((b, n, gw), BF16), jax.ShapeDtypeStruct((b, lc, gw), BF16)],
        scratch_shapes=[pltpu.VMEM((gw, gw), F32), pltpu.VMEM((n, gw), F32), pltpu.VMEM((lc, gw), F32)],
        compiler_params=_cparams("parallel"),
    )(p_l, p_l, p_l, p_l, p_c, p_c, p_c, p_c, cos_l, sin_l, qd, kd, dm, cd)
    return y_l, (y_c if with_ctx_out else None)


def _gdn_prep_kernel(x_ref, w_ref, o_ref):
    j = pl.program_id(1)
    x = x_ref[...].astype(F32)
    n = x.shape[0]
    row = lax.broadcasted_iota(jnp.int32, (n, 1), 0)
    xp = jnp.where(row == 0, 0.0, pltpu.roll(x, 1, 0))
    xn = jnp.where(row == n - 1, 0.0, pltpu.roll(x, n - 1, 0))
    y = xp * w_ref[0:1, :] + x * w_ref[1:2, :] + xn * w_ref[2:3, :]
    y = y * jax.nn.sigmoid(y)
    ss = jnp.dot(y * y, _head_mean_matrix() * HEAD_DIM, preferred_element_type=F32, precision=lax.Precision.HIGHEST)
    nrm = lax.rsqrt(ss + 1e-6) * jnp.where(j == 0, HEAD_DIM ** -0.5, 1.0)
    y = jnp.where(j < 2, y * nrm, y)
    o_ref[...] = y.astype(o_ref.dtype)


def gdn_prep(p, conv_w):
    b, n, _ = p.shape
    gw = GROUP_W
    return pl.pallas_call(
        _gdn_prep_kernel,
        grid=(b, 3),
        in_specs=[pl.BlockSpec((None, n, gw), lambda i, j: (i, 0, CB_GDN_QKV // 2 + j)),
                  pl.BlockSpec((SHORT_CONV, gw), lambda i, j: (0, j))],
        out_specs=pl.BlockSpec((None, None, n, gw), lambda i, j: (i, j, 0, 0)),
        out_shape=jax.ShapeDtypeStruct((b, 3, n, gw), BF16),
        compiler_params=_cparams("parallel", "parallel"),
    )(p, conv_w)


def _gdn_kernel(xl_ref, gl_ref, abl_ref, xc_ref, gc_ref, abc_ref, par_ref, ng_ref, ol_ref, oc_ref,
                s_s, accl_s, accc_s, *, nl, nc, with_ctx_out):
    c = GDN_CHUNK
    gh, gw = GROUP_HEADS, GROUP_W
    hp = lax.Precision.HIGHEST
    lane = lax.broadcasted_iota(jnp.int32, (1, gw), 1)
    head_v = lane // HEAD_DIM
    lane128 = lax.broadcasted_iota(jnp.int32, (1, LANES), 1)
    r_i = lax.broadcasted_iota(jnp.int32, (gw, 1), 0)
    c_i = lax.broadcasted_iota(jnp.int32, (1, gw), 1)
    same_head = (r_i // HEAD_DIM) == (c_i // HEAD_DIM)
    ti = lax.broadcasted_iota(jnp.int32, (c, 1), 0)
    tj = lax.broadcasted_iota(jnp.int32, (1, c), 1)
    nt = (((1,), (1,)), ((), ()))

    def stack(x):
        return jnp.concatenate([jnp.where(head_v == h, x, 0.0) for h in range(gh)], axis=0)

    def unstack(y):
        return y[0:c] + y[c:2 * c] + y[2 * c:3 * c] + y[3 * c:4 * c]

    def by_head(cols):
        out = cols[gh - 1]
        for h in reversed(range(gh - 1)):
            out = jnp.where(head_v == h, cols[h], out)
        return out

    def step(d, refs, row, want_out):
        x_ref, ab_ref = refs
        q = x_ref[0, pl.ds(row, c), :].astype(F32)
        k = x_ref[1, pl.ds(row, c), :].astype(F32)
        v = x_ref[2, pl.ds(row, c), :].astype(F32)
        ab = ab_ref[pl.ds(row, c), :]
        xa = ab + par_ref[1:2, :]
        la = -par_ref[0:1, :] * (jnp.maximum(xa, 0.0) + jnp.log(1.0 + jnp.exp(-jnp.abs(xa))))
        bt = jax.nn.sigmoid(ab)
        tri = jnp.where((ti >= tj) if d == 0 else (ti <= tj), 1.0, 0.0)
        g = jnp.dot(tri, la, preferred_element_type=F32, precision=hp)
        gcols = [jnp.sum(jnp.where(lane128 == d * gh + h, g, 0.0), axis=1, keepdims=True) for h in range(gh)]
        bcols = [jnp.sum(jnp.where(lane128 == 2 * gh + d * gh + h, bt, 0.0), axis=1, keepdims=True) for h in range(gh)]
        gexp, bexp = by_head(gcols), by_head(bcols)
        glast = gexp[c - 1:c, :] if d == 0 else gexp[0:1, :]
        gc_mat = jnp.concatenate([jnp.broadcast_to(gcols[h], (c, gw)) for h in range(gh)], axis=0)
        b_col = jnp.concatenate(bcols, axis=0)
        i_loc, j_loc = r_i % c, c_i % c
        incl = same_head & ((i_loc >= j_loc) if d == 0 else (i_loc <= j_loc))
        strict = same_head & ((i_loc > j_loc) if d == 0 else (i_loc < j_loc))
        decay = jnp.exp(jnp.where(incl, gc_mat - gc_mat.T, -jnp.inf))
        ks = stack(k).astype(BF16)
        kk = lax.dot_general(ks, ks, nt, preferred_element_type=F32)
        bm = jnp.where(strict, -(b_col * kk * decay), 0.0)
        y = jnp.concatenate([stack(bexp * v), stack(bexp * jnp.exp(gexp) * k)], axis=1)
        pw = bm.astype(BF16)
        for it in range(6):
            y = y + jnp.dot(pw, y.astype(BF16), preferred_element_type=F32)
            if it < 5:
                pw = jnp.dot(pw, pw, preferred_element_type=F32).astype(BF16)
        u_base, w = unstack(y[:, :gw]), unstack(y[:, gw:])
        s = s_s[...]
        sb = s.astype(BF16)
        u = u_base - jnp.dot(w.astype(BF16), sb, preferred_element_type=F32)
        k_tail = k * jnp.exp(glast - gexp)
        pad = jnp.zeros((LANES - c, gw), F32)
        kt_t = jnp.concatenate([k_tail, pad], axis=0).T.astype(BF16)
        u_pad = jnp.concatenate([u, pad], axis=0).astype(BF16)
        s_s[...] = s * jnp.exp(glast) + jnp.where(same_head, jnp.dot(kt_t, u_pad, preferred_element_type=F32), 0.0)
        if not want_out:
            return None
        qs = stack(q).astype(BF16)
        qk = lax.dot_general(qs, ks, nt, preferred_element_type=F32) * decay
        o = jnp.dot((q * jnp.exp(gexp)).astype(BF16), sb, preferred_element_type=F32)
        return o + unstack(jnp.dot(qk.astype(BF16), stack(u).astype(BF16), preferred_element_type=F32))

    def finish(o, g_ref, row, o_ref):
        y = _head_rms_norm(o) * ng_ref[...]
        g = g_ref[pl.ds(row, c), :].astype(F32)
        o_ref[pl.ds(row, c), :] = (y * g * jax.nn.sigmoid(g)).astype(o_ref.dtype)

    lat = (xl_ref, abl_ref)
    ctx = (xc_ref, abc_ref)
    ncl, ncc = nl // c, nc // c

    s_s[...] = jnp.zeros_like(s_s)
    for cc in range(ncc):
        o = step(0, ctx, cc * c, with_ctx_out)
        if with_ctx_out:
            accc_s[cc * c:(cc + 1) * c, :] = o

    def fwd(i, carry):
        row = pl.multiple_of(i * c, c)
        accl_s[pl.ds(row, c), :] = step(0, lat, row, True)
        return carry

    lax.fori_loop(0, ncl, fwd, 0)

    s_s[...] = jnp.zeros_like(s_s)
    for cc in reversed(range(ncc)):
        o = step(1, ctx, cc * c, with_ctx_out)
        if with_ctx_out:
            finish(accc_s[cc * c:(cc + 1) * c, :] + o, gc_ref, cc * c, oc_ref)
    if not with_ctx_out:
        oc_ref[...] = jnp.zeros_like(oc_ref)

    def bwd(t, carry):
        row = pl.multiple_of((ncl - 1 - t) * c, c)
        o = step(1, lat, row, True)
        finish(accl_s[pl.ds(row, c), :] + o, gl_ref, row, ol_ref)
        return carry

    lax.fori_loop(0, ncl, bwd, 0)


def gdn(p_l, ab_l, p_c, ab_c, conv_w, a_log, dt_bias, norm_g, with_ctx_out):
    b, n, _ = p_l.shape
    lc = p_c.shape[1]
    gw = GROUP_W
    x_l, x_c = gdn_prep(p_l, conv_w), gdn_prep(p_c, conv_w)
    par = jnp.zeros((8, LANES), F32)
    par = par.at[0, :2 * GROUP_HEADS].set(jnp.exp(a_log.astype(F32)).reshape(-1))
    par = par.at[1, :2 * GROUP_HEADS].set(dt_bias.astype(F32).reshape(-1))
    ng = jnp.tile(norm_g, GROUP_HEADS)[None, :]
    full = lambda shape: pl.BlockSpec(shape, lambda i: (0,) * len(shape))
    y_l, y_c = pl.pallas_call(
        functools.partial(_gdn_kernel, nl=n, nc=lc, with_ctx_out=with_ctx_out),
        grid=(b,),
        in_specs=[pl.BlockSpec((None, 3, n, gw), lambda i: (i, 0, 0, 0)),
                  pl.BlockSpec((None, n, gw), lambda i: (i, 0, CB_GDN_G // 2)),
                  pl.BlockSpec((None, n, LANES), lambda i: (i, 0, 0)),
                  pl.BlockSpec((None, 3, lc, gw), lambda i: (i, 0, 0, 0)),
                  pl.BlockSpec((None, lc, gw), lambda i: (i, 0, CB_GDN_G // 2)),
                  pl.BlockSpec((None, lc, LANES), lambda i: (i, 0, 0)),
                  full((8, LANES)), full((1, gw))],
        out_specs=[pl.BlockSpec((None, n, gw), lambda i: (i, 0, 0)),
                   pl.BlockSpec((None, lc, gw), lambda i: (i, 0, 0))],
        out_shape=[jax.ShapeDtypeStruct((b, n, gw), BF16), jax.ShapeDtypeStruct((b, lc, gw), BF16)],
        scratch_shapes=[pltpu.VMEM((gw, gw), F32), pltpu.VMEM((n, gw), F32), pltpu.VMEM((lc, gw), F32)],
        compiler_params=_cparams("parallel"),
    )(x_l, p_l, ab_l, x_c, p_c, ab_c, par, ng)
    return y_l, (y_c if with_ctx_out else None)


def _layer_norm(x, g, b):
    mu = jnp.mean(x, -1, keepdims=True)
    xc = x - mu
    var = jnp.mean(xc * xc, -1, keepdims=True)
    return xc * lax.rsqrt(var + LN_EPS) * g + b


def _outproj_kernel(y_ref, h_ref, mod_ref, w_ref, ln_ref, rw_ref, hn_ref, u2_ref, aff_ref, *, alpha):
    mix = jnp.dot(y_ref[...], w_ref[...], preferred_element_type=F32)
    hn = _layer_norm(alpha * h_ref[...] + mod_ref[2:3, :] * mix, ln_ref[0:1, :], ln_ref[1:2, :])
    hn_ref[...] = hn
    u2 = (hn * (1.0 + mod_ref[4:5, :]) + mod_ref[3:4, :]).astype(BF16)
    u2_ref[...] = u2
    logits = lax.dot_general(rw_ref[...], u2, (((1,), (1,)), ((), ())), preferred_element_type=F32)
    m = jnp.max(logits, 0, keepdims=True)
    e = jnp.exp(logits - m)
    aff_ref[...] = e / jnp.sum(e, 0, keepdims=True)


def outproj(y, h, mod, w_bf16, ln, rw_t_bf16, alpha, tm):
    b, n, d = h.shape
    mixw = y.shape[-1]
    ne = rw_t_bf16.shape[0]
    return pl.pallas_call(
        functools.partial(_outproj_kernel, alpha=alpha),
        grid=(b, n // tm),
        in_specs=[
            pl.BlockSpec((None, tm, mixw), lambda i, j: (i, j, 0)),
            pl.BlockSpec((None, tm, d), lambda i, j: (i, j, 0)),
            pl.BlockSpec((None, 6, d), lambda i, j: (i, 0, 0)),
            pl.BlockSpec((mixw, d), lambda i, j: (0, 0)),
            pl.BlockSpec((2, d), lambda i, j: (0, 0)),
            pl.BlockSpec((ne, d), lambda i, j: (0, 0)),
        ],
        out_specs=[
            pl.BlockSpec((None, tm, d), lambda i, j: (i, j, 0)),
            pl.BlockSpec((None, tm, d), lambda i, j: (i, j, 0)),
            pl.BlockSpec((None, ne, tm), lambda i, j: (i, 0, j)),
        ],
        out_shape=[
            jax.ShapeDtypeStruct((b, n, d), F32),
            jax.ShapeDtypeStruct((b, n, d), BF16),
            jax.ShapeDtypeStruct((b, ne, n), F32),
        ],
        compiler_params=_cparams("parallel", "parallel"),
    )(y, h, mod, w_bf16, ln, rw_t_bf16)


def _ffn_kernel(x_ref, wg_ref, wu_ref, wd_ref, o_ref, *, rows):
    f = pl.program_id(2)
    tb, cap, d = x_ref.shape

    @pl.when(f == 0)
    def _():
        o_ref[...] = jnp.zeros_like(o_ref)

    wg = wg_ref[...].astype(BF16)
    wu = wu_ref[...].astype(BF16)
    wd = wd_ref[...].astype(BF16)
    for bi in range(tb):
        for r in range(cap // rows):
            x = x_ref[bi, r * rows:(r + 1) * rows, :]
            g = jnp.dot(x, wg, preferred_element_type=F32)
            u = jnp.dot(x, wu, preferred_element_type=F32)
            hid = (g * jax.nn.sigmoid(g) * u).astype(BF16)
            o_ref[bi, r * rows:(r + 1) * rows, :] += jnp.dot(hid, wd, preferred_element_type=F32)


def expert_ffn(xin, w_gate, w_up, w_down, tb, tf, rows):
    b, ne, cap, d = xin.shape
    ff = w_gate.shape[-1]
    return pl.pallas_call(
        functools.partial(_ffn_kernel, rows=rows),
        grid=(ne, b // tb, ff // tf),
        in_specs=[
            pl.BlockSpec((tb, None, cap, d), lambda e, i, f: (i, e, 0, 0)),
            pl.BlockSpec((None, d, tf), lambda e, i, f: (e, 0, f)),
            pl.BlockSpec((None, d, tf), lambda e, i, f: (e, 0, f)),
            pl.BlockSpec((None, tf, d), lambda e, i, f: (e, f, 0)),
        ],
        out_specs=pl.BlockSpec((tb, None, cap, d), lambda e, i, f: (i, e, 0, 0)),
        out_shape=jax.ShapeDtypeStruct((b, ne, cap, d), F32),
        compiler_params=_cparams("parallel", "parallel", "arbitrary"),
    )(xin, w_gate, w_up, w_down)


def layer_norm(x, g, b):
    mu = jnp.mean(x, -1, keepdims=True)
    var = jnp.mean(jnp.square(x - mu), -1, keepdims=True)
    return (x - mu) * lax.rsqrt(var + LN_EPS) * g + b


def rms_norm(x, g=None, eps=RMS_EPS):
    y = x * lax.rsqrt(jnp.mean(jnp.square(x), -1, keepdims=True) + eps)
    if g is not None:
        y = y * g
    return y


def l2_normalize(x, eps=1e-6):
    return x * lax.rsqrt(jnp.sum(jnp.square(x), -1, keepdims=True) + eps)


def axial_rope_angles(n, rot_dim):
    rows = n // GRID_W
    row = jnp.repeat(jnp.arange(rows, dtype=F32), GRID_W)
    col = jnp.tile(jnp.arange(GRID_W, dtype=F32), rows)
    n_freq = rot_dim // 4
    inv = ROPE_BASE ** (-jnp.arange(n_freq, dtype=F32) / n_freq)
    return jnp.concatenate([row[:, None] * inv, col[:, None] * inv], -1)


def retention_angles(n, dim):
    theta = 1.0 / (ROPE_BASE ** jnp.linspace(0.0, 1.0, dim // 2, dtype=F32))
    return jnp.arange(n, dtype=F32)[:, None] * theta


def rotate(x, ang):
    half = x.shape[-1] // 2
    cos, sin = jnp.cos(ang).astype(x.dtype), jnp.sin(ang).astype(x.dtype)
    x1, x2 = x[..., :half], x[..., half:]
    return jnp.concatenate([x1 * cos - x2 * sin, x1 * sin + x2 * cos], -1)


def flip_seq(t):
    return jnp.flip(t, axis=2)


def retention_scan(q, k, v, log_g, s0, with_out):
    b, h, n, dk = q.shape
    dv = v.shape[-1]
    c = RET_CHUNK
    nc = n // c
    qc, kc, vc = (t.reshape(b, h, nc, c, t.shape[-1]) for t in (q, k, v))
    pos = jnp.arange(c, dtype=F32)
    lg = log_g[:, None]
    k_decay = jnp.exp(lg * (c - 1 - pos))
    kv = jnp.einsum('bhncd,hc,bhnce->nbhde', kc, k_decay, vc)
    chunk_decay = jnp.exp(log_g * c)[None, :, None, None]

    def step(s, kv_n):
        return s * chunk_decay + kv_n, (s if with_out else None)

    s_last, s_prev = lax.scan(step, s0, kv)
    if not with_out:
        return None, s_last
    rel = pos[:, None] - pos[None, :]
    d_mat = jnp.where(rel >= 0, jnp.exp(lg[:, :, None] * jnp.maximum(rel, 0.0)), 0.0)
    q_decay = jnp.exp(lg * (pos + 1.0))
    scores = jnp.einsum('bhnid,bhnjd->bhnij', qc, kc) * d_mat[:, None]
    o = (jnp.einsum('bhnij,bhnje->bhnie', scores, vc)
         + jnp.einsum('bhnid,hi,nbhde->bhnie', qc, q_decay, s_prev))
    return o.reshape(b, h, n, dv), s_last


def retention_group(p_l, p_c, ret_decay, with_ctx_out, gh):
    def heads(t):
        return t.reshape(t.shape[0], t.shape[1], gh, HEAD_DIM).transpose(0, 2, 1, 3)
    scale = HEAD_DIM ** -0.5
    ang = retention_angles(p_l['ret_q'].shape[1], HEAD_DIM)
    ql = rotate(heads(p_l['ret_q']) * scale, ang)
    kl = rotate(heads(p_l['ret_k']), ang)
    vl = heads(p_l['ret_v'])
    qc, kc, vc = heads(p_c['ret_q']) * scale, heads(p_c['ret_k']), heads(p_c['ret_v'])
    log_g = jax.nn.log_sigmoid(ret_decay)
    s0 = jnp.zeros((ql.shape[0], gh, HEAD_DIM, HEAD_DIM), F32)
    o_l, o_c = 0.0, 0.0
    for dr in range(2):
        f = flip_seq if dr else (lambda t: t)
        oc, s_ctx = retention_scan(f(qc), f(kc), f(vc), log_g[dr], s0, with_ctx_out)
        ol, _ = retention_scan(f(ql), f(kl), f(vl), log_g[dr], s_ctx, True)
        o_l = o_l + f(ol)
        if with_ctx_out:
            o_c = o_c + f(oc)

    def finish(o, gate):
        y = rms_norm(o).transpose(0, 2, 1, 3).reshape(gate.shape)
        return jax.nn.silu(gate) * y

    return finish(o_l, p_l['ret_g']), (finish(o_c, p_c['ret_g']) if with_ctx_out else None)


def short_conv(x, w):
    return lax.conv_general_dilated(
        x, w[:, None, :], window_strides=(1,),
        padding=[(SHORT_CONV // 2, SHORT_CONV // 2)],
        dimension_numbers=('NWC', 'WIO', 'NWC'), feature_group_count=x.shape[-1])


def gdn_scan(q, k, v, log_a, beta, s0, with_out):
    b, h, n, dk = q.shape
    dv = v.shape[-1]
    c = GDN_CHUNK
    nc = n // c
    q, k, v = (t.reshape(b, h, nc, c, t.shape[-1]) for t in (q, k, v))
    log_a, beta = (t.reshape(b, h, nc, c) for t in (log_a, beta))
    g = jnp.cumsum(log_a, -1)
    idx = jnp.arange(c)
    incl = idx[:, None] >= idx[None, :]
    strict = idx[:, None] > idx[None, :]
    decay = jnp.exp(jnp.where(incl, g[..., :, None] - g[..., None, :], -jnp.inf))
    a_mat = jnp.where(strict, beta[..., :, None] * jnp.einsum('bhnid,bhnjd->bhnij', k, k) * decay, 0.0)
    rhs = jnp.concatenate([beta[..., None] * v, (beta * jnp.exp(g))[..., None] * k], -1)
    sol = lax.linalg.triangular_solve(a_mat + jnp.eye(c, dtype=a_mat.dtype), rhs, left_side=True, lower=True)
    u_base, w = sol[..., :dv], sol[..., dv:]
    k_tail = k * jnp.exp(g[..., -1:] - g)[..., None]
    chunk_decay = jnp.exp(g[..., -1])
    mv = lambda t: jnp.moveaxis(t, 2, 0)
    xs = (mv(u_base), mv(w), mv(k_tail), mv(chunk_decay))
    if with_out:
        qk = jnp.einsum('bhnid,bhnjd->bhnij', q, k) * decay
        xs = xs + (mv(qk), mv(q * jnp.exp(g)[..., None]))

    def step(s, xc):
        u = xc[0] - jnp.einsum('bhck,bhkv->bhcv', xc[1], s)
        s_new = s * xc[3][..., None, None] + jnp.einsum('bhck,bhcv->bhkv', xc[2], u)
        if not with_out:
            return s_new, None
        o = jnp.einsum('bhck,bhkv->bhcv', xc[5], s) + jnp.einsum('bhcj,bhjv->bhcv', xc[4], u)
        return s_new, o

    s_last, o = lax.scan(step, s0, xs)
    if not with_out:
        return None, s_last
    return jnp.moveaxis(o, 0, 2).reshape(b, h, n, dv), s_last


def gdn_group(p_l, p_c, conv_w, a_log, dt_bias, norm_g, with_ctx_out, gh):
    def prep(p):
        b, n, _ = p['gdn_qkv'].shape
        qkv = jax.nn.silu(short_conv(p['gdn_qkv'], conv_w))
        qkv = qkv.reshape(b, n, 3, gh, HEAD_DIM).transpose(2, 0, 3, 1, 4)
        q = l2_normalize(qkv[0]) * HEAD_DIM ** -0.5
        k = l2_normalize(qkv[1])
        a = p['gdn_a'].reshape(b, n, 2, gh).transpose(2, 0, 3, 1)
        log_a = -jnp.exp(a_log)[:, None, :, None] * jax.nn.softplus(a + dt_bias[:, None, :, None])
        beta = jax.nn.sigmoid(p['gdn_b'].reshape(b, n, 2, gh).transpose(2, 0, 3, 1))
        return q, k, qkv[2], log_a, beta

    ql, kl, vl, la_l, bt_l = prep(p_l)
    qc, kc, vc, la_c, bt_c = prep(p_c)
    s0 = jnp.zeros((ql.shape[0], gh, HEAD_DIM, HEAD_DIM), F32)
    o_l, o_c = 0.0, 0.0
    for dr in range(2):
        f = flip_seq if dr else (lambda t: t)
        oc, s_ctx = gdn_scan(f(qc), f(kc), f(vc), f(la_c[dr]), f(bt_c[dr]), s0, with_ctx_out)
        ol, _ = gdn_scan(f(ql), f(kl), f(vl), f(la_l[dr]), f(bt_l[dr]), s_ctx, True)
        o_l = o_l + f(ol)
        if with_ctx_out:
            o_c = o_c + f(oc)

    def finish(o, gate):
        o = o.transpose(0, 2, 1, 3)
        gt = gate.reshape(o.shape)
        return (rms_norm(o, norm_g) * jax.nn.silu(gt)).reshape(gate.shape)

    return finish(o_l, p_l['gdn_g']), (finish(o_c, p_c['gdn_g']) if with_ctx_out else None)


def expert_choice_ffn(u2, aff_t, w_gate, w_up, w_down):
    b, n, d = u2.shape
    ne = aff_t.shape[1]
    cap = CAPACITY_FACTOR * n // ne
    weight, idx = lax.top_k(aff_t, cap)
    xin = jax.vmap(lambda ub, ib: ub[ib])(u2, idx)
    tb = 4 if cap >= 512 else b
    y = expert_ffn(xin, w_gate, w_up, w_down, tb=tb, tf=512, rows=min(cap, 256))
    y = y * weight[..., None]
    flat = (jnp.arange(b)[:, None, None] * n + idx).reshape(-1)
    return jnp.zeros((b * n, d), F32).at[flat].add(y.reshape(-1, d)).reshape(b, n, d)


def _jax_views(p, ab, inv_ret):
    blk = lambda cb, w: p[..., cb * LANES:cb * LANES + w].astype(F32)
    gh = GROUP_HEADS
    return {'ret_q': blk(CB_RET_Q, GROUP_W)[..., inv_ret], 'ret_k': blk(CB_RET_K, GROUP_W)[..., inv_ret],
            'ret_v': blk(CB_RET_V, GROUP_W), 'ret_g': blk(CB_RET_G, GROUP_W),
            'gdn_qkv': blk(CB_GDN_QKV, 3 * GROUP_W), 'gdn_g': blk(CB_GDN_G, GROUP_W),
            'gdn_a': ab[..., :2 * gh], 'gdn_b': ab[..., 2 * gh:4 * gh]}


def kernel(x, c, ctx, c_ctx, ada_w, ada_b, w_in, w_out, ret_decay, diff_lambda, diff_norm, gdn_conv,
           gdn_a_log, gdn_dt_bias, gdn_norm, swa_sink, ln_g, ln_b, router_w, w_gate, w_up, w_down):
    b, n, d = x.shape
    lc = ctx.shape[1]
    depth = ada_w.shape[0]
    gh = GROUP_HEADS
    alpha = (2 * depth) ** 0.25
    main_perm, tail_perm, off = _in_col_perm()
    inv_ret = np.argsort(main_perm[:GROUP_W] - off['ret_q'])
    ang_diff = axial_rope_angles(n, HEAD_DIM // 2)
    ang_swa = axial_rope_angles(n, HEAD_DIM)
    cos_diff, sin_diff = jnp.tile(jnp.cos(ang_diff), (1, 8)), jnp.tile(jnp.sin(ang_diff), (1, 8))
    cos_swa = jnp.tile(jnp.cos(ang_swa), (1, 4))
    sin_swa = jnp.tile(jnp.sin(ang_swa), (1, 4)) * jnp.where(jnp.arange(LANES) < LANES // 2, -1.0, 1.0)
    ang_ret = retention_angles(n, HEAD_DIM)
    cos_ret, sin_ret = jnp.tile(jnp.cos(ang_ret), (1, 4)), jnp.tile(jnp.sin(ang_ret), (1, 4))
    cond_l = jax.nn.silu(c)
    cond_c = jax.nn.silu(c_ctx)
    h, hc = x, ctx
    for layer in range(depth):
        full_ctx = layer < depth - 1
        mod_l = (cond_l @ ada_w[layer] + ada_b[layer]).reshape(b, 6, d)
        mod_c = jnp.broadcast_to((cond_c @ ada_w[layer] + ada_b[layer]).reshape(1, 6, d), (b, 6, d))
        wl = w_in[layer]
        w_in_b = jnp.concatenate([wl[:, main_perm], wl[:, tail_perm],
                                  jnp.zeros((d, LANES - tail_perm.size), F32)], axis=1).astype(BF16)
        w_out_b = w_out[layer].astype(BF16)
        rw_t = router_w[layer].T.astype(BF16)
        ln1 = jnp.stack([ln_g[layer, 0], ln_b[layer, 0]])
        ln2g, ln2b = ln_g[layer, 1], ln_b[layer, 1]
        p_l, ab_l = inproj(h, mod_l, w_in_b, tm=512)
        p_c, ab_c = inproj(hc, mod_c, w_in_b, tm=lc)
        ya_l, ya_c = retention(p_l, p_c, cos_ret, sin_ret, ret_decay[layer], full_ctx)
        yb_l, yb_c = diff_attention(p_l, p_c, cos_diff, sin_diff, diff_lambda[layer], diff_norm[layer],
                                    layer, full_ctx, tq=256)
        yc_l, yc_c = gdn(p_l, ab_l, p_c, ab_c, gdn_conv[layer], gdn_a_log[layer], gdn_dt_bias[layer],
                         gdn_norm[layer], full_ctx)
        yd_l, yd_c = swa_attention(p_l, p_c, cos_swa, sin_swa, swa_sink[layer], full_ctx, tq=256)
        y_l = jnp.concatenate([ya_l, yb_l, yc_l, yd_l], -1)
        h, u2, aff = outproj(y_l, h, mod_l, w_out_b, ln1, rw_t, alpha, tm=512)
        f_l = expert_choice_ffn(u2, aff, w_gate[layer], w_up[layer], w_down[layer])
        h = layer_norm(alpha * h + mod_l[:, None, 5] * f_l, ln2g, ln2b)
        if full_ctx:
            y_c = jnp.concatenate([ya_c, yb_c, yc_c, yd_c], -1)
            hc, u2c, affc = outproj(y_c, hc, mod_c, w_out_b, ln1, rw_t, alpha, tm=lc)
            f_c = expert_choice_ffn(u2c, affc, w_gate[layer], w_up[layer], w_down[layer])
            hc = layer_norm(alpha * hc + mod_c[:, None, 5] * f_c, ln2g, ln2b)
    return h
```

```python
import functools
import math

import numpy as np
import jax
import jax.numpy as jnp
from jax import lax
from jax.experimental import pallas as pl
from jax.experimental.pallas import tpu as pltpu

F32 = jnp.float32
BF16 = jnp.bfloat16

GRID_W = 64
HEAD_DIM = 64
N_MIXERS = 4
GROUP_HEADS = 4
GROUP_W = GROUP_HEADS * HEAD_DIM
ROPE_BASE = 10000.0
RET_CHUNK = 128
GDN_CHUNK = 64
SHORT_CONV = 3
WINDOW = 128
N_EXPERTS = 16
CAPACITY_FACTOR = 2
LN_EPS = 1e-5
RMS_EPS = 1e-6
LANES = 128

VMEM_LIMIT_BYTES = 56 * 1024 * 1024

CB_RET_Q, CB_RET_K, CB_RET_V, CB_RET_G = 0, 2, 4, 6
CB_DIFF_Q, CB_DIFF_K, CB_DIFF_V = 8, 10, 12
CB_GDN_QKV, CB_GDN_G = 14, 20
CB_SWA_Q, CB_SWA_K, CB_SWA_V = 22, 24, 25
MAIN_W = 26 * LANES


def _cparams(*sem):
    return pltpu.CompilerParams(dimension_semantics=sem, vmem_limit_bytes=VMEM_LIMIT_BYTES)


def _split(x, n):
    if x.dtype == BF16:
        return [x]
    terms, r = [], x
    for _ in range(n):
        t = r.astype(BF16)
        terms.append(t)
        r = r - t.astype(F32)
    return terms


def _mm(a, b, na=1, nb=1, dims=None):
    at, bt = _split(a, na), _split(b, nb)
    out = None
    for i, x in enumerate(at):
        for j, y in enumerate(bt):
            if i + j >= max(len(at), len(bt)):
                continue
            t = (jnp.dot(x, y, preferred_element_type=F32) if dims is None
                 else lax.dot_general(x, y, dims, preferred_element_type=F32))
            out = t if out is None else out + t
    return out


def _in_col_perm():
    gw, gh, hd = GROUP_W, GROUP_HEADS, HEAD_DIM
    off = {}
    o = 0
    for name, w in (('ret_q', gw), ('ret_k', gw), ('ret_v', gw), ('ret_g', gw), ('diff_q', gw), ('diff_k', gw),
                    ('diff_v', gw), ('gdn_qkv', 3 * gw), ('gdn_g', gw), ('gdn_a', 2 * gh), ('gdn_b', 2 * gh),
                    ('swa_q', gw), ('swa_k', gw // 2), ('swa_v', gw // 2)):
        off[name] = o
        o += w
    ident = lambda name, w: off[name] + np.arange(w)
    ret = np.zeros(gw, np.int64)
    for h in range(gh):
        for j in range(hd):
            ret[(j // 32) * 128 + h * 32 + j % 32] = h * hd + j
    dif = np.zeros(gw, np.int64)
    for h in range(gh):
        for c in range(2):
            for j in range(32):
                dif[(j // 16) * 128 + (h * 2 + c) * 16 + j % 16] = h * hd + c * 32 + j
    swq = np.zeros(gw, np.int64)
    for hk in range(2):
        for g in range(2):
            for j in range(hd):
                swq[g * 128 + (j // 32) * 64 + hk * 32 + j % 32] = hk * 128 + g * hd + j
    swk = np.zeros(gw // 2, np.int64)
    for hk in range(2):
        for j in range(hd):
            swk[(j // 32) * 64 + hk * 32 + j % 32] = hk * hd + j
    main = np.concatenate([
        off['ret_q'] + ret, off['ret_k'] + ret, ident('ret_v', gw), ident('ret_g', gw),
        off['diff_q'] + dif, off['diff_k'] + dif, ident('diff_v', gw),
        ident('gdn_qkv', 3 * gw), ident('gdn_g', gw),
        off['swa_q'] + swq, off['swa_k'] + swk, ident('swa_v', gw // 2)])
    tail = np.concatenate([ident('gdn_a', 2 * gh), ident('gdn_b', 2 * gh)])
    return main, tail, off


def _inproj_kernel(h_ref, mod_ref, w_ref, o_ref, ab_ref, *, tn):
    u = (h_ref[...] * (1.0 + mod_ref[1:2, :]) + mod_ref[0:1, :]).astype(BF16)
    nmain = o_ref.shape[-1]
    for j in range(nmain // tn):
        o_ref[:, j * tn:(j + 1) * tn] = jnp.dot(
            u, w_ref[:, j * tn:(j + 1) * tn], preferred_element_type=F32).astype(BF16)
    ab_ref[...] = jnp.dot(u, w_ref[:, nmain:], preferred_element_type=F32)


def inproj(h, mod, w_bf16, tm):
    b, n, d = h.shape
    nw = w_bf16.shape[1]
    return pl.pallas_call(
        functools.partial(_inproj_kernel, tn=256),
        grid=(b, n // tm),
        in_specs=[
            pl.BlockSpec((None, tm, d), lambda i, j: (i, j, 0)),
            pl.BlockSpec((None, 6, d), lambda i, j: (i, 0, 0)),
            pl.BlockSpec((d, nw), lambda i, j: (0, 0)),
        ],
        out_specs=[pl.BlockSpec((None, tm, MAIN_W), lambda i, j: (i, j, 0)),
                   pl.BlockSpec((None, tm, LANES), lambda i, j: (i, j, 0))],
        out_shape=[jax.ShapeDtypeStruct((b, n, MAIN_W), BF16),
                   jax.ShapeDtypeStruct((b, n, LANES), F32)],
        compiler_params=_cparams("parallel", "parallel"),
    )(h, mod, w_bf16)


def _head_mean_matrix():
    r = lax.broadcasted_iota(jnp.int32, (GROUP_W, GROUP_W), 0) // HEAD_DIM
    c = lax.broadcasted_iota(jnp.int32, (GROUP_W, GROUP_W), 1) // HEAD_DIM
    return jnp.where(r == c, 1.0 / HEAD_DIM, 0.0).astype(F32)


def _head_rms_norm(o):
    ms = jnp.dot(o * o, _head_mean_matrix(), preferred_element_type=F32, precision=lax.Precision.HIGHEST)
    return o * lax.rsqrt(ms + RMS_EPS)


def _diff_kernel(*refs, latent, lam_init, scale, nl, nc):
    if latent:
        (q_ref, kl_ref, kc_ref, vl_ref, vc_ref, cq_ref, sq_ref, ck_ref, sk_ref, lp_ref, ng_ref,
         o_ref, kt_s, acc_s) = refs
    else:
        q_ref, kc_ref, vc_ref, lp_ref, ng_ref, o_ref, kt_s, acc_s = refs
    half = GROUP_W // 2
    lane = lax.broadcasted_iota(jnp.int32, (1, GROUP_W), 1)

    def rot(x, cos, sin):
        x1, x2 = x[:, :half], x[:, half:]
        return jnp.concatenate([x1 * cos - x2 * sin, x1 * sin + x2 * cos], axis=1)

    def init_keys():
        if latent:
            ck = 512 if nl % 512 == 0 else nl
            for c0 in range(0, nl, ck):
                kr = rot(kl_ref[c0:c0 + ck, :].astype(F32), ck_ref[c0:c0 + ck, :], sk_ref[c0:c0 + ck, :])
                kt_s[:, c0:c0 + ck] = kr.T.astype(BF16)
        off = nl if latent else 0
        kt_s[:, off:off + nc] = kc_ref[...].astype(F32).T.astype(BF16)

    if latent:
        pl.when(pl.program_id(1) == 0)(init_keys)
    else:
        init_keys()

    lp = lp_ref[...]
    lam = (jnp.exp(jnp.sum(lp[0:1] * lp[1:2], axis=1, keepdims=True))
           - jnp.exp(jnp.sum(lp[2:3] * lp[3:4], axis=1, keepdims=True)) + lam_init)
    q = q_ref[...].astype(F32)
    if latent:
        q = rot(q, cq_ref[...], sq_ref[...])
    q = q * scale
    acc_s[...] = jnp.zeros_like(acc_s)

    def probs(hc):
        qm = jnp.where((lane % half) // 16 == hc, q, 0.0).astype(BF16)
        s = jnp.dot(qm, kt_s[...], preferred_element_type=F32)
        e = jnp.exp(s - jnp.max(s, axis=1, keepdims=True))
        return e, 1.0 / jnp.sum(e, axis=1, keepdims=True)

    def body(h, carry):
        e0, r0 = probs(2 * h)
        e1, r1 = probs(2 * h + 1)
        a = (e0 * r0 - e1 * (lam * r1)).astype(BF16)
        if latent:
            r = (jnp.dot(a[:, :nl], vl_ref[...], preferred_element_type=F32)
                 + jnp.dot(a[:, nl:], vc_ref[...], preferred_element_type=F32))
        else:
            r = jnp.dot(a, vc_ref[...], preferred_element_type=F32)
        acc_s[...] += jnp.where(lane // HEAD_DIM == h, r, 0.0)
        return carry

    lax.fori_loop(0, GROUP_HEADS, body, 0)
    y = _head_rms_norm(acc_s[...]) * ng_ref[...] * (1.0 - lam_init)
    o_ref[...] = y.astype(o_ref.dtype)


def diff_attention(p_l, p_c, cos_l, sin_l, lam_params, norm_g, layer_idx, with_ctx_out, tq):
    b, n, _ = p_l.shape
    lc = p_c.shape[1]
    lam_init = 0.8 - 0.6 * math.exp(-0.3 * layer_idx)
    scale = (HEAD_DIM // 2) ** -0.5
    ng = jnp.tile(norm_g, GROUP_HEADS)[None, :]
    gw = GROUP_W
    blk = gw // LANES
    kw = dict(lam_init=lam_init, scale=scale, nl=n, nc=lc)
    y_l = pl.pallas_call(
        functools.partial(_diff_kernel, latent=True, **kw),
        grid=(b, n // tq),
        in_specs=[
            pl.BlockSpec((None, tq, gw), lambda i, j: (i, j, CB_DIFF_Q // blk)),
            pl.BlockSpec((None, n, gw), lambda i, j: (i, 0, CB_DIFF_K // blk)),
            pl.BlockSpec((None, lc, gw), lambda i, j: (i, 0, CB_DIFF_K // blk)),
            pl.BlockSpec((None, n, gw), lambda i, j: (i, 0, CB_DIFF_V // blk)),
            pl.BlockSpec((None, lc, gw), lambda i, j: (i, 0, CB_DIFF_V // blk)),
            pl.BlockSpec((tq, LANES), lambda i, j: (j, 0)),
            pl.BlockSpec((tq, LANES), lambda i, j: (j, 0)),
            pl.BlockSpec((n, LANES), lambda i, j: (0, 0)),
            pl.BlockSpec((n, LANES), lambda i, j: (0, 0)),
            pl.BlockSpec((4, HEAD_DIM // 2), lambda i, j: (0, 0)),
            pl.BlockSpec((1, gw), lambda i, j: (0, 0)),
        ],
        out_specs=pl.BlockSpec((None, tq, gw), lambda i, j: (i, j, 0)),
        out_shape=jax.ShapeDtypeStruct((b, n, gw), BF16),
        scratch_shapes=[pltpu.VMEM((gw, n + lc), BF16), pltpu.VMEM((tq, gw), F32)],
        compiler_params=_cparams("parallel", "arbitrary"),
    )(p_l, p_l, p_c, p_l, p_c, cos_l, sin_l, cos_l, sin_l, lam_params, ng)
    y_c = None
    if with_ctx_out:
        y_c = pl.pallas_call(
            functools.partial(_diff_kernel, latent=False, **kw),
            grid=(b,),
            in_specs=[
                pl.BlockSpec((None, lc, gw), lambda i: (i, 0, CB_DIFF_Q // blk)),
                pl.BlockSpec((None, lc, gw), lambda i: (i, 0, CB_DIFF_K // blk)),
                pl.BlockSpec((None, lc, gw), lambda i: (i, 0, CB_DIFF_V // blk)),
                pl.BlockSpec((4, HEAD_DIM // 2), lambda i: (0, 0)),
                pl.BlockSpec((1, gw), lambda i: (0, 0)),
            ],
            out_specs=pl.BlockSpec((None, lc, gw), lambda i: (i, 0, 0)),
            out_shape=jax.ShapeDtypeStruct((b, lc, gw), BF16),
            scratch_shapes=[pltpu.VMEM((gw, lc), BF16), pltpu.VMEM((lc, gw), F32)],
            compiler_params=_cparams("parallel"),
        )(p_c, p_c, p_c, lam_params, ng)
    return y_l, y_c


def _swa_kernel(*refs, latent, nl, tq, win):
    if latent:
        q_ref, kl_ref, kc_ref, vl_ref, vc_ref, cq_ref, sq_ref, ck_ref, sk_ref, sink_ref, o_ref, kr_s = refs
    else:
        q_ref, kc_ref, vc_ref, sink_ref, o_ref = refs
    hk_w = LANES // 2
    lane = lax.broadcasted_iota(jnp.int32, (1, LANES), 1)
    scale = HEAD_DIM ** -0.5

    def rot(x, cos, sin_signed):
        return x * cos + pltpu.roll(x, hk_w, 1) * sin_signed

    if latent:
        @pl.when(pl.program_id(1) == 0)
        def _():
            kr_s[...] = rot(kl_ref[...].astype(F32), ck_ref[...], sk_ref[...]).astype(BF16)

        j = pl.program_id(1)
        ws = pl.multiple_of(jnp.clip(j * tq - WINDOW, 0, nl - win), WINDOW)
        kwin = kr_s[pl.ds(ws, win), :]
        vwin = vl_ref[pl.ds(ws, win), :]
        qpos = j * tq + lax.broadcasted_iota(jnp.int32, (tq, 1), 0)
        kpos = ws + lax.broadcasted_iota(jnp.int32, (1, win), 1)
        valid = jnp.abs(qpos - kpos) <= WINDOW
    kc = kc_ref[...]
    vc = vc_ref[...]
    nt = (((1,), (1,)), ((), ()))
    for hk in range(2):
        pieces = []
        for g in range(2):
            qg = q_ref[:, g * LANES:(g + 1) * LANES].astype(F32)
            if latent:
                qg = rot(qg, cq_ref[...], sq_ref[...])
            qm = jnp.where((lane % hk_w) // 32 == hk, qg * scale, 0.0).astype(BF16)
            snk = sink_ref[0:1, hk * 2 + g:hk * 2 + g + 1]
            sc = lax.dot_general(qm, kc, nt, preferred_element_type=F32)
            m = jnp.maximum(jnp.max(sc, axis=1, keepdims=True), snk)
            if latent:
                sb = lax.dot_general(qm, kwin, nt, preferred_element_type=F32)
                sb = jnp.where(valid, sb, -jnp.inf)
                m = jnp.maximum(m, jnp.max(sb, axis=1, keepdims=True))
            ec = jnp.exp(sc - m)
            den = jnp.sum(ec, axis=1, keepdims=True) + jnp.exp(snk - m)
            o = jnp.dot(ec.astype(BF16), vc, preferred_element_type=F32)
            if latent:
                eb = jnp.exp(sb - m)
                den = den + jnp.sum(eb, axis=1, keepdims=True)
                o = o + jnp.dot(eb.astype(BF16), vwin, preferred_element_type=F32)
            o = o / den
            if hk != g:
                o = pltpu.roll(o, HEAD_DIM, 1)
            pieces.append(o)
        o_ref[:, hk * LANES:(hk + 1) * LANES] = jnp.where(lane < HEAD_DIM, pieces[0], pieces[1]).astype(o_ref.dtype)


def swa_attention(p_l, p_c, cos_l, sin_l, sink, with_ctx_out, tq):
    b, n, _ = p_l.shape
    lc = p_c.shape[1]
    gw = GROUP_W
    win = tq + 2 * WINDOW
    snk = jnp.zeros((1, LANES), F32).at[0, :GROUP_HEADS].set(sink)
    y_l = pl.pallas_call(
        functools.partial(_swa_kernel, latent=True, nl=n, tq=tq, win=win),
        grid=(b, n // tq),
        in_specs=[
            pl.BlockSpec((None, tq, gw), lambda i, j: (i, j, CB_SWA_Q // 2)),
            pl.BlockSpec((None, n, LANES), lambda i, j: (i, 0, CB_SWA_K)),
            pl.BlockSpec((None, lc, LANES), lambda i, j: (i, 0, CB_SWA_K)),
            pl.BlockSpec((None, n, LANES), lambda i, j: (i, 0, CB_SWA_V)),
            pl.BlockSpec((None, lc, LANES), lambda i, j: (i, 0, CB_SWA_V)),
            pl.BlockSpec((tq, LANES), lambda i, j: (j, 0)),
            pl.BlockSpec((tq, LANES), lambda i, j: (j, 0)),
            pl.BlockSpec((n, LANES), lambda i, j: (0, 0)),
            pl.BlockSpec((n, LANES), lambda i, j: (0, 0)),
            pl.BlockSpec((1, LANES), lambda i, j: (0, 0)),
        ],
        out_specs=pl.BlockSpec((None, tq, gw), lambda i, j: (i, j, 0)),
        out_shape=jax.ShapeDtypeStruct((b, n, gw), BF16),
        scratch_shapes=[pltpu.VMEM((n, LANES), BF16)],
        compiler_params=_cparams("parallel", "arbitrary"),
    )(p_l, p_l, p_c, p_l, p_c, cos_l, sin_l, cos_l, sin_l, snk)
    y_c = None
    if with_ctx_out:
        y_c = pl.pallas_call(
            functools.partial(_swa_kernel, latent=False, nl=n, tq=lc, win=win),
            grid=(b,),
            in_specs=[
                pl.BlockSpec((None, lc, gw), lambda i: (i, 0, CB_SWA_Q // 2)),
                pl.BlockSpec((None, lc, LANES), lambda i: (i, 0, CB_SWA_K)),
                pl.BlockSpec((None, lc, LANES), lambda i: (i, 0, CB_SWA_V)),
                pl.BlockSpec((1, LANES), lambda i: (0, 0)),
            ],
            out_specs=pl.BlockSpec((None, lc, gw), lambda i: (i, 0, 0)),
            out_shape=jax.ShapeDtypeStruct((b, lc, gw), BF16),
            compiler_params=_cparams("parallel"),
        )(p_c, p_c, p_c, snk)
    return y_l, y_c


def _ret_tables(ret_decay):
    c = RET_CHUNK
    lg = jax.nn.log_sigmoid(ret_decay.astype(F32))
    pos = jnp.arange(c, dtype=F32)
    head_k = (np.arange(GROUP_W) % LANES) // (HEAD_DIM // 2)
    head_v = np.arange(GROUP_W) // HEAD_DIM
    qd, kd, dm, cd = [], [], [], []
    for d in range(2):
        l = lg[d]
        qpow = (pos + 1.0) if d == 0 else (c - pos)
        kpow = (c - 1.0 - pos) if d == 0 else pos
        qd.append(jnp.exp(qpow[:, None] * l[head_k][None, :]))
        kd.append(jnp.exp(kpow[:, None] * l[head_k][None, :]))
        rel = pos[:, None] - pos[None, :]
        rel = rel if d == 0 else -rel
        dm.append(jnp.where(rel[None] >= 0, jnp.exp(l[:, None, None] * jnp.maximum(rel, 0.0)[None]), 0.0))
        cd.append(jnp.exp(l * c)[head_v][None, :])
    return jnp.stack(qd), jnp.stack(kd), jnp.stack(dm), jnp.stack(cd)


def _ret_kernel(ql_ref, kl_ref, vl_ref, gl_ref, qc_ref, kc_ref, vc_ref, gc_ref, cos_ref, sin_ref,
                qd_ref, kd_ref, dm_ref, cd_ref, ol_ref, oc_ref, s_s, accl_s, accc_s, *, nl, nc, with_ctx_out):
    c = RET_CHUNK
    half = GROUP_W // 2
    scale = HEAD_DIM ** -0.5
    lane = lax.broadcasted_iota(jnp.int32, (1, GROUP_W), 1)
    head_k = (lane % LANES) // (HEAD_DIM // 2)
    head_v = lane // HEAD_DIM
    row_hk = (lax.broadcasted_iota(jnp.int32, (GROUP_W, 1), 0) % LANES) // (HEAD_DIM // 2)
    same_head = row_hk == head_v
    nt = (((1,), (1,)), ((), ()))

    def rot(x, cos, sin):
        x1, x2 = x[:, :half], x[:, half:]
        return jnp.concatenate([x1 * cos - x2 * sin, x1 * sin + x2 * cos], axis=1)

    def step(d, refs, row, rotate, want_out):
        q_ref, k_ref, v_ref = refs
        q = q_ref[pl.ds(row, c), :].astype(F32)
        k = k_ref[pl.ds(row, c), :].astype(F32)
        v = v_ref[pl.ds(row, c), :]
        if rotate:
            cos, sin = cos_ref[pl.ds(row, c), :], sin_ref[pl.ds(row, c), :]
            q, k = rot(q, cos, sin), rot(k, cos, sin)
        q = q * scale
        s = s_s[...]
        o = None
        if want_out:
            o = _mm(q * qd_ref[d], s, 2, 2)
            for h in range(GROUP_HEADS):
                qm = jnp.where(head_k == h, q, 0.0)
                sc = _mm(qm, k, 2, 2, nt) * dm_ref[d, h]
                o = o + jnp.where(head_v == h, _mm(sc, v, 2, 1), 0.0)
        kv = _mm((k * kd_ref[d]).T, v, 2, 1)
        s_s[...] = s * cd_ref[d] + jnp.where(same_head, kv, 0.0)
        return o

    def finish(o, g_ref, row, o_ref):
        y = _head_rms_norm(o)
        g = g_ref[pl.ds(row, c), :].astype(F32)
        o_ref[pl.ds(row, c), :] = (g * jax.nn.sigmoid(g) * y).astype(o_ref.dtype)

    lat = (ql_ref, kl_ref, vl_ref)
    ctx = (qc_ref, kc_ref, vc_ref)
    ncl, ncc = nl // c, nc // c

    s_s[...] = jnp.zeros_like(s_s)
    for cc in range(ncc):
        o = step(0, ctx, cc * c, False, with_ctx_out)
        if with_ctx_out:
            accc_s[cc * c:(cc + 1) * c, :] = o

    def fwd(i, carry):
        row = pl.multiple_of(i * c, c)
        accl_s[pl.ds(row, c), :] = step(0, lat, row, True, True)
        return carry

    lax.fori_loop(0, ncl, fwd, 0)

    s_s[...] = jnp.zeros_like(s_s)
    for cc in reversed(range(ncc)):
        o = step(1, ctx, cc * c, False, with_ctx_out)
        if with_ctx_out:
            finish(accc_s[cc * c:(cc + 1) * c, :] + o, gc_ref, cc * c, oc_ref)
    if not with_ctx_out:
        oc_ref[...] = jnp.zeros_like(oc_ref)

    def bwd(t, carry):
        row = pl.multiple_of((ncl - 1 - t) * c, c)
        o = step(1, lat, row, True, True)
        finish(accl_s[pl.ds(row, c), :] + o, gl_ref, row, ol_ref)
        return carry

    lax.fori_loop(0, ncl, bwd, 0)


def retention(p_l, p_c, cos_l, sin_l, ret_decay, with_ctx_out):
    b, n, _ = p_l.shape
    lc = p_c.shape[1]
    gw = GROUP_W
    qd, kd, dm, cd = _ret_tables(ret_decay)
    c = RET_CHUNK
    lat = lambda cb: pl.BlockSpec((None, n, gw), lambda i: (i, 0, cb // 2))
    ctx = lambda cb: pl.BlockSpec((None, lc, gw), lambda i: (i, 0, cb // 2))
    full = lambda shape: pl.BlockSpec(shape, lambda i: (0,) * len(shape))
    y_l, y_c = pl.pallas_call(
        functools.partial(_ret_kernel, nl=n, nc=lc, with_ctx_out=with_ctx_out),
        grid=(b,),
        in_specs=[lat(CB_RET_Q), lat(CB_RET_K), lat(CB_RET_V), lat(CB_RET_G),
                  ctx(CB_RET_Q), ctx(CB_RET_K), ctx(CB_RET_V), ctx(CB_RET_G),
                  full((n, LANES)), full((n, LANES)),
                  full((2, c, gw)), full((2, c, gw)), full((2, GROUP_HEADS, c, c)), full((2, 1, gw))],
        out_specs=[pl.BlockSpec((None, n, gw), lambda i: (i, 0, 0)),
                   pl.BlockSpec((None, lc, gw), lambda i: (i, 0, 0))],
        out_shape=[jax.ShapeDtypeStruct((b, n, gw), BF16), jax.ShapeDtypeStruct((b, lc, gw), BF16)],
        scratch_shapes=[pltpu.VMEM((gw, gw), F32), pltpu.VMEM((n, gw), F32), pltpu.VMEM((lc, gw), F32)],
        compiler_params=_cparams("parallel"),
    )(p_l, p_l, p_l, p_l, p_c, p_c, p_c, p_c, cos_l, sin_l, qd, kd, dm, cd)
    return y_l, (y_c if with_ctx_out else None)


def _gdn_prep_kernel(x_ref, w_ref, o_ref):
    j = pl.program_id(1)
    x = x_ref[...].astype(F32)
    n = x.shape[0]
    row = lax.broadcasted_iota(jnp.int32, (n, 1), 0)
    xp = jnp.where(row == 0, 0.0, pltpu.roll(x, 1, 0))
    xn = jnp.where(row == n - 1, 0.0, pltpu.roll(x, n - 1, 0))
    y = xp * w_ref[0:1, :] + x * w_ref[1:2, :] + xn * w_ref[2:3, :]
    y = y * jax.nn.sigmoid(y)
    ss = jnp.dot(y * y, _head_mean_matrix() * HEAD_DIM, preferred_element_type=F32, precision=lax.Precision.HIGHEST)
    nrm = lax.rsqrt(ss + 1e-6) * jnp.where(j == 0, HEAD_DIM ** -0.5, 1.0)
    y = jnp.where(j < 2, y * nrm, y)
    o_ref[...] = y.astype(o_ref.dtype)


def gdn_prep(p, conv_w):
    b, n, _ = p.shape
    gw = GROUP_W
    return pl.pallas_call(
        _gdn_prep_kernel,
        grid=(b, 3),
        in_specs=[pl.BlockSpec((None, n, gw), lambda i, j: (i, 0, CB_GDN_QKV // 2 + j)),
                  pl.BlockSpec((SHORT_CONV, gw), lambda i, j: (0, j))],
        out_specs=pl.BlockSpec((None, None, n, gw), lambda i, j: (i, j, 0, 0)),
        out_shape=jax.ShapeDtypeStruct((b, 3, n, gw), BF16),
        compiler_params=_cparams("parallel", "parallel"),
    )(p, conv_w)


def _gdn_kernel(xl_ref, gl_ref, abl_ref, xc_ref, gc_ref, abc_ref, par_ref, ng_ref, ol_ref, oc_ref,
                s_s, accl_s, accc_s, *, nl, nc, with_ctx_out):
    c = GDN_CHUNK
    gh, gw = GROUP_HEADS, GROUP_W
    hp = lax.Precision.HIGHEST
    lane = lax.broadcasted_iota(jnp.int32, (1, gw), 1)
    head_v = lane // HEAD_DIM
    lane128 = lax.broadcasted_iota(jnp.int32, (1, LANES), 1)
    r_i = lax.broadcasted_iota(jnp.int32, (gw, 1), 0)
    c_i = lax.broadcasted_iota(jnp.int32, (1, gw), 1)
    same_head = (r_i // HEAD_DIM) == (c_i // HEAD_DIM)
    ti = lax.broadcasted_iota(jnp.int32, (c, 1), 0)
    tj = lax.broadcasted_iota(jnp.int32, (1, c), 1)
    nt = (((1,), (1,)), ((), ()))

    def stack(x):
        return jnp.concatenate([jnp.where(head_v == h, x, 0.0) for h in range(gh)], axis=0)

    def unstack(y):
        return y[0:c] + y[c:2 * c] + y[2 * c:3 * c] + y[3 * c:4 * c]

    def by_head(cols):
        out = cols[gh - 1]
        for h in reversed(range(gh - 1)):
            out = jnp.where(head_v == h, cols[h], out)
        return out

    def step(d, refs, row, want_out):
        x_ref, ab_ref = refs
        q = x_ref[0, pl.ds(row, c), :].astype(F32)
        k = x_ref[1, pl.ds(row, c), :].astype(F32)
        v = x_ref[2, pl.ds(row, c), :].astype(F32)
        ab = ab_ref[pl.ds(row, c), :]
        xa = ab + par_ref[1:2, :]
        la = -par_ref[0:1, :] * (jnp.maximum(xa, 0.0) + jnp.log(1.0 + jnp.exp(-jnp.abs(xa))))
        bt = jax.nn.sigmoid(ab)
        tri = jnp.where((ti >= tj) if d == 0 else (ti <= tj), 1.0, 0.0)
        g = jnp.dot(tri, la, preferred_element_type=F32, precision=hp)
        gcols = [jnp.sum(jnp.where(lane128 == d * gh + h, g, 0.0), axis=1, keepdims=True) for h in range(gh)]
        bcols = [jnp.sum(jnp.where(lane128 == 2 * gh + d * gh + h, bt, 0.0), axis=1, keepdims=True) for h in range(gh)]
        gexp, bexp = by_head(gcols), by_head(bcols)
        glast = gexp[c - 1:c, :] if d == 0 else gexp[0:1, :]
        gc_mat = jnp.concatenate([jnp.broadcast_to(gcols[h], (c, gw)) for h in range(gh)], axis=0)
        b_col = jnp.concatenate(bcols, axis=0)
        i_loc, j_loc = r_i % c, c_i % c
        incl = same_head & ((i_loc >= j_loc) if d == 0 else (i_loc <= j_loc))
        strict = same_head & ((i_loc > j_loc) if d == 0 else (i_loc < j_loc))
        decay = jnp.exp(jnp.where(incl, gc_mat - gc_mat.T, -jnp.inf))
        ks = stack(k).astype(BF16)
        kk = lax.dot_general(ks, ks, nt, preferred_element_type=F32)
        a_mat = jnp.where(strict, b_col * kk * decay, 0.0)
        y = jnp.concatenate([stack(bexp * v), stack(bexp * jnp.exp(gexp) * k)], axis=1)
        t_inv = jnp.where(r_i == c_i, 1.0, 0.0)
        sz = 1
        while sz < c:
            off = (i_loc // (2 * sz) == j_loc // (2 * sz)) & (i_loc // sz != j_loc // sz)
            tb16 = t_inv.astype(BF16)
            ta = jnp.dot(tb16, jnp.where(off, a_mat, 0.0).astype(BF16), preferred_element_type=F32)
            t_inv = t_inv - jnp.dot(ta.astype(BF16), tb16, preferred_element_type=F32)
            sz *= 2
        y = _mm(t_inv, y, 2, 2)
        u_base, w = unstack(y[:, :gw]), unstack(y[:, gw:])
        s = s_s[d]
        sb = s.astype(BF16)
        u = u_base - jnp.dot(w.astype(BF16), sb, preferred_element_type=F32)
        k_tail = k * jnp.exp(glast - gexp)
        pad = jnp.zeros((LANES - c, gw), F32)
        kt_t = jnp.concatenate([k_tail, pad], axis=0).T.astype(BF16)
        u_pad = jnp.concatenate([u, pad], axis=0).astype(BF16)
        s_s[d] = s * jnp.exp(glast) + jnp.where(same_head, jnp.dot(kt_t, u_pad, preferred_element_type=F32), 0.0)
        if not want_out:
            return None
        qs = stack(q).astype(BF16)
        qk = lax.dot_general(qs, ks, nt, preferred_element_type=F32) * decay
        o = jnp.dot((q * jnp.exp(gexp)).astype(BF16), sb, preferred_element_type=F32)
        return o + unstack(_mm(qk, stack(u), 2, 2))

    def finish(o, g_ref, row, o_ref):
        y = _head_rms_norm(o) * ng_ref[...]
        g = g_ref[pl.ds(row, c), :].astype(F32)
        o_ref[pl.ds(row, c), :] = (y * g * jax.nn.sigmoid(g)).astype(o_ref.dtype)

    lat = (xl_ref, abl_ref)
    ctx = (xc_ref, abc_ref)
    ncl, ncc = nl // c, nc // c

    s_s[...] = jnp.zeros_like(s_s)
    for t in range(ncc):
        rf, rb = t * c, (ncc - 1 - t) * c
        of = step(0, ctx, rf, with_ctx_out)
        ob = step(1, ctx, rb, with_ctx_out)
        if with_ctx_out:
            accc_s[0, rf:rf + c, :] = of
            accc_s[1, rb:rb + c, :] = ob
    if with_ctx_out:
        for t in range(ncc):
            finish(accc_s[0, t * c:(t + 1) * c, :] + accc_s[1, t * c:(t + 1) * c, :], gc_ref, t * c, oc_ref)
    else:
        oc_ref[...] = jnp.zeros_like(oc_ref)

    def both(t, carry):
        rf = pl.multiple_of(t * c, c)
        rb = pl.multiple_of((ncl - 1 - t) * c, c)
        accl_s[0, pl.ds(rf, c), :] = step(0, lat, rf, True)
        accl_s[1, pl.ds(rb, c), :] = step(1, lat, rb, True)
        return carry

    lax.fori_loop(0, ncl, both, 0)

    def fin(t, carry):
        row = pl.multiple_of(t * c, c)
        finish(accl_s[0, pl.ds(row, c), :] + accl_s[1, pl.ds(row, c), :], gl_ref, row, ol_ref)
        return carry

    lax.fori_loop(0, ncl, fin, 0)


def gdn(p_l, ab_l, p_c, ab_c, conv_w, a_log, dt_bias, norm_g, with_ctx_out):
    b, n, _ = p_l.shape
    lc = p_c.shape[1]
    gw = GROUP_W
    x_l, x_c = gdn_prep(p_l, conv_w), gdn_prep(p_c, conv_w)
    par = jnp.zeros((8, LANES), F32)
    par = par.at[0, :2 * GROUP_HEADS].set(jnp.exp(a_log.astype(F32)).reshape(-1))
    par = par.at[1, :2 * GROUP_HEADS].set(dt_bias.astype(F32).reshape(-1))
    ng = jnp.tile(norm_g, GROUP_HEADS)[None, :]
    full = lambda shape: pl.BlockSpec(shape, lambda i: (0,) * len(shape))
    y_l, y_c = pl.pallas_call(
        functools.partial(_gdn_kernel, nl=n, nc=lc, with_ctx_out=with_ctx_out),
        grid=(b,),
        in_specs=[pl.BlockSpec((None, 3, n, gw), lambda i: (i, 0, 0, 0)),
                  pl.BlockSpec((None, n, gw), lambda i: (i, 0, CB_GDN_G // 2)),
                  pl.BlockSpec((None, n, LANES), lambda i: (i, 0, 0)),
                  pl.BlockSpec((None, 3, lc, gw), lambda i: (i, 0, 0, 0)),
                  pl.BlockSpec((None, lc, gw), lambda i: (i, 0, CB_GDN_G // 2)),
                  pl.BlockSpec((None, lc, LANES), lambda i: (i, 0, 0)),
                  full((8, LANES)), full((1, gw))],
        out_specs=[pl.BlockSpec((None, n, gw), lambda i: (i, 0, 0)),
                   pl.BlockSpec((None, lc, gw), lambda i: (i, 0, 0))],
        out_shape=[jax.ShapeDtypeStruct((b, n, gw), BF16), jax.ShapeDtypeStruct((b, lc, gw), BF16)],
        scratch_shapes=[pltpu.VMEM((2, gw, gw), F32), pltpu.VMEM((2, n, gw), F32), pltpu.VMEM((2, lc, gw), F32)],
        compiler_params=_cparams("parallel"),
    )(x_l, p_l, ab_l, x_c, p_c, ab_c, par, ng)
    return y_l, (y_c if with_ctx_out else None)


def _layer_norm(x, g, b):
    mu = jnp.mean(x, -1, keepdims=True)
    xc = x - mu
    var = jnp.mean(xc * xc, -1, keepdims=True)
    return xc * lax.rsqrt(var + LN_EPS) * g + b


def _outproj_kernel(y_ref, h_ref, mod_ref, w_ref, ln_ref, rw_ref, hn_ref, u2_ref, aff_ref, *, alpha):
    mix = jnp.dot(y_ref[...], w_ref[...], preferred_element_type=F32)
    hn = _layer_norm(alpha * h_ref[...] + mod_ref[2:3, :] * mix, ln_ref[0:1, :], ln_ref[1:2, :])
    hn_ref[...] = hn
    u2 = hn * (1.0 + mod_ref[4:5, :]) + mod_ref[3:4, :]
    u2_ref[...] = u2.astype(BF16)
    logits = _mm(rw_ref[...], u2, 2, 2, (((1,), (1,)), ((), ())))
    m = jnp.max(logits, 0, keepdims=True)
    e = jnp.exp(logits - m)
    aff_ref[...] = e / jnp.sum(e, 0, keepdims=True)


def outproj(y, h, mod, w_bf16, ln, rw_t_bf16, alpha, tm):
    b, n, d = h.shape
    mixw = y.shape[-1]
    ne = rw_t_bf16.shape[0]
    return pl.pallas_call(
        functools.partial(_outproj_kernel, alpha=alpha),
        grid=(b, n // tm),
        in_specs=[
            pl.BlockSpec((None, tm, mixw), lambda i, j: (i, j, 0)),
            pl.BlockSpec((None, tm, d), lambda i, j: (i, j, 0)),
            pl.BlockSpec((None, 6, d), lambda i, j: (i, 0, 0)),
            pl.BlockSpec((mixw, d), lambda i, j: (0, 0)),
            pl.BlockSpec((2, d), lambda i, j: (0, 0)),
            pl.BlockSpec((ne, d), lambda i, j: (0, 0)),
        ],
        out_specs=[
            pl.BlockSpec((None, tm, d), lambda i, j: (i, j, 0)),
            pl.BlockSpec((None, tm, d), lambda i, j: (i, j, 0)),
            pl.BlockSpec((None, ne, tm), lambda i, j: (i, 0, j)),
        ],
        out_shape=[
            jax.ShapeDtypeStruct((b, n, d), F32),
            jax.ShapeDtypeStruct((b, n, d), BF16),
            jax.ShapeDtypeStruct((b, ne, n), F32),
        ],
        compiler_params=_cparams("parallel", "parallel"),
    )(y, h, mod, w_bf16, ln, rw_t_bf16)


def _ffn_kernel(x_ref, wg_ref, wu_ref, wd_ref, o_ref, acc_s, *, rows):
    f = pl.program_id(2)
    tb, cap, d = x_ref.shape

    @pl.when(f == 0)
    def _():
        acc_s[...] = jnp.zeros_like(acc_s)

    wg = wg_ref[...].astype(BF16)
    wu = wu_ref[...].astype(BF16)
    wd = wd_ref[...].astype(BF16)
    for bi in range(tb):
        for r in range(cap // rows):
            x = x_ref[bi, r * rows:(r + 1) * rows, :]
            g = jnp.dot(x, wg, preferred_element_type=F32)
            u = jnp.dot(x, wu, preferred_element_type=F32)
            hid = (g * jax.nn.sigmoid(g) * u).astype(BF16)
            acc_s[bi, r * rows:(r + 1) * rows, :] += jnp.dot(hid, wd, preferred_element_type=F32)

    @pl.when(f == pl.num_programs(2) - 1)
    def _():
        o_ref[...] = acc_s[...].astype(o_ref.dtype)


def expert_ffn(xin, w_gate, w_up, w_down, layer, tb, tf, rows):
    b, ne, cap, d = xin.shape
    ff = w_gate.shape[-1]
    return pl.pallas_call(
        functools.partial(_ffn_kernel, rows=rows),
        grid=(ne, b // tb, ff // tf),
        in_specs=[
            pl.BlockSpec((tb, None, cap, d), lambda e, i, f: (i, e, 0, 0)),
            pl.BlockSpec((None, None, d, tf), lambda e, i, f: (layer, e, 0, f)),
            pl.BlockSpec((None, None, d, tf), lambda e, i, f: (layer, e, 0, f)),
            pl.BlockSpec((None, None, tf, d), lambda e, i, f: (layer, e, f, 0)),
        ],
        out_specs=pl.BlockSpec((tb, None, cap, d), lambda e, i, f: (i, e, 0, 0)),
        out_shape=jax.ShapeDtypeStruct((b, ne, cap, d), BF16),
        scratch_shapes=[pltpu.VMEM((tb, cap, d), F32)],
        compiler_params=_cparams("parallel", "parallel", "arbitrary"),
    )(xin, w_gate, w_up, w_down)


def _route_kernel(aff_ref, pos_ref, *, cap):
    aff = aff_ref[...]
    ne, n = aff.shape
    bits = pltpu.bitcast(aff, jnp.int32)

    def count(mask):
        return jnp.sum(jnp.where(mask, 1.0, 0.0), axis=1, keepdims=True)

    def search(i, t):
        cand = t | (jnp.int32(1) << (30 - i))
        return jnp.where(count(bits >= cand) >= cap, cand, t)

    thr = lax.fori_loop(0, 31, search, jnp.zeros((ne, 1), jnp.int32))
    gt = bits > thr
    eq = bits == thr
    need = cap - count(gt)

    tile = 256 if n % 256 == 0 else n
    r = lax.broadcasted_iota(jnp.int32, (tile, tile), 0)
    cc = lax.broadcasted_iota(jnp.int32, (tile, tile), 1)
    upper = jnp.where(r <= cc, 1.0, 0.0).astype(BF16)

    def prefix_excl(m):
        outs, carry = [], jnp.zeros((ne, 1), F32)
        for j in range(n // tile):
            mt = m[:, j * tile:(j + 1) * tile]
            inc = jnp.dot(mt.astype(BF16), upper, preferred_element_type=F32) + carry
            outs.append(inc - mt)
            carry = inc[:, tile - 1:tile]
        return jnp.concatenate(outs, axis=1)

    eqf = jnp.where(eq, 1.0, 0.0)
    sel = gt | (eq & (prefix_excl(eqf) < need))
    pos_ref[...] = jnp.where(sel, prefix_excl(jnp.where(sel, 1.0, 0.0)), -1.0)


def route(aff_t, cap):
    b, ne, n = aff_t.shape
    return pl.pallas_call(
        functools.partial(_route_kernel, cap=cap),
        grid=(b,),
        in_specs=[pl.BlockSpec((None, ne, n), lambda i: (i, 0, 0))],
        out_specs=pl.BlockSpec((None, ne, n), lambda i: (i, 0, 0)),
        out_shape=jax.ShapeDtypeStruct((b, ne, n), F32),
        compiler_params=_cparams("parallel"),
    )(aff_t)


def _gather_kernel(pos_ref, u_ref, x_ref):
    cap = x_ref.shape[0]
    slot = lax.broadcasted_iota(jnp.int32, (cap, 1), 0).astype(F32)
    onehot = jnp.where(pos_ref[...] == slot, 1.0, 0.0).astype(BF16)
    x_ref[...] = jnp.dot(onehot, u_ref[...], preferred_element_type=F32).astype(x_ref.dtype)


def gather_tokens(pos, u2, cap):
    b, ne, n = pos.shape
    d = u2.shape[-1]
    return pl.pallas_call(
        _gather_kernel,
        grid=(b, ne),
        in_specs=[pl.BlockSpec((None, None, 1, n), lambda i, e: (i, e, 0, 0)),
                  pl.BlockSpec((None, n, d), lambda i, e: (i, 0, 0))],
        out_specs=pl.BlockSpec((None, None, cap, d), lambda i, e: (i, e, 0, 0)),
        out_shape=jax.ShapeDtypeStruct((b, ne, cap, d), BF16),
        compiler_params=_cparams("parallel", "arbitrary"),
    )(pos.reshape(b, ne, 1, n), u2)


def _scatter_kernel(post_ref, afft_ref, y_ref, f_ref):
    e = pl.program_id(2)
    tn, ne = post_ref.shape
    cap = y_ref.shape[0]

    @pl.when(e == 0)
    def _():
        f_ref[...] = jnp.zeros_like(f_ref)

    lane_e = lax.broadcasted_iota(jnp.int32, (1, ne), 1)
    pos_col = jnp.sum(jnp.where(lane_e == e, post_ref[...], 0.0), axis=1, keepdims=True)
    w_col = jnp.sum(jnp.where(lane_e == e, afft_ref[...], 0.0), axis=1, keepdims=True)
    slot = lax.broadcasted_iota(jnp.int32, (1, cap), 1).astype(F32)
    onehot = jnp.where(pos_col == slot, 1.0, 0.0).astype(BF16)
    f_ref[...] += w_col * jnp.dot(onehot, y_ref[...], preferred_element_type=F32)


def scatter_tokens(pos_t, aff_tm, y, tn):
    b, n, ne = pos_t.shape
    cap, d = y.shape[2], y.shape[3]
    return pl.pallas_call(
        _scatter_kernel,
        grid=(b, n // tn, ne),
        in_specs=[pl.BlockSpec((None, tn, ne), lambda i, j, e: (i, j, 0)),
                  pl.BlockSpec((None, tn, ne), lambda i, j, e: (i, j, 0)),
                  pl.BlockSpec((None, None, cap, d), lambda i, j, e: (i, e, 0, 0))],
        out_specs=pl.BlockSpec((None, tn, d), lambda i, j, e: (i, j, 0)),
        out_shape=jax.ShapeDtypeStruct((b, n, d), F32),
        compiler_params=_cparams("parallel", "parallel", "arbitrary"),
    )(pos_t, aff_tm, y)


def _ln2_kernel(h_ref, f_ref, mod_ref, ln_ref, o_ref, *, alpha):
    o_ref[...] = _layer_norm(alpha * h_ref[...] + mod_ref[5:6, :] * f_ref[...], ln_ref[0:1, :], ln_ref[1:2, :])


def ffn_residual_norm(h, f, mod, ln, alpha, tm):
    b, n, d = h.shape
    return pl.pallas_call(
        functools.partial(_ln2_kernel, alpha=alpha),
        grid=(b, n // tm),
        in_specs=[pl.BlockSpec((None, tm, d), lambda i, j: (i, j, 0)),
                  pl.BlockSpec((None, tm, d), lambda i, j: (i, j, 0)),
                  pl.BlockSpec((None, 6, d), lambda i, j: (i, 0, 0)),
                  pl.BlockSpec((2, d), lambda i, j: (0, 0))],
        out_specs=pl.BlockSpec((None, tm, d), lambda i, j: (i, j, 0)),
        out_shape=jax.ShapeDtypeStruct((b, n, d), F32),
        compiler_params=_cparams("parallel", "parallel"),
    )(h, f, mod, ln)


def expert_choice_ffn(u2, aff_t, w_gate, w_up, w_down, layer):
    b, n, d = u2.shape
    ne = aff_t.shape[1]
    cap = CAPACITY_FACTOR * n // ne
    pos = route(aff_t, cap)
    xin = gather_tokens(pos, u2, cap)
    tb = math.gcd(b, 4) if cap >= 512 else b
    y = expert_ffn(xin, w_gate, w_up, w_down, layer, tb=tb, tf=512, rows=min(cap, 256))
    return scatter_tokens(jnp.swapaxes(pos, 1, 2), jnp.swapaxes(aff_t, 1, 2), y, tn=min(n, 2048))


def layer_norm(x, g, b):
    mu = jnp.mean(x, -1, keepdims=True)
    var = jnp.mean(jnp.square(x - mu), -1, keepdims=True)
    return (x - mu) * lax.rsqrt(var + LN_EPS) * g + b


def rms_norm(x, g=None, eps=RMS_EPS):
    y = x * lax.rsqrt(jnp.mean(jnp.square(x), -1, keepdims=True) + eps)
    if g is not None:
        y = y * g
    return y


def l2_normalize(x, eps=1e-6):
    return x * lax.rsqrt(jnp.sum(jnp.square(x), -1, keepdims=True) + eps)


def axial_rope_angles(n, rot_dim):
    rows = n // GRID_W
    row = jnp.repeat(jnp.arange(rows, dtype=F32), GRID_W)
    col = jnp.tile(jnp.arange(GRID_W, dtype=F32), rows)
    n_freq = rot_dim // 4
    inv = ROPE_BASE ** (-jnp.arange(n_freq, dtype=F32) / n_freq)
    return jnp.concatenate([row[:, None] * inv, col[:, None] * inv], -1)


def retention_angles(n, dim):
    theta = 1.0 / (ROPE_BASE ** jnp.linspace(0.0, 1.0, dim // 2, dtype=F32))
    return jnp.arange(n, dtype=F32)[:, None] * theta


def rotate(x, ang):
    half = x.shape[-1] // 2
    cos, sin = jnp.cos(ang).astype(x.dtype), jnp.sin(ang).astype(x.dtype)
    x1, x2 = x[..., :half], x[..., half:]
    return jnp.concatenate([x1 * cos - x2 * sin, x1 * sin + x2 * cos], -1)


def flip_seq(t):
    return jnp.flip(t, axis=2)


def retention_scan(q, k, v, log_g, s0, with_out):
    b, h, n, dk = q.shape
    dv = v.shape[-1]
    c = RET_CHUNK
    nc = n // c
    qc, kc, vc = (t.reshape(b, h, nc, c, t.shape[-1]) for t in (q, k, v))
    pos = jnp.arange(c, dtype=F32)
    lg = log_g[:, None]
    k_decay = jnp.exp(lg * (c - 1 - pos))
    kv = jnp.einsum('bhncd,hc,bhnce->nbhde', kc, k_decay, vc)
    chunk_decay = jnp.exp(log_g * c)[None, :, None, None]

    def step(s, kv_n):
        return s * chunk_decay + kv_n, (s if with_out else None)

    s_last, s_prev = lax.scan(step, s0, kv)
    if not with_out:
        return None, s_last
    rel = pos[:, None] - pos[None, :]
    d_mat = jnp.where(rel >= 0, jnp.exp(lg[:, :, None] * jnp.maximum(rel, 0.0)), 0.0)
    q_decay = jnp.exp(lg * (pos + 1.0))
    scores = jnp.einsum('bhnid,bhnjd->bhnij', qc, kc) * d_mat[:, None]
    o = (jnp.einsum('bhnij,bhnje->bhnie', scores, vc)
         + jnp.einsum('bhnid,hi,nbhde->bhnie', qc, q_decay, s_prev))
    return o.reshape(b, h, n, dv), s_last


def retention_group(p_l, p_c, ret_decay, with_ctx_out, gh):
    def heads(t):
        return t.reshape(t.shape[0], t.shape[1], gh, HEAD_DIM).transpose(0, 2, 1, 3)
    scale = HEAD_DIM ** -0.5
    ang = retention_angles(p_l['ret_q'].shape[1], HEAD_DIM)
    ql = rotate(heads(p_l['ret_q']) * scale, ang)
    kl = rotate(heads(p_l['ret_k']), ang)
    vl = heads(p_l['ret_v'])
    qc, kc, vc = heads(p_c['ret_q']) * scale, heads(p_c['ret_k']), heads(p_c['ret_v'])
    log_g = jax.nn.log_sigmoid(ret_decay)
    s0 = jnp.zeros((ql.shape[0], gh, HEAD_DIM, HEAD_DIM), F32)
    o_l, o_c = 0.0, 0.0
    for dr in range(2):
        f = flip_seq if dr else (lambda t: t)
        oc, s_ctx = retention_scan(f(qc), f(kc), f(vc), log_g[dr], s0, with_ctx_out)
        ol, _ = retention_scan(f(ql), f(kl), f(vl), log_g[dr], s_ctx, True)
        o_l = o_l + f(ol)
        if with_ctx_out:
            o_c = o_c + f(oc)

    def finish(o, gate):
        y = rms_norm(o).transpose(0, 2, 1, 3).reshape(gate.shape)
        return jax.nn.silu(gate) * y

    return finish(o_l, p_l['ret_g']), (finish(o_c, p_c['ret_g']) if with_ctx_out else None)


def short_conv(x, w):
    return lax.conv_general_dilated(
        x, w[:, None, :], window_strides=(1,),
        padding=[(SHORT_CONV // 2, SHORT_CONV // 2)],
        dimension_numbers=('NWC', 'WIO', 'NWC'), feature_group_count=x.shape[-1])


def gdn_scan(q, k, v, log_a, beta, s0, with_out):
    b, h, n, dk = q.shape
    dv = v.shape[-1]
    c = GDN_CHUNK
    nc = n // c
    q, k, v = (t.reshape(b, h, nc, c, t.shape[-1]) for t in (q, k, v))
    log_a, beta = (t.reshape(b, h, nc, c) for t in (log_a, beta))
    g = jnp.cumsum(log_a, -1)
    idx = jnp.arange(c)
    incl = idx[:, None] >= idx[None, :]
    strict = idx[:, None] > idx[None, :]
    decay = jnp.exp(jnp.where(incl, g[..., :, None] - g[..., None, :], -jnp.inf))
    a_mat = jnp.where(strict, beta[..., :, None] * jnp.einsum('bhnid,bhnjd->bhnij', k, k) * decay, 0.0)
    rhs = jnp.concatenate([beta[..., None] * v, (beta * jnp.exp(g))[..., None] * k], -1)
    sol = lax.linalg.triangular_solve(a_mat + jnp.eye(c, dtype=a_mat.dtype), rhs, left_side=True, lower=True)
    u_base, w = sol[..., :dv], sol[..., dv:]
    k_tail = k * jnp.exp(g[..., -1:] - g)[..., None]
    chunk_decay = jnp.exp(g[..., -1])
    mv = lambda t: jnp.moveaxis(t, 2, 0)
    xs = (mv(u_base), mv(w), mv(k_tail), mv(chunk_decay))
    if with_out:
        qk = jnp.einsum('bhnid,bhnjd->bhnij', q, k) * decay
        xs = xs + (mv(qk), mv(q * jnp.exp(g)[..., None]))

    def step(s, xc):
        u = xc[0] - jnp.einsum('bhck,bhkv->bhcv', xc[1], s)
        s_new = s * xc[3][..., None, None] + jnp.einsum('bhck,bhcv->bhkv', xc[2], u)
        if not with_out:
            return s_new, None
        o = jnp.einsum('bhck,bhkv->bhcv', xc[5], s) + jnp.einsum('bhcj,bhjv->bhcv', xc[4], u)
        return s_new, o

    s_last, o = lax.scan(step, s0, xs)
    if not with_out:
        return None, s_last
    return jnp.moveaxis(o, 0, 2).reshape(b, h, n, dv), s_last


def gdn_group(p_l, p_c, conv_w, a_log, dt_bias, norm_g, with_ctx_out, gh):
    def prep(p):
        b, n, _ = p['gdn_qkv'].shape
        qkv = jax.nn.silu(short_conv(p['gdn_qkv'], conv_w))
        qkv = qkv.reshape(b, n, 3, gh, HEAD_DIM).transpose(2, 0, 3, 1, 4)
        q = l2_normalize(qkv[0]) * HEAD_DIM ** -0.5
        k = l2_normalize(qkv[1])
        a = p['gdn_a'].reshape(b, n, 2, gh).transpose(2, 0, 3, 1)
        log_a = -jnp.exp(a_log)[:, None, :, None] * jax.nn.softplus(a + dt_bias[:, None, :, None])
        beta = jax.nn.sigmoid(p['gdn_b'].reshape(b, n, 2, gh).transpose(2, 0, 3, 1))
        return q, k, qkv[2], log_a, beta

    ql, kl, vl, la_l, bt_l = prep(p_l)
    qc, kc, vc, la_c, bt_c = prep(p_c)
    s0 = jnp.zeros((ql.shape[0], gh, HEAD_DIM, HEAD_DIM), F32)
    o_l, o_c = 0.0, 0.0
    for dr in range(2):
        f = flip_seq if dr else (lambda t: t)
        oc, s_ctx = gdn_scan(f(qc), f(kc), f(vc), f(la_c[dr]), f(bt_c[dr]), s0, with_ctx_out)
        ol, _ = gdn_scan(f(ql), f(kl), f(vl), f(la_l[dr]), f(bt_l[dr]), s_ctx, True)
        o_l = o_l + f(ol)
        if with_ctx_out:
            o_c = o_c + f(oc)

    def finish(o, gate):
        o = o.transpose(0, 2, 1, 3)
        gt = gate.reshape(o.shape)
        return (rms_norm(o, norm_g) * jax.nn.silu(gt)).reshape(gate.shape)

    return finish(o_l, p_l['gdn_g']), (finish(o_c, p_c['gdn_g']) if with_ctx_out else None)


def _jax_expert_choice_ffn(u2, aff_t, w_gate, w_up, w_down):
    b, n, d = u2.shape
    ne = aff_t.shape[1]
    cap = CAPACITY_FACTOR * n // ne
    weight, idx = lax.top_k(aff_t, cap)
    xin = jax.vmap(lambda ub, ib: ub[ib])(u2, idx)
    tb = 4 if cap >= 512 else b
    y = expert_ffn(xin, w_gate, w_up, w_down, tb=tb, tf=512, rows=min(cap, 256))
    y = y * weight[..., None]
    flat = (jnp.arange(b)[:, None, None] * n + idx).reshape(-1)
    return jnp.zeros((b * n, d), F32).at[flat].add(y.reshape(-1, d)).reshape(b, n, d)


def _jax_views(p, ab, inv_ret):
    blk = lambda cb, w: p[..., cb * LANES:cb * LANES + w].astype(F32)
    gh = GROUP_HEADS
    return {'ret_q': blk(CB_RET_Q, GROUP_W)[..., inv_ret], 'ret_k': blk(CB_RET_K, GROUP_W)[..., inv_ret],
            'ret_v': blk(CB_RET_V, GROUP_W), 'ret_g': blk(CB_RET_G, GROUP_W),
            'gdn_qkv': blk(CB_GDN_QKV, 3 * GROUP_W), 'gdn_g': blk(CB_GDN_G, GROUP_W),
            'gdn_a': ab[..., :2 * gh], 'gdn_b': ab[..., 2 * gh:4 * gh]}


def kernel(x, c, ctx, c_ctx, ada_w, ada_b, w_in, w_out, ret_decay, diff_lambda, diff_norm, gdn_conv,
           gdn_a_log, gdn_dt_bias, gdn_norm, swa_sink, ln_g, ln_b, router_w, w_gate, w_up, w_down):
    b, n, d = x.shape
    lc = ctx.shape[1]
    depth = ada_w.shape[0]
    gh = GROUP_HEADS
    alpha = (2 * depth) ** 0.25
    main_perm, tail_perm, off = _in_col_perm()
    inv_ret = np.argsort(main_perm[:GROUP_W] - off['ret_q'])
    ang_diff = axial_rope_angles(n, HEAD_DIM // 2)
    ang_swa = axial_rope_angles(n, HEAD_DIM)
    cos_diff, sin_diff = jnp.tile(jnp.cos(ang_diff), (1, 8)), jnp.tile(jnp.sin(ang_diff), (1, 8))
    cos_swa = jnp.tile(jnp.cos(ang_swa), (1, 4))
    sin_swa = jnp.tile(jnp.sin(ang_swa), (1, 4)) * jnp.where(jnp.arange(LANES) < LANES // 2, -1.0, 1.0)
    ang_ret = retention_angles(n, HEAD_DIM)
    cos_ret, sin_ret = jnp.tile(jnp.cos(ang_ret), (1, 4)), jnp.tile(jnp.sin(ang_ret), (1, 4))
    cond_l = jax.nn.silu(c)
    cond_c = jax.nn.silu(c_ctx)
    h, hc = x, ctx
    for layer in range(depth):
        full_ctx = layer < depth - 1
        mod_l = (cond_l @ ada_w[layer] + ada_b[layer]).reshape(b, 6, d)
        mod_c = jnp.broadcast_to((cond_c @ ada_w[layer] + ada_b[layer]).reshape(1, 6, d), (b, 6, d))
        wl = w_in[layer]
        w_in_b = jnp.concatenate([wl[:, main_perm], wl[:, tail_perm],
                                  jnp.zeros((d, LANES - tail_perm.size), F32)], axis=1).astype(BF16)
        w_out_b = w_out[layer].astype(BF16)
        rw_t = router_w[layer].T
        ln1 = jnp.stack([ln_g[layer, 0], ln_b[layer, 0]])
        ln2 = jnp.stack([ln_g[layer, 1], ln_b[layer, 1]])
        p_l, ab_l = inproj(h, mod_l, w_in_b, tm=512)
        p_c, ab_c = inproj(hc, mod_c, w_in_b, tm=lc)
        ya_l, ya_c = retention(p_l, p_c, cos_ret, sin_ret, ret_decay[layer], full_ctx)
        yb_l, yb_c = diff_attention(p_l, p_c, cos_diff, sin_diff, diff_lambda[layer], diff_norm[layer],
                                    layer, full_ctx, tq=256)
        yc_l, yc_c = gdn(p_l, ab_l, p_c, ab_c, gdn_conv[layer], gdn_a_log[layer], gdn_dt_bias[layer],
                         gdn_norm[layer], full_ctx)
        yd_l, yd_c = swa_attention(p_l, p_c, cos_swa, sin_swa, swa_sink[layer], full_ctx, tq=256)
        y_l = jnp.concatenate([ya_l, yb_l, yc_l, yd_l], -1)
        h, u2, aff = outproj(y_l, h, mod_l, w_out_b, ln1, rw_t, alpha, tm=512)
        f_l = expert_choice_ffn(u2, aff, w_gate, w_up, w_down, layer)
        h = ffn_residual_norm(h, f_l, mod_l, ln2, alpha, tm=512)
        if full_ctx:
            y_c = jnp.concatenate([ya_c, yb_c, yc_c, yd_c], -1)
            hc, u2c, affc = outproj(y_c, hc, mod_c, w_out_b, ln1, rw_t, alpha, tm=lc)
            f_c = expert_choice_ffn(u2c, affc, w_gate, w_up, w_down, layer)
            hc = ffn_residual_norm(hc, f_c, mod_c, ln2, alpha, tm=lc)
    return h
```

```python
import functools
import math

import numpy as np
import jax
import jax.numpy as jnp
from jax import lax
from jax.experimental import pallas as pl
from jax.experimental.pallas import tpu as pltpu

F32 = jnp.float32
BF16 = jnp.bfloat16

GRID_W = 64
HEAD_DIM = 64
N_MIXERS = 4
GROUP_HEADS = 4
GROUP_W = GROUP_HEADS * HEAD_DIM
ROPE_BASE = 10000.0
RET_CHUNK = 128
GDN_CHUNK = 64
SHORT_CONV = 3
WINDOW = 128
N_EXPERTS = 16
CAPACITY_FACTOR = 2
LN_EPS = 1e-5
RMS_EPS = 1e-6
LANES = 128

VMEM_LIMIT_BYTES = 56 * 1024 * 1024

CB_RET_Q, CB_RET_K, CB_RET_V, CB_RET_G = 0, 2, 4, 6
CB_DIFF_Q, CB_DIFF_K, CB_DIFF_V = 8, 10, 12
CB_GDN_QKV, CB_GDN_G = 14, 20
CB_SWA_Q, CB_SWA_K, CB_SWA_V = 22, 24, 25
MAIN_W = 26 * LANES


def _cparams(*sem):
    return pltpu.CompilerParams(dimension_semantics=sem, vmem_limit_bytes=VMEM_LIMIT_BYTES)


def _split(x, n):
    if x.dtype == BF16:
        return [x]
    terms, r = [], x
    for _ in range(n):
        t = r.astype(BF16)
        terms.append(t)
        r = r - t.astype(F32)
    return terms


def _mm(a, b, na=1, nb=1, dims=None):
    at, bt = _split(a, na), _split(b, nb)
    out = None
    for i, x in enumerate(at):
        for j, y in enumerate(bt):
            if i + j >= max(len(at), len(bt)):
                continue
            t = (jnp.dot(x, y, preferred_element_type=F32) if dims is None
                 else lax.dot_general(x, y, dims, preferred_element_type=F32))
            out = t if out is None else out + t
    return out


def _in_col_perm():
    gw, gh, hd = GROUP_W, GROUP_HEADS, HEAD_DIM
    off = {}
    o = 0
    for name, w in (('ret_q', gw), ('ret_k', gw), ('ret_v', gw), ('ret_g', gw), ('diff_q', gw), ('diff_k', gw),
                    ('diff_v', gw), ('gdn_qkv', 3 * gw), ('gdn_g', gw), ('gdn_a', 2 * gh), ('gdn_b', 2 * gh),
                    ('swa_q', gw), ('swa_k', gw // 2), ('swa_v', gw // 2)):
        off[name] = o
        o += w
    ident = lambda name, w: off[name] + np.arange(w)
    ret = np.zeros(gw, np.int64)
    for h in range(gh):
        for j in range(hd):
            ret[(j // 32) * 128 + h * 32 + j % 32] = h * hd + j
    dif = np.zeros(gw, np.int64)
    for h in range(gh):
        for c in range(2):
            for j in range(32):
                dif[(j // 16) * 128 + (h * 2 + c) * 16 + j % 16] = h * hd + c * 32 + j
    swq = np.zeros(gw, np.int64)
    for hk in range(2):
        for g in range(2):
            for j in range(hd):
                swq[g * 128 + (j // 32) * 64 + hk * 32 + j % 32] = hk * 128 + g * hd + j
    swk = np.zeros(gw // 2, np.int64)
    for hk in range(2):
        for j in range(hd):
            swk[(j // 32) * 64 + hk * 32 + j % 32] = hk * hd + j
    main = np.concatenate([
        off['ret_q'] + ret, off['ret_k'] + ret, ident('ret_v', gw), ident('ret_g', gw),
        off['diff_q'] + dif, off['diff_k'] + dif, ident('diff_v', gw),
        ident('gdn_qkv', 3 * gw), ident('gdn_g', gw),
        off['swa_q'] + swq, off['swa_k'] + swk, ident('swa_v', gw // 2)])
    tail = np.concatenate([ident('gdn_a', 2 * gh), ident('gdn_b', 2 * gh)])
    return main, tail, off


def _inproj_kernel(h_ref, mod_ref, w_ref, o_ref, ab_ref, *, tn):
    u = (h_ref[...] * (1.0 + mod_ref[1:2, :]) + mod_ref[0:1, :]).astype(BF16)
    nmain = o_ref.shape[-1]
    for j in range(nmain // tn):
        o_ref[:, j * tn:(j + 1) * tn] = jnp.dot(
            u, w_ref[:, j * tn:(j + 1) * tn], preferred_element_type=F32).astype(BF16)
    ab_ref[...] = jnp.dot(u, w_ref[:, nmain:], preferred_element_type=F32)


def inproj(h, mod, w_bf16, tm):
    b, n, d = h.shape
    nw = w_bf16.shape[1]
    return pl.pallas_call(
        functools.partial(_inproj_kernel, tn=256),
        grid=(b, n // tm),
        in_specs=[
            pl.BlockSpec((None, tm, d), lambda i, j: (i, j, 0)),
            pl.BlockSpec((None, 6, d), lambda i, j: (i, 0, 0)),
            pl.BlockSpec((d, nw), lambda i, j: (0, 0)),
        ],
        out_specs=[pl.BlockSpec((None, tm, MAIN_W), lambda i, j: (i, j, 0)),
                   pl.BlockSpec((None, tm, LANES), lambda i, j: (i, j, 0))],
        out_shape=[jax.ShapeDtypeStruct((b, n, MAIN_W), BF16),
                   jax.ShapeDtypeStruct((b, n, LANES), F32)],
        compiler_params=_cparams("parallel", "parallel"),
    )(h, mod, w_bf16)


def _head_mean_matrix():
    r = lax.broadcasted_iota(jnp.int32, (GROUP_W, GROUP_W), 0) // HEAD_DIM
    c = lax.broadcasted_iota(jnp.int32, (GROUP_W, GROUP_W), 1) // HEAD_DIM
    return jnp.where(r == c, 1.0 / HEAD_DIM, 0.0).astype(F32)


def _head_rms_norm(o):
    ms = jnp.dot(o * o, _head_mean_matrix(), preferred_element_type=F32, precision=lax.Precision.HIGHEST)
    return o * lax.rsqrt(ms + RMS_EPS)


def _diff_kernel(*refs, latent, lam_init, scale, nl, nc):
    if latent:
        (q_ref, kl_ref, kc_ref, vl_ref, vc_ref, cq_ref, sq_ref, ck_ref, sk_ref, lp_ref, ng_ref,
         o_ref, kt_s, acc_s) = refs
    else:
        q_ref, kc_ref, vc_ref, lp_ref, ng_ref, o_ref, kt_s, acc_s = refs
    half = GROUP_W // 2
    lane = lax.broadcasted_iota(jnp.int32, (1, GROUP_W), 1)

    def rot(x, cos, sin):
        x1, x2 = x[:, :half], x[:, half:]
        return jnp.concatenate([x1 * cos - x2 * sin, x1 * sin + x2 * cos], axis=1)

    def init_keys():
        if latent:
            ck = 512 if nl % 512 == 0 else nl
            for c0 in range(0, nl, ck):
                kr = rot(kl_ref[c0:c0 + ck, :].astype(F32), ck_ref[c0:c0 + ck, :], sk_ref[c0:c0 + ck, :])
                kt_s[:, c0:c0 + ck] = kr.T.astype(BF16)
        off = nl if latent else 0
        kt_s[:, off:off + nc] = kc_ref[...].astype(F32).T.astype(BF16)

    if latent:
        pl.when(pl.program_id(1) == 0)(init_keys)
    else:
        init_keys()

    lp = lp_ref[...]
    lam = (jnp.exp(jnp.sum(lp[0:1] * lp[1:2], axis=1, keepdims=True))
           - jnp.exp(jnp.sum(lp[2:3] * lp[3:4], axis=1, keepdims=True)) + lam_init)
    q = q_ref[...].astype(F32)
    if latent:
        q = rot(q, cq_ref[...], sq_ref[...])
    q = q * (scale * math.log2(math.e))
    acc_s[...] = jnp.zeros_like(acc_s)

    def probs(hc):
        qm = jnp.where((lane % half) // 16 == hc, q, 0.0).astype(BF16)
        s = jnp.dot(qm, kt_s[...], preferred_element_type=F32)
        e = jnp.exp2(s - jnp.max(s, axis=1, keepdims=True))
        return e, jnp.sum(e, axis=1, keepdims=True)

    def body(h, carry):
        e0, l0 = probs(2 * h)
        e1, l1 = probs(2 * h + 1)
        a = (e0 - e1 * (lam * l0 / l1)).astype(BF16)
        if latent:
            r = (jnp.dot(a[:, :nl], vl_ref[...], preferred_element_type=F32)
                 + jnp.dot(a[:, nl:], vc_ref[...], preferred_element_type=F32))
        else:
            r = jnp.dot(a, vc_ref[...], preferred_element_type=F32)
        acc_s[...] += jnp.where(lane // HEAD_DIM == h, r / l0, 0.0)
        return carry

    lax.fori_loop(0, GROUP_HEADS, body, 0)
    y = _head_rms_norm(acc_s[...]) * ng_ref[...] * (1.0 - lam_init)
    o_ref[...] = y.astype(o_ref.dtype)


def diff_attention(p_l, p_c, cos_l, sin_l, lam_params, norm_g, layer_idx, with_ctx_out, tq):
    b, n, _ = p_l.shape
    lc = p_c.shape[1]
    lam_init = 0.8 - 0.6 * math.exp(-0.3 * layer_idx)
    scale = (HEAD_DIM // 2) ** -0.5
    ng = jnp.tile(norm_g, GROUP_HEADS)[None, :]
    gw = GROUP_W
    blk = gw // LANES
    kw = dict(lam_init=lam_init, scale=scale, nl=n, nc=lc)
    y_l = pl.pallas_call(
        functools.partial(_diff_kernel, latent=True, **kw),
        grid=(b, n // tq),
        in_specs=[
            pl.BlockSpec((None, tq, gw), lambda i, j: (i, j, CB_DIFF_Q // blk)),
            pl.BlockSpec((None, n, gw), lambda i, j: (i, 0, CB_DIFF_K // blk)),
            pl.BlockSpec((None, lc, gw), lambda i, j: (i, 0, CB_DIFF_K // blk)),
            pl.BlockSpec((None, n, gw), lambda i, j: (i, 0, CB_DIFF_V // blk)),
            pl.BlockSpec((None, lc, gw), lambda i, j: (i, 0, CB_DIFF_V // blk)),
            pl.BlockSpec((tq, LANES), lambda i, j: (j, 0)),
            pl.BlockSpec((tq, LANES), lambda i, j: (j, 0)),
            pl.BlockSpec((n, LANES), lambda i, j: (0, 0)),
            pl.BlockSpec((n, LANES), lambda i, j: (0, 0)),
            pl.BlockSpec((4, HEAD_DIM // 2), lambda i, j: (0, 0)),
            pl.BlockSpec((1, gw), lambda i, j: (0, 0)),
        ],
        out_specs=pl.BlockSpec((None, tq, gw), lambda i, j: (i, j, 0)),
        out_shape=jax.ShapeDtypeStruct((b, n, gw), BF16),
        scratch_shapes=[pltpu.VMEM((gw, n + lc), BF16), pltpu.VMEM((tq, gw), F32)],
        compiler_params=_cparams("parallel", "arbitrary"),
    )(p_l, p_l, p_c, p_l, p_c, cos_l, sin_l, cos_l, sin_l, lam_params, ng)
    y_c = None
    if with_ctx_out:
        y_c = pl.pallas_call(
            functools.partial(_diff_kernel, latent=False, **kw),
            grid=(b,),
            in_specs=[
                pl.BlockSpec((None, lc, gw), lambda i: (i, 0, CB_DIFF_Q // blk)),
                pl.BlockSpec((None, lc, gw), lambda i: (i, 0, CB_DIFF_K // blk)),
                pl.BlockSpec((None, lc, gw), lambda i: (i, 0, CB_DIFF_V // blk)),
                pl.BlockSpec((4, HEAD_DIM // 2), lambda i: (0, 0)),
                pl.BlockSpec((1, gw), lambda i: (0, 0)),
            ],
            out_specs=pl.BlockSpec((None, lc, gw), lambda i: (i, 0, 0)),
            out_shape=jax.ShapeDtypeStruct((b, lc, gw), BF16),
            scratch_shapes=[pltpu.VMEM((gw, lc), BF16), pltpu.VMEM((lc, gw), F32)],
            compiler_params=_cparams("parallel"),
        )(p_c, p_c, p_c, lam_params, ng)
    return y_l, y_c


def _swa_kernel(*refs, latent, nl, tq, win):
    if latent:
        q_ref, kl_ref, kc_ref, vl_ref, vc_ref, cq_ref, sq_ref, ck_ref, sk_ref, sink_ref, o_ref, kr_s = refs
    else:
        q_ref, kc_ref, vc_ref, sink_ref, o_ref = refs
    hk_w = LANES // 2
    lane = lax.broadcasted_iota(jnp.int32, (1, LANES), 1)
    scale = HEAD_DIM ** -0.5

    def rot(x, cos, sin_signed):
        return x * cos + pltpu.roll(x, hk_w, 1) * sin_signed

    if latent:
        @pl.when(pl.program_id(1) == 0)
        def _():
            kr_s[...] = rot(kl_ref[...].astype(F32), ck_ref[...], sk_ref[...]).astype(BF16)

        j = pl.program_id(1)
        ws = pl.multiple_of(jnp.clip(j * tq - WINDOW, 0, nl - win), WINDOW)
        kwin = kr_s[pl.ds(ws, win), :]
        vwin = vl_ref[pl.ds(ws, win), :]
        qpos = j * tq + lax.broadcasted_iota(jnp.int32, (tq, 1), 0)
        kpos = ws + lax.broadcasted_iota(jnp.int32, (1, win), 1)
        valid = jnp.abs(qpos - kpos) <= WINDOW
    kc = kc_ref[...]
    vc = vc_ref[...]
    nt = (((1,), (1,)), ((), ()))
    for hk in range(2):
        pieces = []
        for g in range(2):
            qg = q_ref[:, g * LANES:(g + 1) * LANES].astype(F32)
            if latent:
                qg = rot(qg, cq_ref[...], sq_ref[...])
            qm = jnp.where((lane % hk_w) // 32 == hk, qg * scale, 0.0).astype(BF16)
            snk = sink_ref[0:1, hk * 2 + g:hk * 2 + g + 1]
            sc = lax.dot_general(qm, kc, nt, preferred_element_type=F32)
            m = jnp.maximum(jnp.max(sc, axis=1, keepdims=True), snk)
            if latent:
                sb = lax.dot_general(qm, kwin, nt, preferred_element_type=F32)
                sb = jnp.where(valid, sb, -jnp.inf)
                m = jnp.maximum(m, jnp.max(sb, axis=1, keepdims=True))
            ec = jnp.exp(sc - m)
            den = jnp.sum(ec, axis=1, keepdims=True) + jnp.exp(snk - m)
            o = jnp.dot(ec.astype(BF16), vc, preferred_element_type=F32)
            if latent:
                eb = jnp.exp(sb - m)
                den = den + jnp.sum(eb, axis=1, keepdims=True)
                o = o + jnp.dot(eb.astype(BF16), vwin, preferred_element_type=F32)
            o = o / den
            if hk != g:
                o = pltpu.roll(o, HEAD_DIM, 1)
            pieces.append(o)
        o_ref[:, hk * LANES:(hk + 1) * LANES] = jnp.where(lane < HEAD_DIM, pieces[0], pieces[1]).astype(o_ref.dtype)


def swa_attention(p_l, p_c, cos_l, sin_l, sink, with_ctx_out, tq):
    b, n, _ = p_l.shape
    lc = p_c.shape[1]
    gw = GROUP_W
    win = tq + 2 * WINDOW
    snk = jnp.zeros((1, LANES), F32).at[0, :GROUP_HEADS].set(sink)
    y_l = pl.pallas_call(
        functools.partial(_swa_kernel, latent=True, nl=n, tq=tq, win=win),
        grid=(b, n // tq),
        in_specs=[
            pl.BlockSpec((None, tq, gw), lambda i, j: (i, j, CB_SWA_Q // 2)),
            pl.BlockSpec((None, n, LANES), lambda i, j: (i, 0, CB_SWA_K)),
            pl.BlockSpec((None, lc, LANES), lambda i, j: (i, 0, CB_SWA_K)),
            pl.BlockSpec((None, n, LANES), lambda i, j: (i, 0, CB_SWA_V)),
            pl.BlockSpec((None, lc, LANES), lambda i, j: (i, 0, CB_SWA_V)),
            pl.BlockSpec((tq, LANES), lambda i, j: (j, 0)),
            pl.BlockSpec((tq, LANES), lambda i, j: (j, 0)),
            pl.BlockSpec((n, LANES), lambda i, j: (0, 0)),
            pl.BlockSpec((n, LANES), lambda i, j: (0, 0)),
            pl.BlockSpec((1, LANES), lambda i, j: (0, 0)),
        ],
        out_specs=pl.BlockSpec((None, tq, gw), lambda i, j: (i, j, 0)),
        out_shape=jax.ShapeDtypeStruct((b, n, gw), BF16),
        scratch_shapes=[pltpu.VMEM((n, LANES), BF16)],
        compiler_params=_cparams("parallel", "arbitrary"),
    )(p_l, p_l, p_c, p_l, p_c, cos_l, sin_l, cos_l, sin_l, snk)
    y_c = None
    if with_ctx_out:
        y_c = pl.pallas_call(
            functools.partial(_swa_kernel, latent=False, nl=n, tq=lc, win=win),
            grid=(b,),
            in_specs=[
                pl.BlockSpec((None, lc, gw), lambda i: (i, 0, CB_SWA_Q // 2)),
                pl.BlockSpec((None, lc, LANES), lambda i: (i, 0, CB_SWA_K)),
                pl.BlockSpec((None, lc, LANES), lambda i: (i, 0, CB_SWA_V)),
                pl.BlockSpec((1, LANES), lambda i: (0, 0)),
            ],
            out_specs=pl.BlockSpec((None, lc, gw), lambda i: (i, 0, 0)),
            out_shape=jax.ShapeDtypeStruct((b, lc, gw), BF16),
            compiler_params=_cparams("parallel"),
        )(p_c, p_c, p_c, snk)
    return y_l, y_c


def _ret_tables(ret_decay):
    c = RET_CHUNK
    lg = jax.nn.log_sigmoid(ret_decay.astype(F32))
    pos = jnp.arange(c, dtype=F32)
    head_k = (np.arange(GROUP_W) % LANES) // (HEAD_DIM // 2)
    head_v = np.arange(GROUP_W) // HEAD_DIM
    qd, kd, dm, cd = [], [], [], []
    for d in range(2):
        l = lg[d]
        qpow = (pos + 1.0) if d == 0 else (c - pos)
        kpow = (c - 1.0 - pos) if d == 0 else pos
        qd.append(jnp.exp(qpow[:, None] * l[head_k][None, :]))
        kd.append(jnp.exp(kpow[:, None] * l[head_k][None, :]))
        rel = pos[:, None] - pos[None, :]
        rel = rel if d == 0 else -rel
        dm.append(jnp.where(rel[None] >= 0, jnp.exp(l[:, None, None] * jnp.maximum(rel, 0.0)[None]), 0.0))
        cd.append(jnp.exp(l * c)[head_v][None, :])
    return jnp.stack(qd), jnp.stack(kd), jnp.stack(dm), jnp.stack(cd)


def _ret_kernel(ql_ref, kl_ref, vl_ref, gl_ref, qc_ref, kc_ref, vc_ref, gc_ref, cos_ref, sin_ref,
                qd_ref, kd_ref, dm_ref, cd_ref, ol_ref, oc_ref, s_s, accl_s, accc_s, *, nl, nc, with_ctx_out):
    c = RET_CHUNK
    half = GROUP_W // 2
    scale = HEAD_DIM ** -0.5
    lane = lax.broadcasted_iota(jnp.int32, (1, GROUP_W), 1)
    head_k = (lane % LANES) // (HEAD_DIM // 2)
    head_v = lane // HEAD_DIM
    row_hk = (lax.broadcasted_iota(jnp.int32, (GROUP_W, 1), 0) % LANES) // (HEAD_DIM // 2)
    same_head = row_hk == head_v
    nt = (((1,), (1,)), ((), ()))

    def rot(x, cos, sin):
        x1, x2 = x[:, :half], x[:, half:]
        return jnp.concatenate([x1 * cos - x2 * sin, x1 * sin + x2 * cos], axis=1)

    def step(d, refs, row, rotate, want_out):
        q_ref, k_ref, v_ref = refs
        q = q_ref[pl.ds(row, c), :].astype(F32)
        k = k_ref[pl.ds(row, c), :].astype(F32)
        v = v_ref[pl.ds(row, c), :]
        if rotate:
            cos, sin = cos_ref[pl.ds(row, c), :], sin_ref[pl.ds(row, c), :]
            q, k = rot(q, cos, sin), rot(k, cos, sin)
        q = q * scale
        s = s_s[d]
        o = None
        if want_out:
            o = _mm(q * qd_ref[d], s, 2, 2)
            for h in range(GROUP_HEADS):
                qm = jnp.where(head_k == h, q, 0.0)
                sc = _mm(qm, k, 2, 2, nt) * dm_ref[d, h]
                o = o + jnp.where(head_v == h, _mm(sc, v, 2, 1), 0.0)
        kv = _mm((k * kd_ref[d]).T, v, 2, 1)
        s_s[d] = s * cd_ref[d] + jnp.where(same_head, kv, 0.0)
        return o

    def finish(o, g_ref, row, o_ref):
        y = _head_rms_norm(o)
        g = g_ref[pl.ds(row, c), :].astype(F32)
        o_ref[pl.ds(row, c), :] = (g * jax.nn.sigmoid(g) * y).astype(o_ref.dtype)

    lat = (ql_ref, kl_ref, vl_ref)
    ctx = (qc_ref, kc_ref, vc_ref)
    ncl, ncc = nl // c, nc // c

    s_s[...] = jnp.zeros_like(s_s)
    for t in range(ncc):
        rf, rb = t * c, (ncc - 1 - t) * c
        of = step(0, ctx, rf, False, with_ctx_out)
        ob = step(1, ctx, rb, False, with_ctx_out)
        if with_ctx_out:
            accc_s[0, rf:rf + c, :] = of
            accc_s[1, rb:rb + c, :] = ob
    if with_ctx_out:
        for t in range(ncc):
            finish(accc_s[0, t * c:(t + 1) * c, :] + accc_s[1, t * c:(t + 1) * c, :], gc_ref, t * c, oc_ref)
    else:
        oc_ref[...] = jnp.zeros_like(oc_ref)

    def both(t, carry):
        rf = pl.multiple_of(t * c, c)
        rb = pl.multiple_of((ncl - 1 - t) * c, c)
        accl_s[0, pl.ds(rf, c), :] = step(0, lat, rf, True, True)
        accl_s[1, pl.ds(rb, c), :] = step(1, lat, rb, True, True)
        return carry

    lax.fori_loop(0, ncl, both, 0)

    def fin(t, carry):
        row = pl.multiple_of(t * c, c)
        finish(accl_s[0, pl.ds(row, c), :] + accl_s[1, pl.ds(row, c), :], gl_ref, row, ol_ref)
        return carry

    lax.fori_loop(0, ncl, fin, 0)


def retention(p_l, p_c, cos_l, sin_l, ret_decay, with_ctx_out):
    b, n, _ = p_l.shape
    lc = p_c.shape[1]
    gw = GROUP_W
    qd, kd, dm, cd = _ret_tables(ret_decay)
    c = RET_CHUNK
    lat = lambda cb: pl.BlockSpec((None, n, gw), lambda i: (i, 0, cb // 2))
    ctx = lambda cb: pl.BlockSpec((None, lc, gw), lambda i: (i, 0, cb // 2))
    full = lambda shape: pl.BlockSpec(shape, lambda i: (0,) * len(shape))
    y_l, y_c = pl.pallas_call(
        functools.partial(_ret_kernel, nl=n, nc=lc, with_ctx_out=with_ctx_out),
        grid=(b,),
        in_specs=[lat(CB_RET_Q), lat(CB_RET_K), lat(CB_RET_V), lat(CB_RET_G),
                  ctx(CB_RET_Q), ctx(CB_RET_K), ctx(CB_RET_V), ctx(CB_RET_G),
                  full((n, LANES)), full((n, LANES)),
                  full((2, c, gw)), full((2, c, gw)), full((2, GROUP_HEADS, c, c)), full((2, 1, gw))],
        out_specs=[pl.BlockSpec((None, n, gw), lambda i: (i, 0, 0)),
                   pl.BlockSpec((None, lc, gw), lambda i: (i, 0, 0))],
        out_shape=[jax.ShapeDtypeStruct((b, n, gw), BF16), jax.ShapeDtypeStruct((b, lc, gw), BF16)],
        scratch_shapes=[pltpu.VMEM((2, gw, gw), F32), pltpu.VMEM((2, n, gw), F32), pltpu.VMEM((2, lc, gw), F32)],
        compiler_params=_cparams("parallel"),
    )(p_l, p_l, p_l, p_l, p_c, p_c, p_c, p_c, cos_l, sin_l, qd, kd, dm, cd)
    return y_l, (y_c if with_ctx_out else None)


def _gdn_prep_kernel(x_ref, w_ref, o_ref):
    j = pl.program_id(1)
    x = x_ref[...].astype(F32)
    n = x.shape[0]
    row = lax.broadcasted_iota(jnp.int32, (n, 1), 0)
    xp = jnp.where(row == 0, 0.0, pltpu.roll(x, 1, 0))
    xn = jnp.where(row == n - 1, 0.0, pltpu.roll(x, n - 1, 0))
    y = xp * w_ref[0:1, :] + x * w_ref[1:2, :] + xn * w_ref[2:3, :]
    y = y * jax.nn.sigmoid(y)
    ss = jnp.dot(y * y, _head_mean_matrix() * HEAD_DIM, preferred_element_type=F32, precision=lax.Precision.HIGHEST)
    nrm = lax.rsqrt(ss + 1e-6) * jnp.where(j == 0, HEAD_DIM ** -0.5, 1.0)
    y = jnp.where(j < 2, y * nrm, y)
    o_ref[...] = y.astype(o_ref.dtype)


def gdn_prep(p, conv_w):
    b, n, _ = p.shape
    gw = GROUP_W
    return pl.pallas_call(
        _gdn_prep_kernel,
        grid=(b, 3),
        in_specs=[pl.BlockSpec((None, n, gw), lambda i, j: (i, 0, CB_GDN_QKV // 2 + j)),
                  pl.BlockSpec((SHORT_CONV, gw), lambda i, j: (0, j))],
        out_specs=pl.BlockSpec((None, None, n, gw), lambda i, j: (i, j, 0, 0)),
        out_shape=jax.ShapeDtypeStruct((b, 3, n, gw), BF16),
        compiler_params=_cparams("parallel", "parallel"),
    )(p, conv_w)


def _gdn_kernel(xl_ref, gl_ref, abl_ref, xc_ref, gc_ref, abc_ref, par_ref, ng_ref, ol_ref, oc_ref,
                s_s, accl_s, accc_s, *, nl, nc, with_ctx_out):
    c = GDN_CHUNK
    gh, gw = GROUP_HEADS, GROUP_W
    hp = lax.Precision.HIGHEST
    lane = lax.broadcasted_iota(jnp.int32, (1, gw), 1)
    head_v = lane // HEAD_DIM
    lane128 = lax.broadcasted_iota(jnp.int32, (1, LANES), 1)
    r_i = lax.broadcasted_iota(jnp.int32, (gw, 1), 0)
    c_i = lax.broadcasted_iota(jnp.int32, (1, gw), 1)
    same_head = (r_i // HEAD_DIM) == (c_i // HEAD_DIM)
    ti = lax.broadcasted_iota(jnp.int32, (c, 1), 0)
    tj = lax.broadcasted_iota(jnp.int32, (1, c), 1)
    nt = (((1,), (1,)), ((), ()))

    def stack(x):
        return jnp.concatenate([jnp.where(head_v == h, x, 0.0) for h in range(gh)], axis=0)

    def unstack(y):
        return y[0:c] + y[c:2 * c] + y[2 * c:3 * c] + y[3 * c:4 * c]

    def by_head(cols):
        out = cols[gh - 1]
        for h in reversed(range(gh - 1)):
            out = jnp.where(head_v == h, cols[h], out)
        return out

    def step(d, refs, row, want_out):
        x_ref, ab_ref = refs
        q = x_ref[0, pl.ds(row, c), :].astype(F32)
        k = x_ref[1, pl.ds(row, c), :].astype(F32)
        v = x_ref[2, pl.ds(row, c), :].astype(F32)
        ab = ab_ref[pl.ds(row, c), :]
        xa = ab + par_ref[1:2, :]
        la = -par_ref[0:1, :] * (jnp.maximum(xa, 0.0) + jnp.log(1.0 + jnp.exp(-jnp.abs(xa))))
        bt = jax.nn.sigmoid(ab)
        tri = jnp.where((ti >= tj) if d == 0 else (ti <= tj), 1.0, 0.0)
        g = jnp.dot(tri, la, preferred_element_type=F32, precision=hp)
        gcols = [jnp.sum(jnp.where(lane128 == d * gh + h, g, 0.0), axis=1, keepdims=True) for h in range(gh)]
        bcols = [jnp.sum(jnp.where(lane128 == 2 * gh + d * gh + h, bt, 0.0), axis=1, keepdims=True) for h in range(gh)]
        gexp, bexp = by_head(gcols), by_head(bcols)
        glast = gexp[c - 1:c, :] if d == 0 else gexp[0:1, :]
        gc_mat = jnp.concatenate([jnp.broadcast_to(gcols[h], (c, gw)) for h in range(gh)], axis=0)
        b_col = jnp.concatenate(bcols, axis=0)
        i_loc, j_loc = r_i % c, c_i % c
        incl = same_head & ((i_loc >= j_loc) if d == 0 else (i_loc <= j_loc))
        strict = same_head & ((i_loc > j_loc) if d == 0 else (i_loc < j_loc))
        decay = jnp.exp(jnp.where(incl, gc_mat - gc_mat.T, -jnp.inf))
        ks = stack(k).astype(BF16)
        kk = lax.dot_general(ks, ks, nt, preferred_element_type=F32)
        a_mat = jnp.where(strict, b_col * kk * decay, 0.0)
        y = jnp.concatenate([stack(bexp * v), stack(bexp * jnp.exp(gexp) * k)], axis=1)
        pair = (i_loc // 2 == j_loc // 2) & (i_loc != j_loc)
        t_inv = jnp.where(r_i == c_i, 1.0, 0.0) - jnp.where(pair, a_mat, 0.0)
        sz = 2
        while sz < c:
            off = (i_loc // (2 * sz) == j_loc // (2 * sz)) & (i_loc // sz != j_loc // sz)
            tb16 = t_inv.astype(BF16)
            ta = jnp.dot(tb16, jnp.where(off, a_mat, 0.0).astype(BF16), preferred_element_type=F32)
            t_inv = t_inv - jnp.dot(ta.astype(BF16), tb16, preferred_element_type=F32)
            sz *= 2
        y = _mm(t_inv, y, 2, 2)
        u_base, w = unstack(y[:, :gw]), unstack(y[:, gw:])
        s = s_s[d]
        sb = s.astype(BF16)
        u = u_base - jnp.dot(w.astype(BF16), sb, preferred_element_type=F32)
        k_tail = k * jnp.exp(glast - gexp)
        pad = jnp.zeros((LANES - c, gw), F32)
        kt_t = jnp.concatenate([k_tail, pad], axis=0).T.astype(BF16)
        u_pad = jnp.concatenate([u, pad], axis=0).astype(BF16)
        s_s[d] = s * jnp.exp(glast) + jnp.where(same_head, jnp.dot(kt_t, u_pad, preferred_element_type=F32), 0.0)
        if not want_out:
            return None
        qs = stack(q).astype(BF16)
        qk = lax.dot_general(qs, ks, nt, preferred_element_type=F32) * decay
        o = jnp.dot((q * jnp.exp(gexp)).astype(BF16), sb, preferred_element_type=F32)
        return o + unstack(_mm(qk, stack(u), 2, 2))

    def finish(o, g_ref, row, o_ref):
        y = _head_rms_norm(o) * ng_ref[...]
        g = g_ref[pl.ds(row, c), :].astype(F32)
        o_ref[pl.ds(row, c), :] = (y * g * jax.nn.sigmoid(g)).astype(o_ref.dtype)

    lat = (xl_ref, abl_ref)
    ctx = (xc_ref, abc_ref)
    ncl, ncc = nl // c, nc // c

    s_s[...] = jnp.zeros_like(s_s)
    for t in range(ncc):
        rf, rb = t * c, (ncc - 1 - t) * c
        of = step(0, ctx, rf, with_ctx_out)
        ob = step(1, ctx, rb, with_ctx_out)
        if with_ctx_out:
            accc_s[0, rf:rf + c, :] = of
            accc_s[1, rb:rb + c, :] = ob
    if with_ctx_out:
        for t in range(ncc):
            finish(accc_s[0, t * c:(t + 1) * c, :] + accc_s[1, t * c:(t + 1) * c, :], gc_ref, t * c, oc_ref)
    else:
        oc_ref[...] = jnp.zeros_like(oc_ref)

    def both(t, carry):
        rf = pl.multiple_of(t * c, c)
        rb = pl.multiple_of((ncl - 1 - t) * c, c)
        accl_s[0, pl.ds(rf, c), :] = step(0, lat, rf, True)
        accl_s[1, pl.ds(rb, c), :] = step(1, lat, rb, True)
        return carry

    lax.fori_loop(0, ncl, both, 0)

    def fin(t, carry):
        row = pl.multiple_of(t * c, c)
        finish(accl_s[0, pl.ds(row, c), :] + accl_s[1, pl.ds(row, c), :], gl_ref, row, ol_ref)
        return carry

    lax.fori_loop(0, ncl, fin, 0)


def gdn(p_l, ab_l, p_c, ab_c, conv_w, a_log, dt_bias, norm_g, with_ctx_out):
    b, n, _ = p_l.shape
    lc = p_c.shape[1]
    gw = GROUP_W
    x_l, x_c = gdn_prep(p_l, conv_w), gdn_prep(p_c, conv_w)
    par = jnp.zeros((8, LANES), F32)
    par = par.at[0, :2 * GROUP_HEADS].set(jnp.exp(a_log.astype(F32)).reshape(-1))
    par = par.at[1, :2 * GROUP_HEADS].set(dt_bias.astype(F32).reshape(-1))
    ng = jnp.tile(norm_g, GROUP_HEADS)[None, :]
    full = lambda shape: pl.BlockSpec(shape, lambda i: (0,) * len(shape))
    y_l, y_c = pl.pallas_call(
        functools.partial(_gdn_kernel, nl=n, nc=lc, with_ctx_out=with_ctx_out),
        grid=(b,),
        in_specs=[pl.BlockSpec((None, 3, n, gw), lambda i: (i, 0, 0, 0)),
                  pl.BlockSpec((None, n, gw), lambda i: (i, 0, CB_GDN_G // 2)),
                  pl.BlockSpec((None, n, LANES), lambda i: (i, 0, 0)),
                  pl.BlockSpec((None, 3, lc, gw), lambda i: (i, 0, 0, 0)),
                  pl.BlockSpec((None, lc, gw), lambda i: (i, 0, CB_GDN_G // 2)),
                  pl.BlockSpec((None, lc, LANES), lambda i: (i, 0, 0)),
                  full((8, LANES)), full((1, gw))],
        out_specs=[pl.BlockSpec((None, n, gw), lambda i: (i, 0, 0)),
                   pl.BlockSpec((None, lc, gw), lambda i: (i, 0, 0))],
        out_shape=[jax.ShapeDtypeStruct((b, n, gw), BF16), jax.ShapeDtypeStruct((b, lc, gw), BF16)],
        scratch_shapes=[pltpu.VMEM((2, gw, gw), F32), pltpu.VMEM((2, n, gw), F32), pltpu.VMEM((2, lc, gw), F32)],
        compiler_params=_cparams("parallel"),
    )(x_l, p_l, ab_l, x_c, p_c, ab_c, par, ng)
    return y_l, (y_c if with_ctx_out else None)


def _layer_norm(x, g, b):
    mu = jnp.mean(x, -1, keepdims=True)
    xc = x - mu
    var = jnp.mean(xc * xc, -1, keepdims=True)
    return xc * lax.rsqrt(var + LN_EPS) * g + b


def _outproj_kernel(y_ref, h_ref, mod_ref, w_ref, ln_ref, rw_ref, hn_ref, u2_ref, aff_ref, *, alpha):
    mix = jnp.dot(y_ref[...], w_ref[...], preferred_element_type=F32)
    hn = _layer_norm(alpha * h_ref[...] + mod_ref[2:3, :] * mix, ln_ref[0:1, :], ln_ref[1:2, :])
    hn_ref[...] = hn
    u2 = hn * (1.0 + mod_ref[4:5, :]) + mod_ref[3:4, :]
    u2_ref[...] = u2.astype(BF16)
    logits = _mm(rw_ref[...], u2, 2, 2, (((1,), (1,)), ((), ())))
    m = jnp.max(logits, 0, keepdims=True)
    e = jnp.exp(logits - m)
    aff_ref[...] = e / jnp.sum(e, 0, keepdims=True)


def outproj(y, h, mod, w_bf16, ln, rw_t_bf16, alpha, tm):
    b, n, d = h.shape
    mixw = y.shape[-1]
    ne = rw_t_bf16.shape[0]
    return pl.pallas_call(
        functools.partial(_outproj_kernel, alpha=alpha),
        grid=(b, n // tm),
        in_specs=[
            pl.BlockSpec((None, tm, mixw), lambda i, j: (i, j, 0)),
            pl.BlockSpec((None, tm, d), lambda i, j: (i, j, 0)),
            pl.BlockSpec((None, 6, d), lambda i, j: (i, 0, 0)),
            pl.BlockSpec((mixw, d), lambda i, j: (0, 0)),
            pl.BlockSpec((2, d), lambda i, j: (0, 0)),
            pl.BlockSpec((ne, d), lambda i, j: (0, 0)),
        ],
        out_specs=[
            pl.BlockSpec((None, tm, d), lambda i, j: (i, j, 0)),
            pl.BlockSpec((None, tm, d), lambda i, j: (i, j, 0)),
            pl.BlockSpec((None, ne, tm), lambda i, j: (i, 0, j)),
        ],
        out_shape=[
            jax.ShapeDtypeStruct((b, n, d), F32),
            jax.ShapeDtypeStruct((b, n, d), BF16),
            jax.ShapeDtypeStruct((b, ne, n), F32),
        ],
        compiler_params=_cparams("parallel", "parallel"),
    )(y, h, mod, w_bf16, ln, rw_t_bf16)


def _ffn_kernel(x_ref, wg_ref, wu_ref, wd_ref, o_ref, acc_s, *, rows):
    f = pl.program_id(2)
    tb, cap, d = x_ref.shape

    @pl.when(f == 0)
    def _():
        acc_s[...] = jnp.zeros_like(acc_s)

    wg = wg_ref[...].astype(BF16)
    wu = wu_ref[...].astype(BF16)
    wd = wd_ref[...].astype(BF16)
    for bi in range(tb):
        for r in range(cap // rows):
            x = x_ref[bi, r * rows:(r + 1) * rows, :]
            g = jnp.dot(x, wg, preferred_element_type=F32)
            u = jnp.dot(x, wu, preferred_element_type=F32)
            hid = (g * jax.nn.sigmoid(g) * u).astype(BF16)
            acc_s[bi, r * rows:(r + 1) * rows, :] += jnp.dot(hid, wd, preferred_element_type=F32)

    @pl.when(f == pl.num_programs(2) - 1)
    def _():
        o_ref[...] = acc_s[...].astype(o_ref.dtype)


def expert_ffn(xin, w_gate, w_up, w_down, layer, tb, tf, rows):
    b, ne, cap, d = xin.shape
    ff = w_gate.shape[-1]
    return pl.pallas_call(
        functools.partial(_ffn_kernel, rows=rows),
        grid=(ne, b // tb, ff // tf),
        in_specs=[
            pl.BlockSpec((tb, None, cap, d), lambda e, i, f: (i, e, 0, 0)),
            pl.BlockSpec((None, None, d, tf), lambda e, i, f: (layer, e, 0, f)),
            pl.BlockSpec((None, None, d, tf), lambda e, i, f: (layer, e, 0, f)),
            pl.BlockSpec((None, None, tf, d), lambda e, i, f: (layer, e, f, 0)),
        ],
        out_specs=pl.BlockSpec((tb, None, cap, d), lambda e, i, f: (i, e, 0, 0)),
        out_shape=jax.ShapeDtypeStruct((b, ne, cap, d), BF16),
        scratch_shapes=[pltpu.VMEM((tb, cap, d), F32)],
        compiler_params=_cparams("parallel", "parallel", "arbitrary"),
    )(xin, w_gate, w_up, w_down)


def _route_kernel(aff_ref, pos_ref, *, cap):
    aff = aff_ref[...]
    ne, n = aff.shape
    bits = pltpu.bitcast(aff, jnp.int32)

    def count(mask):
        return jnp.sum(jnp.where(mask, 1.0, 0.0), axis=1, keepdims=True)

    def search(i, t):
        cand = t | (jnp.int32(1) << (30 - i))
        return jnp.where(count(bits >= cand) >= cap, cand, t)

    thr = lax.fori_loop(0, 31, search, jnp.zeros((ne, 1), jnp.int32))
    gt = bits > thr
    eq = bits == thr
    need = cap - count(gt)

    tile = 256 if n % 256 == 0 else n
    r = lax.broadcasted_iota(jnp.int32, (tile, tile), 0)
    cc = lax.broadcasted_iota(jnp.int32, (tile, tile), 1)
    upper = jnp.where(r <= cc, 1.0, 0.0).astype(BF16)

    def prefix_excl(m):
        outs, carry = [], jnp.zeros((ne, 1), F32)
        for j in range(n // tile):
            mt = m[:, j * tile:(j + 1) * tile]
            inc = jnp.dot(mt.astype(BF16), upper, preferred_element_type=F32) + carry
            outs.append(inc - mt)
            carry = inc[:, tile - 1:tile]
        return jnp.concatenate(outs, axis=1)

    eqf = jnp.where(eq, 1.0, 0.0)
    sel = gt | (eq & (prefix_excl(eqf) < need))
    pos_ref[...] = jnp.where(sel, prefix_excl(jnp.where(sel, 1.0, 0.0)), -1.0)


def route(aff_t, cap):
    b, ne, n = aff_t.shape
    return pl.pallas_call(
        functools.partial(_route_kernel, cap=cap),
        grid=(b,),
        in_specs=[pl.BlockSpec((None, ne, n), lambda i: (i, 0, 0))],
        out_specs=pl.BlockSpec((None, ne, n), lambda i: (i, 0, 0)),
        out_shape=jax.ShapeDtypeStruct((b, ne, n), F32),
        compiler_params=_cparams("parallel"),
    )(aff_t)


def _gather_kernel(pos_ref, u_ref, x_ref):
    cap = x_ref.shape[0]
    slot = lax.broadcasted_iota(jnp.int32, (cap, 1), 0).astype(F32)
    onehot = jnp.where(pos_ref[...] == slot, 1.0, 0.0).astype(BF16)
    x_ref[...] = jnp.dot(onehot, u_ref[...], preferred_element_type=F32).astype(x_ref.dtype)


def gather_tokens(pos, u2, cap):
    b, ne, n = pos.shape
    d = u2.shape[-1]
    return pl.pallas_call(
        _gather_kernel,
        grid=(b, ne),
        in_specs=[pl.BlockSpec((None, None, 1, n), lambda i, e: (i, e, 0, 0)),
                  pl.BlockSpec((None, n, d), lambda i, e: (i, 0, 0))],
        out_specs=pl.BlockSpec((None, None, cap, d), lambda i, e: (i, e, 0, 0)),
        out_shape=jax.ShapeDtypeStruct((b, ne, cap, d), BF16),
        compiler_params=_cparams("parallel", "arbitrary"),
    )(pos.reshape(b, ne, 1, n), u2)


def _scatter_kernel(post_ref, afft_ref, y_ref, f_ref):
    e = pl.program_id(2)
    tn, ne = post_ref.shape
    cap = y_ref.shape[0]

    @pl.when(e == 0)
    def _():
        f_ref[...] = jnp.zeros_like(f_ref)

    lane_e = lax.broadcasted_iota(jnp.int32, (1, ne), 1)
    pos_col = jnp.sum(jnp.where(lane_e == e, post_ref[...], 0.0), axis=1, keepdims=True)
    w_col = jnp.sum(jnp.where(lane_e == e, afft_ref[...], 0.0), axis=1, keepdims=True)
    slot = lax.broadcasted_iota(jnp.int32, (1, cap), 1).astype(F32)
    onehot = jnp.where(pos_col == slot, 1.0, 0.0).astype(BF16)
    f_ref[...] += w_col * jnp.dot(onehot, y_ref[...], preferred_element_type=F32)


def scatter_tokens(pos_t, aff_tm, y, tn):
    b, n, ne = pos_t.shape
    cap, d = y.shape[2], y.shape[3]
    return pl.pallas_call(
        _scatter_kernel,
        grid=(b, n // tn, ne),
        in_specs=[pl.BlockSpec((None, tn, ne), lambda i, j, e: (i, j, 0)),
                  pl.BlockSpec((None, tn, ne), lambda i, j, e: (i, j, 0)),
                  pl.BlockSpec((None, None, cap, d), lambda i, j, e: (i, e, 0, 0))],
        out_specs=pl.BlockSpec((None, tn, d), lambda i, j, e: (i, j, 0)),
        out_shape=jax.ShapeDtypeStruct((b, n, d), F32),
        compiler_params=_cparams("parallel", "parallel", "arbitrary"),
    )(pos_t, aff_tm, y)


def _ln2_kernel(h_ref, f_ref, mod_ref, ln_ref, o_ref, *, alpha):
    o_ref[...] = _layer_norm(alpha * h_ref[...] + mod_ref[5:6, :] * f_ref[...], ln_ref[0:1, :], ln_ref[1:2, :])


def ffn_residual_norm(h, f, mod, ln, alpha, tm):
    b, n, d = h.shape
    return pl.pallas_call(
        functools.partial(_ln2_kernel, alpha=alpha),
        grid=(b, n // tm),
        in_specs=[pl.BlockSpec((None, tm, d), lambda i, j: (i, j, 0)),
                  pl.BlockSpec((None, tm, d), lambda i, j: (i, j, 0)),
                  pl.BlockSpec((None, 6, d), lambda i, j: (i, 0, 0)),
                  pl.BlockSpec((2, d), lambda i, j: (0, 0))],
        out_specs=pl.BlockSpec((None, tm, d), lambda i, j: (i, j, 0)),
        out_shape=jax.ShapeDtypeStruct((b, n, d), F32),
        compiler_params=_cparams("parallel", "parallel"),
    )(h, f, mod, ln)


def expert_choice_ffn(u2, aff_t, w_gate, w_up, w_down, layer):
    b, n, d = u2.shape
    ne = aff_t.shape[1]
    cap = CAPACITY_FACTOR * n // ne
    pos = route(aff_t, cap)
    xin = gather_tokens(pos, u2, cap)
    tb = math.gcd(b, 4) if cap >= 512 else b
    y = expert_ffn(xin, w_gate, w_up, w_down, layer, tb=tb, tf=512, rows=min(cap, 256))
    return scatter_tokens(jnp.swapaxes(pos, 1, 2), jnp.swapaxes(aff_t, 1, 2), y, tn=min(n, 2048))


def axial_rope_angles(n, rot_dim):
    rows = n // GRID_W
    row = jnp.repeat(jnp.arange(rows, dtype=F32), GRID_W)
    col = jnp.tile(jnp.arange(GRID_W, dtype=F32), rows)
    n_freq = rot_dim // 4
    inv = ROPE_BASE ** (-jnp.arange(n_freq, dtype=F32) / n_freq)
    return jnp.concatenate([row[:, None] * inv, col[:, None] * inv], -1)


def retention_angles(n, dim):
    theta = 1.0 / (ROPE_BASE ** jnp.linspace(0.0, 1.0, dim // 2, dtype=F32))
    return jnp.arange(n, dtype=F32)[:, None] * theta


def _ada_kernel(c_ref, w_ref, b_ref, o_ref):
    c = c_ref[...]
    o_ref[...] = _mm(c * jax.nn.sigmoid(c), w_ref[...], 2, 2) + b_ref[...]


def ada_modulation(cond_rows, ada_w, ada_b, tn):
    depth, d, nw = ada_w.shape
    rows = cond_rows.shape[0]
    return pl.pallas_call(
        _ada_kernel,
        grid=(depth, nw // tn),
        in_specs=[pl.BlockSpec((rows, d), lambda l, j: (0, 0)),
                  pl.BlockSpec((None, d, tn), lambda l, j: (l, 0, j)),
                  pl.BlockSpec((None, 1, tn), lambda l, j: (l, 0, j))],
        out_specs=pl.BlockSpec((None, rows, tn), lambda l, j: (l, 0, j)),
        out_shape=jax.ShapeDtypeStruct((depth, rows, nw), F32),
        compiler_params=_cparams("parallel", "parallel"),
    )(cond_rows, ada_w, ada_b.reshape(depth, 1, nw))


def kernel(x, c, ctx, c_ctx, ada_w, ada_b, w_in, w_out, ret_decay, diff_lambda, diff_norm, gdn_conv,
           gdn_a_log, gdn_dt_bias, gdn_norm, swa_sink, ln_g, ln_b, router_w, w_gate, w_up, w_down):
    b, n, d = x.shape
    lc = ctx.shape[1]
    depth = ada_w.shape[0]
    alpha = (2 * depth) ** 0.25
    main_perm, tail_perm, _ = _in_col_perm()
    ang_diff = axial_rope_angles(n, HEAD_DIM // 2)
    ang_swa = axial_rope_angles(n, HEAD_DIM)
    cos_diff, sin_diff = jnp.tile(jnp.cos(ang_diff), (1, 8)), jnp.tile(jnp.sin(ang_diff), (1, 8))
    cos_swa = jnp.tile(jnp.cos(ang_swa), (1, 4))
    sin_swa = jnp.tile(jnp.sin(ang_swa), (1, 4)) * jnp.where(jnp.arange(LANES) < LANES // 2, -1.0, 1.0)
    ang_ret = retention_angles(n, HEAD_DIM)
    cos_ret, sin_ret = jnp.tile(jnp.cos(ang_ret), (1, 4)), jnp.tile(jnp.sin(ang_ret), (1, 4))
    sub = 8
    rows = -(-(b + 1) // sub) * sub
    cond_rows = jnp.concatenate([c, c_ctx[None, :], jnp.zeros((rows - b - 1, d), F32)], axis=0)
    mods = ada_modulation(cond_rows, ada_w, ada_b, tn=d)
    h, hc = x, ctx
    for layer in range(depth):
        full_ctx = layer < depth - 1
        mod_l = mods[layer, :b].reshape(b, 6, d)
        mod_c = jnp.broadcast_to(mods[layer, b].reshape(1, 6, d), (b, 6, d))
        wl = w_in[layer]
        w_in_b = jnp.concatenate([wl[:, main_perm], wl[:, tail_perm],
                                  jnp.zeros((d, LANES - tail_perm.size), F32)], axis=1).astype(BF16)
        w_out_b = w_out[layer].astype(BF16)
        rw_t = router_w[layer].T
        ln1 = jnp.stack([ln_g[layer, 0], ln_b[layer, 0]])
        ln2 = jnp.stack([ln_g[layer, 1], ln_b[layer, 1]])
        p_l, ab_l = inproj(h, mod_l, w_in_b, tm=512)
        p_c, ab_c = inproj(hc, mod_c, w_in_b, tm=lc)
        ya_l, ya_c = retention(p_l, p_c, cos_ret, sin_ret, ret_decay[layer], full_ctx)
        yb_l, yb_c = diff_attention(p_l, p_c, cos_diff, sin_diff, diff_lambda[layer], diff_norm[layer],
                                    layer, full_ctx, tq=256)
        yc_l, yc_c = gdn(p_l, ab_l, p_c, ab_c, gdn_conv[layer], gdn_a_log[layer], gdn_dt_bias[layer],
                         gdn_norm[layer], full_ctx)
        yd_l, yd_c = swa_attention(p_l, p_c, cos_swa, sin_swa, swa_sink[layer], full_ctx, tq=256)
        y_l = jnp.concatenate([ya_l, yb_l, yc_l, yd_l], -1)
        h, u2, aff = outproj(y_l, h, mod_l, w_out_b, ln1, rw_t, alpha, tm=512)
        f_l = expert_choice_ffn(u2, aff, w_gate, w_up, w_down, layer)
        h = ffn_residual_norm(h, f_l, mod_l, ln2, alpha, tm=512)
        if full_ctx:
            y_c = jnp.concatenate([ya_c, yb_c, yc_c, yd_c], -1)
            hc, u2c, affc = outproj(y_c, hc, mod_c, w_out_b, ln1, rw_t, alpha, tm=lc)
            f_c = expert_choice_ffn(u2c, affc, w_gate, w_up, w_down, layer)
            hc = ffn_residual_norm(hc, f_c, mod_c, ln2, alpha, tm=lc)
    return h
```

```python
import functools
import math

import numpy as np
import jax
import jax.numpy as jnp
from jax import lax
from jax.experimental import pallas as pl
from jax.experimental.pallas import tpu as pltpu

F32 = jnp.float32
BF16 = jnp.bfloat16

GRID_W = 64
HEAD_DIM = 64
N_MIXERS = 4
GROUP_HEADS = 4
GROUP_W = GROUP_HEADS * HEAD_DIM
ROPE_BASE = 10000.0
RET_CHUNK = 128
GDN_CHUNK = 64
SHORT_CONV = 3
WINDOW = 128
N_EXPERTS = 16
CAPACITY_FACTOR = 2
LN_EPS = 1e-5
RMS_EPS = 1e-6
LANES = 128

VMEM_LIMIT_BYTES = 56 * 1024 * 1024

CB_RET_Q, CB_RET_K, CB_RET_V, CB_RET_G = 0, 2, 4, 6
CB_DIFF_Q, CB_DIFF_K, CB_DIFF_V = 8, 10, 12
CB_GDN_QKV, CB_GDN_G = 14, 20
CB_SWA_Q, CB_SWA_K, CB_SWA_V = 22, 24, 25
MAIN_W = 26 * LANES


def _cparams(*sem):
    return pltpu.CompilerParams(dimension_semantics=sem, vmem_limit_bytes=VMEM_LIMIT_BYTES)


def _split(x, n):
    if x.dtype == BF16:
        return [x]
    terms, r = [], x
    for _ in range(n):
        t = r.astype(BF16)
        terms.append(t)
        r = r - t.astype(F32)
    return terms


def _mm(a, b, na=1, nb=1, dims=None):
    at, bt = _split(a, na), _split(b, nb)
    out = None
    for i, x in enumerate(at):
        for j, y in enumerate(bt):
            if i + j >= max(len(at), len(bt)):
                continue
            t = (jnp.dot(x, y, preferred_element_type=F32) if dims is None
                 else lax.dot_general(x, y, dims, preferred_element_type=F32))
            out = t if out is None else out + t
    return out


def _in_col_perm():
    gw, gh, hd = GROUP_W, GROUP_HEADS, HEAD_DIM
    off = {}
    o = 0
    for name, w in (('ret_q', gw), ('ret_k', gw), ('ret_v', gw), ('ret_g', gw), ('diff_q', gw), ('diff_k', gw),
                    ('diff_v', gw), ('gdn_qkv', 3 * gw), ('gdn_g', gw), ('gdn_a', 2 * gh), ('gdn_b', 2 * gh),
                    ('swa_q', gw), ('swa_k', gw // 2), ('swa_v', gw // 2)):
        off[name] = o
        o += w
    ident = lambda name, w: off[name] + np.arange(w)
    ret = np.zeros(gw, np.int64)
    for h in range(gh):
        for j in range(hd):
            ret[(j // 32) * 128 + h * 32 + j % 32] = h * hd + j
    dif = np.zeros(gw, np.int64)
    for h in range(gh):
        for c in range(2):
            for j in range(32):
                dif[(j // 16) * 128 + (h * 2 + c) * 16 + j % 16] = h * hd + c * 32 + j
    swq = np.zeros(gw, np.int64)
    for hk in range(2):
        for g in range(2):
            for j in range(hd):
                swq[g * 128 + (j // 32) * 64 + hk * 32 + j % 32] = hk * 128 + g * hd + j
    swk = np.zeros(gw // 2, np.int64)
    for hk in range(2):
        for j in range(hd):
            swk[(j // 32) * 64 + hk * 32 + j % 32] = hk * hd + j
    main = np.concatenate([
        off['ret_q'] + ret, off['ret_k'] + ret, ident('ret_v', gw), ident('ret_g', gw),
        off['diff_q'] + dif, off['diff_k'] + dif, ident('diff_v', gw),
        ident('gdn_qkv', 3 * gw), ident('gdn_g', gw),
        off['swa_q'] + swq, off['swa_k'] + swk, ident('swa_v', gw // 2)])
    tail = np.concatenate([ident('gdn_a', 2 * gh), ident('gdn_b', 2 * gh)])
    return main, tail, off


def _inproj_kernel(h_ref, mod_ref, w_ref, o_ref, ab_ref, *, tn):
    u = (h_ref[...] * (1.0 + mod_ref[1:2, :]) + mod_ref[0:1, :]).astype(BF16)
    nmain = o_ref.shape[-1]
    for j in range(nmain // tn):
        o_ref[:, j * tn:(j + 1) * tn] = jnp.dot(
            u, w_ref[:, j * tn:(j + 1) * tn], preferred_element_type=F32).astype(BF16)
    ab_ref[...] = jnp.dot(u, w_ref[:, nmain:], preferred_element_type=F32)


def inproj(h, mod, w_bf16, tm):
    b, n, d = h.shape
    nw = w_bf16.shape[1]
    return pl.pallas_call(
        functools.partial(_inproj_kernel, tn=256),
        grid=(b, n // tm),
        in_specs=[
            pl.BlockSpec((None, tm, d), lambda i, j: (i, j, 0)),
            pl.BlockSpec((None, 6, d), lambda i, j: (i, 0, 0)),
            pl.BlockSpec((d, nw), lambda i, j: (0, 0)),
        ],
        out_specs=[pl.BlockSpec((None, tm, MAIN_W), lambda i, j: (i, j, 0)),
                   pl.BlockSpec((None, tm, LANES), lambda i, j: (i, j, 0))],
        out_shape=[jax.ShapeDtypeStruct((b, n, MAIN_W), BF16),
                   jax.ShapeDtypeStruct((b, n, LANES), F32)],
        compiler_params=_cparams("parallel", "parallel"),
    )(h, mod, w_bf16)


def _head_mean_matrix():
    r = lax.broadcasted_iota(jnp.int32, (GROUP_W, GROUP_W), 0) // HEAD_DIM
    c = lax.broadcasted_iota(jnp.int32, (GROUP_W, GROUP_W), 1) // HEAD_DIM
    return jnp.where(r == c, 1.0 / HEAD_DIM, 0.0).astype(F32)


def _head_rms_norm(o):
    ms = jnp.dot(o * o, _head_mean_matrix(), preferred_element_type=F32, precision=lax.Precision.HIGHEST)
    return o * lax.rsqrt(ms + RMS_EPS)


def _diff_kernel(*refs, latent, lam_init, scale, nl, nc):
    if latent:
        (q_ref, kl_ref, kc_ref, vl_ref, vc_ref, cq_ref, sq_ref, ck_ref, sk_ref, lp_ref, ng_ref,
         o_ref, kt_s, acc_s) = refs
    else:
        q_ref, kc_ref, vc_ref, lp_ref, ng_ref, o_ref, kt_s, acc_s = refs
    half = GROUP_W // 2
    lane = lax.broadcasted_iota(jnp.int32, (1, GROUP_W), 1)

    def rot(x, cos, sin):
        x1, x2 = x[:, :half], x[:, half:]
        return jnp.concatenate([x1 * cos - x2 * sin, x1 * sin + x2 * cos], axis=1)

    def init_keys():
        if latent:
            ck = 512 if nl % 512 == 0 else nl
            for c0 in range(0, nl, ck):
                kr = rot(kl_ref[c0:c0 + ck, :].astype(F32), ck_ref[c0:c0 + ck, :], sk_ref[c0:c0 + ck, :])
                kt_s[:, c0:c0 + ck] = kr.T.astype(BF16)
        off = nl if latent else 0
        kt_s[:, off:off + nc] = kc_ref[...].astype(F32).T.astype(BF16)

    if latent:
        pl.when(pl.program_id(1) == 0)(init_keys)
    else:
        init_keys()

    lp = lp_ref[...]
    lam = (jnp.exp(jnp.sum(lp[0:1] * lp[1:2], axis=1, keepdims=True))
           - jnp.exp(jnp.sum(lp[2:3] * lp[3:4], axis=1, keepdims=True)) + lam_init)
    q = q_ref[...].astype(F32)
    if latent:
        q = rot(q, cq_ref[...], sq_ref[...])
    q = q * (scale * math.log2(math.e))
    acc_s[...] = jnp.zeros_like(acc_s)

    def probs(hc):
        qm = jnp.where((lane % half) // 16 == hc, q, 0.0).astype(BF16)
        s = jnp.dot(qm, kt_s[...], preferred_element_type=F32)
        e = jnp.exp2(s - jnp.max(s, axis=1, keepdims=True))
        return e, jnp.sum(e, axis=1, keepdims=True)

    def body(h, carry):
        e0, l0 = probs(2 * h)
        e1, l1 = probs(2 * h + 1)
        a = (e0 - e1 * (lam * l0 / l1)).astype(BF16)
        if latent:
            r = (jnp.dot(a[:, :nl], vl_ref[...], preferred_element_type=F32)
                 + jnp.dot(a[:, nl:], vc_ref[...], preferred_element_type=F32))
        else:
            r = jnp.dot(a, vc_ref[...], preferred_element_type=F32)
        acc_s[...] += jnp.where(lane // HEAD_DIM == h, r / l0, 0.0)
        return carry

    lax.fori_loop(0, GROUP_HEADS, body, 0)
    y = _head_rms_norm(acc_s[...]) * ng_ref[...] * (1.0 - lam_init)
    o_ref[...] = y.astype(o_ref.dtype)


def diff_attention(p_l, p_c, cos_l, sin_l, lam_params, norm_g, layer_idx, with_ctx_out, tq):
    b, n, _ = p_l.shape
    lc = p_c.shape[1]
    lam_init = 0.8 - 0.6 * math.exp(-0.3 * layer_idx)
    scale = (HEAD_DIM // 2) ** -0.5
    ng = jnp.tile(norm_g, GROUP_HEADS)[None, :]
    gw = GROUP_W
    blk = gw // LANES
    kw = dict(lam_init=lam_init, scale=scale, nl=n, nc=lc)
    y_l = pl.pallas_call(
        functools.partial(_diff_kernel, latent=True, **kw),
        grid=(b, n // tq),
        in_specs=[
            pl.BlockSpec((None, tq, gw), lambda i, j: (i, j, CB_DIFF_Q // blk)),
            pl.BlockSpec((None, n, gw), lambda i, j: (i, 0, CB_DIFF_K // blk)),
            pl.BlockSpec((None, lc, gw), lambda i, j: (i, 0, CB_DIFF_K // blk)),
            pl.BlockSpec((None, n, gw), lambda i, j: (i, 0, CB_DIFF_V // blk)),
            pl.BlockSpec((None, lc, gw), lambda i, j: (i, 0, CB_DIFF_V // blk)),
            pl.BlockSpec((tq, LANES), lambda i, j: (j, 0)),
            pl.BlockSpec((tq, LANES), lambda i, j: (j, 0)),
            pl.BlockSpec((n, LANES), lambda i, j: (0, 0)),
            pl.BlockSpec((n, LANES), lambda i, j: (0, 0)),
            pl.BlockSpec((4, HEAD_DIM // 2), lambda i, j: (0, 0)),
            pl.BlockSpec((1, gw), lambda i, j: (0, 0)),
        ],
        out_specs=pl.BlockSpec((None, tq, gw), lambda i, j: (i, j, 0)),
        out_shape=jax.ShapeDtypeStruct((b, n, gw), BF16),
        scratch_shapes=[pltpu.VMEM((gw, n + lc), BF16), pltpu.VMEM((tq, gw), F32)],
        compiler_params=_cparams("parallel", "arbitrary"),
    )(p_l, p_l, p_c, p_l, p_c, cos_l, sin_l, cos_l, sin_l, lam_params, ng)
    y_c = None
    if with_ctx_out:
        y_c = pl.pallas_call(
            functools.partial(_diff_kernel, latent=False, **kw),
            grid=(b,),
            in_specs=[
                pl.BlockSpec((None, lc, gw), lambda i: (i, 0, CB_DIFF_Q // blk)),
                pl.BlockSpec((None, lc, gw), lambda i: (i, 0, CB_DIFF_K // blk)),
                pl.BlockSpec((None, lc, gw), lambda i: (i, 0, CB_DIFF_V // blk)),
                pl.BlockSpec((4, HEAD_DIM // 2), lambda i: (0, 0)),
                pl.BlockSpec((1, gw), lambda i: (0, 0)),
            ],
            out_specs=pl.BlockSpec((None, lc, gw), lambda i: (i, 0, 0)),
            out_shape=jax.ShapeDtypeStruct((b, lc, gw), BF16),
            scratch_shapes=[pltpu.VMEM((gw, lc), BF16), pltpu.VMEM((lc, gw), F32)],
            compiler_params=_cparams("parallel"),
        )(p_c, p_c, p_c, lam_params, ng)
    return y_l, y_c


def _swa_kernel(*refs, latent, nl, tq, win):
    if latent:
        q_ref, kl_ref, kc_ref, vl_ref, vc_ref, cq_ref, sq_ref, ck_ref, sk_ref, sink_ref, o_ref, kr_s = refs
    else:
        q_ref, kc_ref, vc_ref, sink_ref, o_ref = refs
    hk_w = LANES // 2
    lane = lax.broadcasted_iota(jnp.int32, (1, LANES), 1)
    scale = HEAD_DIM ** -0.5

    def rot(x, cos, sin_signed):
        return x * cos + pltpu.roll(x, hk_w, 1) * sin_signed

    if latent:
        @pl.when(pl.program_id(1) == 0)
        def _():
            kr_s[...] = rot(kl_ref[...].astype(F32), ck_ref[...], sk_ref[...]).astype(BF16)

        j = pl.program_id(1)
        ws = pl.multiple_of(jnp.clip(j * tq - WINDOW, 0, nl - win), WINDOW)
        kwin = kr_s[pl.ds(ws, win), :]
        vwin = vl_ref[pl.ds(ws, win), :]
        qpos = j * tq + lax.broadcasted_iota(jnp.int32, (tq, 1), 0)
        kpos = ws + lax.broadcasted_iota(jnp.int32, (1, win), 1)
        valid = jnp.abs(qpos - kpos) <= WINDOW
    kc = kc_ref[...]
    vc = vc_ref[...]
    nt = (((1,), (1,)), ((), ()))
    for hk in range(2):
        pieces = []
        for g in range(2):
            qg = q_ref[:, g * LANES:(g + 1) * LANES].astype(F32)
            if latent:
                qg = rot(qg, cq_ref[...], sq_ref[...])
            qm = jnp.where((lane % hk_w) // 32 == hk, qg * scale, 0.0).astype(BF16)
            snk = sink_ref[0:1, hk * 2 + g:hk * 2 + g + 1]
            sc = lax.dot_general(qm, kc, nt, preferred_element_type=F32)
            m = jnp.maximum(jnp.max(sc, axis=1, keepdims=True), snk)
            if latent:
                sb = lax.dot_general(qm, kwin, nt, preferred_element_type=F32)
                sb = jnp.where(valid, sb, -jnp.inf)
                m = jnp.maximum(m, jnp.max(sb, axis=1, keepdims=True))
            ec = jnp.exp(sc - m)
            den = jnp.sum(ec, axis=1, keepdims=True) + jnp.exp(snk - m)
            o = jnp.dot(ec.astype(BF16), vc, preferred_element_type=F32)
            if latent:
                eb = jnp.exp(sb - m)
                den = den + jnp.sum(eb, axis=1, keepdims=True)
                o = o + jnp.dot(eb.astype(BF16), vwin, preferred_element_type=F32)
            o = o / den
            if hk != g:
                o = pltpu.roll(o, HEAD_DIM, 1)
            pieces.append(o)
        o_ref[:, hk * LANES:(hk + 1) * LANES] = jnp.where(lane < HEAD_DIM, pieces[0], pieces[1]).astype(o_ref.dtype)


def swa_attention(p_l, p_c, cos_l, sin_l, sink, with_ctx_out, tq):
    b, n, _ = p_l.shape
    lc = p_c.shape[1]
    gw = GROUP_W
    win = tq + 2 * WINDOW
    snk = jnp.zeros((1, LANES), F32).at[0, :GROUP_HEADS].set(sink)
    y_l = pl.pallas_call(
        functools.partial(_swa_kernel, latent=True, nl=n, tq=tq, win=win),
        grid=(b, n // tq),
        in_specs=[
            pl.BlockSpec((None, tq, gw), lambda i, j: (i, j, CB_SWA_Q // 2)),
            pl.BlockSpec((None, n, LANES), lambda i, j: (i, 0, CB_SWA_K)),
            pl.BlockSpec((None, lc, LANES), lambda i, j: (i, 0, CB_SWA_K)),
            pl.BlockSpec((None, n, LANES), lambda i, j: (i, 0, CB_SWA_V)),
            pl.BlockSpec((None, lc, LANES), lambda i, j: (i, 0, CB_SWA_V)),
            pl.BlockSpec((tq, LANES), lambda i, j: (j, 0)),
            pl.BlockSpec((tq, LANES), lambda i, j: (j, 0)),
            pl.BlockSpec((n, LANES), lambda i, j: (0, 0)),
            pl.BlockSpec((n, LANES), lambda i, j: (0, 0)),
            pl.BlockSpec((1, LANES), lambda i, j: (0, 0)),
        ],
        out_specs=pl.BlockSpec((None, tq, gw), lambda i, j: (i, j, 0)),
        out_shape=jax.ShapeDtypeStruct((b, n, gw), BF16),
        scratch_shapes=[pltpu.VMEM((n, LANES), BF16)],
        compiler_params=_cparams("parallel", "arbitrary"),
    )(p_l, p_l, p_c, p_l, p_c, cos_l, sin_l, cos_l, sin_l, snk)
    y_c = None
    if with_ctx_out:
        y_c = pl.pallas_call(
            functools.partial(_swa_kernel, latent=False, nl=n, tq=lc, win=win),
            grid=(b,),
            in_specs=[
                pl.BlockSpec((None, lc, gw), lambda i: (i, 0, CB_SWA_Q // 2)),
                pl.BlockSpec((None, lc, LANES), lambda i: (i, 0, CB_SWA_K)),
                pl.BlockSpec((None, lc, LANES), lambda i: (i, 0, CB_SWA_V)),
                pl.BlockSpec((1, LANES), lambda i: (0, 0)),
            ],
            out_specs=pl.BlockSpec((None, lc, gw), lambda i: (i, 0, 0)),
            out_shape=jax.ShapeDtypeStruct((b, lc, gw), BF16),
            compiler_params=_cparams("parallel"),
        )(p_c, p_c, p_c, snk)
    return y_l, y_c


def _ret_tables(ret_decay):
    c = RET_CHUNK
    lg = jax.nn.log_sigmoid(ret_decay.astype(F32))
    pos = jnp.arange(c, dtype=F32)
    head_k = (np.arange(GROUP_W) % LANES) // (HEAD_DIM // 2)
    head_v = np.arange(GROUP_W) // HEAD_DIM
    qd, kd, dm, cd = [], [], [], []
    for d in range(2):
        l = lg[d]
        qpow = (pos + 1.0) if d == 0 else (c - pos)
        kpow = (c - 1.0 - pos) if d == 0 else pos
        qd.append(jnp.exp(qpow[:, None] * l[head_k][None, :]))
        kd.append(jnp.exp(kpow[:, None] * l[head_k][None, :]))
        rel = pos[:, None] - pos[None, :]
        rel = rel if d == 0 else -rel
        dm.append(jnp.where(rel[None] >= 0, jnp.exp(l[:, None, None] * jnp.maximum(rel, 0.0)[None]), 0.0))
        cd.append(jnp.exp(l * c)[head_v][None, :])
    return jnp.stack(qd), jnp.stack(kd), jnp.stack(dm), jnp.stack(cd)


def _ret_kernel(ql_ref, kl_ref, vl_ref, gl_ref, qc_ref, kc_ref, vc_ref, gc_ref, cos_ref, sin_ref,
                qd_ref, kd_ref, dm_ref, cd_ref, ol_ref, oc_ref, s_s, accl_s, accc_s, *, nl, nc, with_ctx_out):
    c = RET_CHUNK
    half = GROUP_W // 2
    scale = HEAD_DIM ** -0.5
    lane = lax.broadcasted_iota(jnp.int32, (1, GROUP_W), 1)
    head_k = (lane % LANES) // (HEAD_DIM // 2)
    head_v = lane // HEAD_DIM
    row_hk = (lax.broadcasted_iota(jnp.int32, (GROUP_W, 1), 0) % LANES) // (HEAD_DIM // 2)
    same_head = row_hk == head_v
    nt = (((1,), (1,)), ((), ()))

    def rot(x, cos, sin):
        x1, x2 = x[:, :half], x[:, half:]
        return jnp.concatenate([x1 * cos - x2 * sin, x1 * sin + x2 * cos], axis=1)

    def step(d, refs, row, rotate, want_out):
        q_ref, k_ref, v_ref = refs
        q = q_ref[pl.ds(row, c), :].astype(F32)
        k = k_ref[pl.ds(row, c), :].astype(F32)
        v = v_ref[pl.ds(row, c), :]
        if rotate:
            cos, sin = cos_ref[pl.ds(row, c), :], sin_ref[pl.ds(row, c), :]
            q, k = rot(q, cos, sin), rot(k, cos, sin)
        q = q * scale
        s = s_s[d]
        o = None
        if want_out:
            o = _mm(q * qd_ref[d], s, 2, 2)
            for h in range(GROUP_HEADS):
                qm = jnp.where(head_k == h, q, 0.0)
                sc = _mm(qm, k, 1, 1, nt) * dm_ref[d, h]
                o = o + jnp.where(head_v == h, _mm(sc, v, 1, 1), 0.0)
        kv = _mm((k * kd_ref[d]).T, v, 2, 1)
        s_s[d] = s * cd_ref[d] + jnp.where(same_head, kv, 0.0)
        return o

    def finish(o, g_ref, row, o_ref):
        y = _head_rms_norm(o)
        g = g_ref[pl.ds(row, c), :].astype(F32)
        o_ref[pl.ds(row, c), :] = (g * jax.nn.sigmoid(g) * y).astype(o_ref.dtype)

    lat = (ql_ref, kl_ref, vl_ref)
    ctx = (qc_ref, kc_ref, vc_ref)
    ncl, ncc = nl // c, nc // c

    s_s[...] = jnp.zeros_like(s_s)
    for t in range(ncc):
        rf, rb = t * c, (ncc - 1 - t) * c
        of = step(0, ctx, rf, False, with_ctx_out)
        ob = step(1, ctx, rb, False, with_ctx_out)
        if with_ctx_out:
            accc_s[0, rf:rf + c, :] = of
            accc_s[1, rb:rb + c, :] = ob
    if with_ctx_out:
        for t in range(ncc):
            finish(accc_s[0, t * c:(t + 1) * c, :] + accc_s[1, t * c:(t + 1) * c, :], gc_ref, t * c, oc_ref)
    else:
        oc_ref[...] = jnp.zeros_like(oc_ref)

    def both(t, carry):
        rf = pl.multiple_of(t * c, c)
        rb = pl.multiple_of((ncl - 1 - t) * c, c)
        accl_s[0, pl.ds(rf, c), :] = step(0, lat, rf, True, True)
        accl_s[1, pl.ds(rb, c), :] = step(1, lat, rb, True, True)
        return carry

    lax.fori_loop(0, ncl, both, 0)

    def fin(t, carry):
        row = pl.multiple_of(t * c, c)
        finish(accl_s[0, pl.ds(row, c), :] + accl_s[1, pl.ds(row, c), :], gl_ref, row, ol_ref)
        return carry

    lax.fori_loop(0, ncl, fin, 0)


def retention(p_l, p_c, cos_l, sin_l, ret_decay, with_ctx_out):
    b, n, _ = p_l.shape
    lc = p_c.shape[1]
    gw = GROUP_W
    qd, kd, dm, cd = _ret_tables(ret_decay)
    c = RET_CHUNK
    lat = lambda cb: pl.BlockSpec((None, n, gw), lambda i: (i, 0, cb // 2))
    ctx = lambda cb: pl.BlockSpec((None, lc, gw), lambda i: (i, 0, cb // 2))
    full = lambda shape: pl.BlockSpec(shape, lambda i: (0,) * len(shape))
    y_l, y_c = pl.pallas_call(
        functools.partial(_ret_kernel, nl=n, nc=lc, with_ctx_out=with_ctx_out),
        grid=(b,),
        in_specs=[lat(CB_RET_Q), lat(CB_RET_K), lat(CB_RET_V), lat(CB_RET_G),
                  ctx(CB_RET_Q), ctx(CB_RET_K), ctx(CB_RET_V), ctx(CB_RET_G),
                  full((n, LANES)), full((n, LANES)),
                  full((2, c, gw)), full((2, c, gw)), full((2, GROUP_HEADS, c, c)), full((2, 1, gw))],
        out_specs=[pl.BlockSpec((None, n, gw), lambda i: (i, 0, 0)),
                   pl.BlockSpec((None, lc, gw), lambda i: (i, 0, 0))],
        out_shape=[jax.ShapeDtypeStruct((b, n, gw), BF16), jax.ShapeDtypeStruct((b, lc, gw), BF16)],
        scratch_shapes=[pltpu.VMEM((2, gw, gw), F32), pltpu.VMEM((2, n, gw), F32), pltpu.VMEM((2, lc, gw), F32)],
        compiler_params=_cparams("parallel"),
    )(p_l, p_l, p_l, p_l, p_c, p_c, p_c, p_c, cos_l, sin_l, qd, kd, dm, cd)
    return y_l, (y_c if with_ctx_out else None)


def _gdn_prep_kernel(x_ref, w_ref, o_ref):
    j = pl.program_id(1)
    x = x_ref[...].astype(F32)
    n = x.shape[0]
    row = lax.broadcasted_iota(jnp.int32, (n, 1), 0)
    xp = jnp.where(row == 0, 0.0, pltpu.roll(x, 1, 0))
    xn = jnp.where(row == n - 1, 0.0, pltpu.roll(x, n - 1, 0))
    y = xp * w_ref[0:1, :] + x * w_ref[1:2, :] + xn * w_ref[2:3, :]
    y = y * jax.nn.sigmoid(y)
    ss = jnp.dot(y * y, _head_mean_matrix() * HEAD_DIM, preferred_element_type=F32, precision=lax.Precision.HIGHEST)
    nrm = lax.rsqrt(ss + 1e-6) * jnp.where(j == 0, HEAD_DIM ** -0.5, 1.0)
    y = jnp.where(j < 2, y * nrm, y)
    o_ref[...] = y.astype(o_ref.dtype)


def gdn_prep(p, conv_w):
    b, n, _ = p.shape
    gw = GROUP_W
    return pl.pallas_call(
        _gdn_prep_kernel,
        grid=(b, 3),
        in_specs=[pl.BlockSpec((None, n, gw), lambda i, j: (i, 0, CB_GDN_QKV // 2 + j)),
                  pl.BlockSpec((SHORT_CONV, gw), lambda i, j: (0, j))],
        out_specs=pl.BlockSpec((None, None, n, gw), lambda i, j: (i, j, 0, 0)),
        out_shape=jax.ShapeDtypeStruct((b, 3, n, gw), BF16),
        compiler_params=_cparams("parallel", "parallel"),
    )(p, conv_w)


def _gdn_kernel(xl_ref, gl_ref, abl_ref, xc_ref, gc_ref, abc_ref, par_ref, ng_ref, ol_ref, oc_ref,
                s_s, accl_s, accc_s, *, nl, nc, with_ctx_out):
    c = GDN_CHUNK
    gh, gw = GROUP_HEADS, GROUP_W
    hp = lax.Precision.HIGHEST
    lane = lax.broadcasted_iota(jnp.int32, (1, gw), 1)
    head_v = lane // HEAD_DIM
    lane128 = lax.broadcasted_iota(jnp.int32, (1, LANES), 1)
    r_i = lax.broadcasted_iota(jnp.int32, (gw, 1), 0)
    c_i = lax.broadcasted_iota(jnp.int32, (1, gw), 1)
    same_head = (r_i // HEAD_DIM) == (c_i // HEAD_DIM)
    ti = lax.broadcasted_iota(jnp.int32, (c, 1), 0)
    tj = lax.broadcasted_iota(jnp.int32, (1, c), 1)
    nt = (((1,), (1,)), ((), ()))

    def stack(x):
        return jnp.concatenate([jnp.where(head_v == h, x, 0.0) for h in range(gh)], axis=0)

    def unstack(y):
        return y[0:c] + y[c:2 * c] + y[2 * c:3 * c] + y[3 * c:4 * c]

    def by_head(cols):
        out = cols[gh - 1]
        for h in reversed(range(gh - 1)):
            out = jnp.where(head_v == h, cols[h], out)
        return out

    def step(d, refs, row, want_out):
        x_ref, ab_ref = refs
        q = x_ref[0, pl.ds(row, c), :].astype(F32)
        k = x_ref[1, pl.ds(row, c), :].astype(F32)
        v = x_ref[2, pl.ds(row, c), :].astype(F32)
        ab = ab_ref[pl.ds(row, c), :]
        xa = ab + par_ref[1:2, :]
        la = -par_ref[0:1, :] * (jnp.maximum(xa, 0.0) + jnp.log(1.0 + jnp.exp(-jnp.abs(xa))))
        bt = jax.nn.sigmoid(ab)
        tri = jnp.where((ti >= tj) if d == 0 else (ti <= tj), 1.0, 0.0)
        g = jnp.dot(tri, la, preferred_element_type=F32, precision=hp)
        gcols = [jnp.sum(jnp.where(lane128 == d * gh + h, g, 0.0), axis=1, keepdims=True) for h in range(gh)]
        bcols = [jnp.sum(jnp.where(lane128 == 2 * gh + d * gh + h, bt, 0.0), axis=1, keepdims=True) for h in range(gh)]
        gexp, bexp = by_head(gcols), by_head(bcols)
        glast = gexp[c - 1:c, :] if d == 0 else gexp[0:1, :]
        gc_mat = jnp.concatenate([jnp.broadcast_to(gcols[h], (c, gw)) for h in range(gh)], axis=0)
        b_col = jnp.concatenate(bcols, axis=0)
        i_loc, j_loc = r_i % c, c_i % c
        incl = same_head & ((i_loc >= j_loc) if d == 0 else (i_loc <= j_loc))
        strict = same_head & ((i_loc > j_loc) if d == 0 else (i_loc < j_loc))
        decay = jnp.exp(jnp.where(incl, gc_mat - gc_mat.T, -jnp.inf))
        ks = stack(k).astype(BF16)
        kk = lax.dot_general(ks, ks, nt, preferred_element_type=F32)
        a_mat = jnp.where(strict, b_col * kk * decay, 0.0)
        y = jnp.concatenate([stack(bexp * v), stack(bexp * jnp.exp(gexp) * k)], axis=1)
        pair = (i_loc // 2 == j_loc // 2) & (i_loc != j_loc)
        t_inv = jnp.where(r_i == c_i, 1.0, 0.0) - jnp.where(pair, a_mat, 0.0)
        sz = 2
        while sz < c:
            off = (i_loc // (2 * sz) == j_loc // (2 * sz)) & (i_loc // sz != j_loc // sz)
            tb16 = t_inv.astype(BF16)
            ta = jnp.dot(tb16, jnp.where(off, a_mat, 0.0).astype(BF16), preferred_element_type=F32)
            t_inv = t_inv - jnp.dot(ta.astype(BF16), tb16, preferred_element_type=F32)
            sz *= 2
        y = _mm(t_inv, y, 2, 1)
        u_base, w = unstack(y[:, :gw]), unstack(y[:, gw:])
        s = s_s[d]
        sb = s.astype(BF16)
        u = u_base - jnp.dot(w.astype(BF16), sb, preferred_element_type=F32)
        k_tail = k * jnp.exp(glast - gexp)
        pad = jnp.zeros((LANES - c, gw), F32)
        kt_t = jnp.concatenate([k_tail, pad], axis=0).T.astype(BF16)
        u_pad = jnp.concatenate([u, pad], axis=0).astype(BF16)
        s_s[d] = s * jnp.exp(glast) + jnp.where(same_head, jnp.dot(kt_t, u_pad, preferred_element_type=F32), 0.0)
        if not want_out:
            return None
        qs = stack(q).astype(BF16)
        qk = lax.dot_general(qs, ks, nt, preferred_element_type=F32) * decay
        o = jnp.dot((q * jnp.exp(gexp)).astype(BF16), sb, preferred_element_type=F32)
        return o + unstack(_mm(qk, stack(u), 2, 1))

    def finish(o, g_ref, row, o_ref):
        y = _head_rms_norm(o) * ng_ref[...]
        g = g_ref[pl.ds(row, c), :].astype(F32)
        o_ref[pl.ds(row, c), :] = (y * g * jax.nn.sigmoid(g)).astype(o_ref.dtype)

    lat = (xl_ref, abl_ref)
    ctx = (xc_ref, abc_ref)
    ncl, ncc = nl // c, nc // c

    s_s[...] = jnp.zeros_like(s_s)
    for t in range(ncc):
        rf, rb = t * c, (ncc - 1 - t) * c
        of = step(0, ctx, rf, with_ctx_out)
        ob = step(1, ctx, rb, with_ctx_out)
        if with_ctx_out:
            accc_s[0, rf:rf + c, :] = of
            accc_s[1, rb:rb + c, :] = ob
    if with_ctx_out:
        for t in range(ncc):
            finish(accc_s[0, t * c:(t + 1) * c, :] + accc_s[1, t * c:(t + 1) * c, :], gc_ref, t * c, oc_ref)
    else:
        oc_ref[...] = jnp.zeros_like(oc_ref)

    def both(t, carry):
        rf = pl.multiple_of(t * c, c)
        rb = pl.multiple_of((ncl - 1 - t) * c, c)
        accl_s[0, pl.ds(rf, c), :] = step(0, lat, rf, True)
        accl_s[1, pl.ds(rb, c), :] = step(1, lat, rb, True)
        return carry

    lax.fori_loop(0, ncl, both, 0)

    def fin(t, carry):
        row = pl.multiple_of(t * c, c)
        finish(accl_s[0, pl.ds(row, c), :] + accl_s[1, pl.ds(row, c), :], gl_ref, row, ol_ref)
        return carry

    lax.fori_loop(0, ncl, fin, 0)


def gdn(p_l, ab_l, p_c, ab_c, conv_w, a_log, dt_bias, norm_g, with_ctx_out):
    b, n, _ = p_l.shape
    lc = p_c.shape[1]
    gw = GROUP_W
    x_l, x_c = gdn_prep(p_l, conv_w), gdn_prep(p_c, conv_w)
    par = jnp.zeros((8, LANES), F32)
    par = par.at[0, :2 * GROUP_HEADS].set(jnp.exp(a_log.astype(F32)).reshape(-1))
    par = par.at[1, :2 * GROUP_HEADS].set(dt_bias.astype(F32).reshape(-1))
    ng = jnp.tile(norm_g, GROUP_HEADS)[None, :]
    full = lambda shape: pl.BlockSpec(shape, lambda i: (0,) * len(shape))
    y_l, y_c = pl.pallas_call(
        functools.partial(_gdn_kernel, nl=n, nc=lc, with_ctx_out=with_ctx_out),
        grid=(b,),
        in_specs=[pl.BlockSpec((None, 3, n, gw), lambda i: (i, 0, 0, 0)),
                  pl.BlockSpec((None, n, gw), lambda i: (i, 0, CB_GDN_G // 2)),
                  pl.BlockSpec((None, n, LANES), lambda i: (i, 0, 0)),
                  pl.BlockSpec((None, 3, lc, gw), lambda i: (i, 0, 0, 0)),
                  pl.BlockSpec((None, lc, gw), lambda i: (i, 0, CB_GDN_G // 2)),
                  pl.BlockSpec((None, lc, LANES), lambda i: (i, 0, 0)),
                  full((8, LANES)), full((1, gw))],
        out_specs=[pl.BlockSpec((None, n, gw), lambda i: (i, 0, 0)),
                   pl.BlockSpec((None, lc, gw), lambda i: (i, 0, 0))],
        out_shape=[jax.ShapeDtypeStruct((b, n, gw), BF16), jax.ShapeDtypeStruct((b, lc, gw), BF16)],
        scratch_shapes=[pltpu.VMEM((2, gw, gw), F32), pltpu.VMEM((2, n, gw), F32), pltpu.VMEM((2, lc, gw), F32)],
        compiler_params=_cparams("parallel"),
    )(x_l, p_l, ab_l, x_c, p_c, ab_c, par, ng)
    return y_l, (y_c if with_ctx_out else None)


def _layer_norm(x, g, b):
    mu = jnp.mean(x, -1, keepdims=True)
    xc = x - mu
    var = jnp.mean(xc * xc, -1, keepdims=True)
    return xc * lax.rsqrt(var + LN_EPS) * g + b


def _outproj_kernel(y_ref, h_ref, mod_ref, w_ref, ln_ref, rw_ref, hn_ref, u2_ref, aff_ref, *, alpha):
    mix = jnp.dot(y_ref[...], w_ref[...], preferred_element_type=F32)
    hn = _layer_norm(alpha * h_ref[...] + mod_ref[2:3, :] * mix, ln_ref[0:1, :], ln_ref[1:2, :])
    hn_ref[...] = hn
    u2 = hn * (1.0 + mod_ref[4:5, :]) + mod_ref[3:4, :]
    u2_ref[...] = u2.astype(BF16)
    logits = _mm(rw_ref[...], u2, 2, 2, (((1,), (1,)), ((), ())))
    m = jnp.max(logits, 0, keepdims=True)
    e = jnp.exp(logits - m)
    aff_ref[...] = e / jnp.sum(e, 0, keepdims=True)


def outproj(y, h, mod, w_bf16, ln, rw_t_bf16, alpha, tm):
    b, n, d = h.shape
    mixw = y.shape[-1]
    ne = rw_t_bf16.shape[0]
    return pl.pallas_call(
        functools.partial(_outproj_kernel, alpha=alpha),
        grid=(b, n // tm),
        in_specs=[
            pl.BlockSpec((None, tm, mixw), lambda i, j: (i, j, 0)),
            pl.BlockSpec((None, tm, d), lambda i, j: (i, j, 0)),
            pl.BlockSpec((None, 6, d), lambda i, j: (i, 0, 0)),
            pl.BlockSpec((mixw, d), lambda i, j: (0, 0)),
            pl.BlockSpec((2, d), lambda i, j: (0, 0)),
            pl.BlockSpec((ne, d), lambda i, j: (0, 0)),
        ],
        out_specs=[
            pl.BlockSpec((None, tm, d), lambda i, j: (i, j, 0)),
            pl.BlockSpec((None, tm, d), lambda i, j: (i, j, 0)),
            pl.BlockSpec((None, ne, tm), lambda i, j: (i, 0, j)),
        ],
        out_shape=[
            jax.ShapeDtypeStruct((b, n, d), F32),
            jax.ShapeDtypeStruct((b, n, d), BF16),
            jax.ShapeDtypeStruct((b, ne, n), F32),
        ],
        compiler_params=_cparams("parallel", "parallel"),
    )(y, h, mod, w_bf16, ln, rw_t_bf16)


def _ffn_kernel(x_ref, wg_ref, wu_ref, wd_ref, o_ref, acc_s, *, rows):
    f = pl.program_id(2)
    tb, cap, d = x_ref.shape

    @pl.when(f == 0)
    def _():
        acc_s[...] = jnp.zeros_like(acc_s)

    wg = wg_ref[...].astype(BF16)
    wu = wu_ref[...].astype(BF16)
    wd = wd_ref[...].astype(BF16)
    for bi in range(tb):
        for r in range(cap // rows):
            x = x_ref[bi, r * rows:(r + 1) * rows, :]
            g = jnp.dot(x, wg, preferred_element_type=F32)
            u = jnp.dot(x, wu, preferred_element_type=F32)
            hid = (g * jax.nn.sigmoid(g) * u).astype(BF16)
            acc_s[bi, r * rows:(r + 1) * rows, :] += jnp.dot(hid, wd, preferred_element_type=F32)

    @pl.when(f == pl.num_programs(2) - 1)
    def _():
        o_ref[...] = acc_s[...].astype(o_ref.dtype)


def expert_ffn(xin, w_gate, w_up, w_down, layer, tb, tf, rows):
    b, ne, cap, d = xin.shape
    ff = w_gate.shape[-1]
    return pl.pallas_call(
        functools.partial(_ffn_kernel, rows=rows),
        grid=(ne, b // tb, ff // tf),
        in_specs=[
            pl.BlockSpec((tb, None, cap, d), lambda e, i, f: (i, e, 0, 0)),
            pl.BlockSpec((None, None, d, tf), lambda e, i, f: (layer, e, 0, f)),
            pl.BlockSpec((None, None, d, tf), lambda e, i, f: (layer, e, 0, f)),
            pl.BlockSpec((None, None, tf, d), lambda e, i, f: (layer, e, f, 0)),
        ],
        out_specs=pl.BlockSpec((tb, None, cap, d), lambda e, i, f: (i, e, 0, 0)),
        out_shape=jax.ShapeDtypeStruct((b, ne, cap, d), BF16),
        scratch_shapes=[pltpu.VMEM((tb, cap, d), F32)],
        compiler_params=_cparams("parallel", "parallel", "arbitrary"),
    )(xin, w_gate, w_up, w_down)


def _route_kernel(aff_ref, pos_ref, *, cap):
    aff = aff_ref[...]
    ne, n = aff.shape
    bits = pltpu.bitcast(aff, jnp.int32)

    def count(mask):
        return jnp.sum(jnp.where(mask, 1.0, 0.0), axis=1, keepdims=True)

    def search(i, t):
        cand = t | (jnp.int32(1) << (30 - i))
        return jnp.where(count(bits >= cand) >= cap, cand, t)

    thr = lax.fori_loop(0, 31, search, jnp.zeros((ne, 1), jnp.int32))
    gt = bits > thr
    eq = bits == thr
    need = cap - count(gt)

    tile = 256 if n % 256 == 0 else n
    r = lax.broadcasted_iota(jnp.int32, (tile, tile), 0)
    cc = lax.broadcasted_iota(jnp.int32, (tile, tile), 1)
    upper = jnp.where(r <= cc, 1.0, 0.0).astype(BF16)

    def prefix_excl(m):
        outs, carry = [], jnp.zeros((ne, 1), F32)
        for j in range(n // tile):
            mt = m[:, j * tile:(j + 1) * tile]
            inc = jnp.dot(mt.astype(BF16), upper, preferred_element_type=F32) + carry
            outs.append(inc - mt)
            carry = inc[:, tile - 1:tile]
        return jnp.concatenate(outs, axis=1)

    eqf = jnp.where(eq, 1.0, 0.0)
    sel = gt | (eq & (prefix_excl(eqf) < need))
    pos_ref[...] = jnp.where(sel, prefix_excl(jnp.where(sel, 1.0, 0.0)), -1.0)


def route(aff_t, cap):
    b, ne, n = aff_t.shape
    return pl.pallas_call(
        functools.partial(_route_kernel, cap=cap),
        grid=(b,),
        in_specs=[pl.BlockSpec((None, ne, n), lambda i: (i, 0, 0))],
        out_specs=pl.BlockSpec((None, ne, n), lambda i: (i, 0, 0)),
        out_shape=jax.ShapeDtypeStruct((b, ne, n), F32),
        compiler_params=_cparams("parallel"),
    )(aff_t)


def _gather_kernel(pos_ref, u_ref, x_ref):
    cap = x_ref.shape[0]
    slot = lax.broadcasted_iota(jnp.int32, (cap, 1), 0).astype(F32)
    onehot = jnp.where(pos_ref[...] == slot, 1.0, 0.0).astype(BF16)
    x_ref[...] = jnp.dot(onehot, u_ref[...], preferred_element_type=F32).astype(x_ref.dtype)


def gather_tokens(pos, u2, cap):
    b, ne, n = pos.shape
    d = u2.shape[-1]
    return pl.pallas_call(
        _gather_kernel,
        grid=(b, ne),
        in_specs=[pl.BlockSpec((None, None, 1, n), lambda i, e: (i, e, 0, 0)),
                  pl.BlockSpec((None, n, d), lambda i, e: (i, 0, 0))],
        out_specs=pl.BlockSpec((None, None, cap, d), lambda i, e: (i, e, 0, 0)),
        out_shape=jax.ShapeDtypeStruct((b, ne, cap, d), BF16),
        compiler_params=_cparams("parallel", "arbitrary"),
    )(pos.reshape(b, ne, 1, n), u2)


def _scatter_kernel(post_ref, afft_ref, y_ref, f_ref):
    e = pl.program_id(2)
    tn, ne = post_ref.shape
    cap = y_ref.shape[0]

    @pl.when(e == 0)
    def _():
        f_ref[...] = jnp.zeros_like(f_ref)

    lane_e = lax.broadcasted_iota(jnp.int32, (1, ne), 1)
    pos_col = jnp.sum(jnp.where(lane_e == e, post_ref[...], 0.0), axis=1, keepdims=True)
    w_col = jnp.sum(jnp.where(lane_e == e, afft_ref[...], 0.0), axis=1, keepdims=True)
    slot = lax.broadcasted_iota(jnp.int32, (1, cap), 1).astype(F32)
    onehot = jnp.where(pos_col == slot, 1.0, 0.0).astype(BF16)
    f_ref[...] += w_col * jnp.dot(onehot, y_ref[...], preferred_element_type=F32)


def scatter_tokens(pos_t, aff_tm, y, cap, slot0, tn):
    b, n, ne = pos_t.shape
    d = y.shape[3]
    assert slot0 % cap == 0
    return pl.pallas_call(
        _scatter_kernel,
        grid=(b, n // tn, ne),
        in_specs=[pl.BlockSpec((None, tn, ne), lambda i, j, e: (i, j, 0)),
                  pl.BlockSpec((None, tn, ne), lambda i, j, e: (i, j, 0)),
                  pl.BlockSpec((None, None, cap, d), lambda i, j, e: (i, e, slot0 // cap, 0))],
        out_specs=pl.BlockSpec((None, tn, d), lambda i, j, e: (i, j, 0)),
        out_shape=jax.ShapeDtypeStruct((b, n, d), F32),
        compiler_params=_cparams("parallel", "parallel", "arbitrary"),
    )(pos_t, aff_tm, y)


def _ln2_kernel(h_ref, f_ref, mod_ref, ln_ref, o_ref, *, alpha):
    o_ref[...] = _layer_norm(alpha * h_ref[...] + mod_ref[5:6, :] * f_ref[...], ln_ref[0:1, :], ln_ref[1:2, :])


def ffn_residual_norm(h, f, mod, ln, alpha, tm):
    b, n, d = h.shape
    return pl.pallas_call(
        functools.partial(_ln2_kernel, alpha=alpha),
        grid=(b, n // tm),
        in_specs=[pl.BlockSpec((None, tm, d), lambda i, j: (i, j, 0)),
                  pl.BlockSpec((None, tm, d), lambda i, j: (i, j, 0)),
                  pl.BlockSpec((None, 6, d), lambda i, j: (i, 0, 0)),
                  pl.BlockSpec((2, d), lambda i, j: (0, 0))],
        out_specs=pl.BlockSpec((None, tm, d), lambda i, j: (i, j, 0)),
        out_shape=jax.ShapeDtypeStruct((b, n, d), F32),
        compiler_params=_cparams("parallel", "parallel"),
    )(h, f, mod, ln)


def expert_choice_ffn(token_sets, w_gate, w_up, w_down, layer):
    b = token_sets[0][0].shape[0]
    routed = []
    for u2, aff_t in token_sets:
        n, ne = u2.shape[1], aff_t.shape[1]
        cap = CAPACITY_FACTOR * n // ne
        pos = route(aff_t, cap)
        routed.append((pos, gather_tokens(pos, u2, cap), cap))
    xin = routed[0][1] if len(routed) == 1 else jnp.concatenate([r[1] for r in routed], axis=2)
    cap_all = xin.shape[2]
    rows = cap_all // 2 if cap_all >= 512 else cap_all
    tb = math.gcd(b, 4) if cap_all >= 512 else b
    y = expert_ffn(xin, w_gate, w_up, w_down, layer, tb=tb, tf=512, rows=rows)
    outs, start = [], 0
    for (u2, aff_t), (pos, _, cap) in zip(token_sets, routed):
        outs.append(scatter_tokens(jnp.swapaxes(pos, 1, 2), jnp.swapaxes(aff_t, 1, 2), y, cap, start,
                                   tn=min(u2.shape[1], 2048)))
        start += cap
    return outs


def axial_rope_angles(n, rot_dim):
    rows = n // GRID_W
    row = jnp.repeat(jnp.arange(rows, dtype=F32), GRID_W)
    col = jnp.tile(jnp.arange(GRID_W, dtype=F32), rows)
    n_freq = rot_dim // 4
    inv = ROPE_BASE ** (-jnp.arange(n_freq, dtype=F32) / n_freq)
    return jnp.concatenate([row[:, None] * inv, col[:, None] * inv], -1)


def retention_angles(n, dim):
    theta = 1.0 / (ROPE_BASE ** jnp.linspace(0.0, 1.0, dim // 2, dtype=F32))
    return jnp.arange(n, dtype=F32)[:, None] * theta


def _ada_kernel(c_ref, w_ref, b_ref, o_ref):
    c = c_ref[...]
    o_ref[...] = _mm(c * jax.nn.sigmoid(c), w_ref[...], 2, 2) + b_ref[...]


def ada_modulation(cond_rows, ada_w, ada_b, tn):
    depth, d, nw = ada_w.shape
    rows = cond_rows.shape[0]
    return pl.pallas_call(
        _ada_kernel,
        grid=(depth, nw // tn),
        in_specs=[pl.BlockSpec((rows, d), lambda l, j: (0, 0)),
                  pl.BlockSpec((None, d, tn), lambda l, j: (l, 0, j)),
                  pl.BlockSpec((None, 1, tn), lambda l, j: (l, 0, j))],
        out_specs=pl.BlockSpec((None, rows, tn), lambda l, j: (l, 0, j)),
        out_shape=jax.ShapeDtypeStruct((depth, rows, nw), F32),
        compiler_params=_cparams("parallel", "parallel"),
    )(cond_rows, ada_w, ada_b.reshape(depth, 1, nw))


def kernel(x, c, ctx, c_ctx, ada_w, ada_b, w_in, w_out, ret_decay, diff_lambda, diff_norm, gdn_conv,
           gdn_a_log, gdn_dt_bias, gdn_norm, swa_sink, ln_g, ln_b, router_w, w_gate, w_up, w_down):
    b, n, d = x.shape
    lc = ctx.shape[1]
    depth = ada_w.shape[0]
    alpha = (2 * depth) ** 0.25
    main_perm, tail_perm, _ = _in_col_perm()
    ang_diff = axial_rope_angles(n, HEAD_DIM // 2)
    ang_swa = axial_rope_angles(n, HEAD_DIM)
    cos_diff, sin_diff = jnp.tile(jnp.cos(ang_diff), (1, 8)), jnp.tile(jnp.sin(ang_diff), (1, 8))
    cos_swa = jnp.tile(jnp.cos(ang_swa), (1, 4))
    sin_swa = jnp.tile(jnp.sin(ang_swa), (1, 4)) * jnp.where(jnp.arange(LANES) < LANES // 2, -1.0, 1.0)
    ang_ret = retention_angles(n, HEAD_DIM)
    cos_ret, sin_ret = jnp.tile(jnp.cos(ang_ret), (1, 4)), jnp.tile(jnp.sin(ang_ret), (1, 4))
    sub = 8
    rows = -(-(b + 1) // sub) * sub
    cond_rows = jnp.concatenate([c, c_ctx[None, :], jnp.zeros((rows - b - 1, d), F32)], axis=0)
    mods = ada_modulation(cond_rows, ada_w, ada_b, tn=d)
    h, hc = x, ctx
    for layer in range(depth):
        full_ctx = layer < depth - 1
        mod_l = mods[layer, :b].reshape(b, 6, d)
        mod_c = jnp.broadcast_to(mods[layer, b].reshape(1, 6, d), (b, 6, d))
        wl = w_in[layer]
        w_in_b = jnp.concatenate([wl[:, main_perm], wl[:, tail_perm],
                                  jnp.zeros((d, LANES - tail_perm.size), F32)], axis=1).astype(BF16)
        w_out_b = w_out[layer].astype(BF16)
        rw_t = router_w[layer].T
        ln1 = jnp.stack([ln_g[layer, 0], ln_b[layer, 0]])
        ln2 = jnp.stack([ln_g[layer, 1], ln_b[layer, 1]])
        p_l, ab_l = inproj(h, mod_l, w_in_b, tm=512)
        p_c, ab_c = inproj(hc, mod_c, w_in_b, tm=lc)
        ya_l, ya_c = retention(p_l, p_c, cos_ret, sin_ret, ret_decay[layer], full_ctx)
        yb_l, yb_c = diff_attention(p_l, p_c, cos_diff, sin_diff, diff_lambda[layer], diff_norm[layer],
                                    layer, full_ctx, tq=256)
        yc_l, yc_c = gdn(p_l, ab_l, p_c, ab_c, gdn_conv[layer], gdn_a_log[layer], gdn_dt_bias[layer],
                         gdn_norm[layer], full_ctx)
        yd_l, yd_c = swa_attention(p_l, p_c, cos_swa, sin_swa, swa_sink[layer], full_ctx, tq=256)
        y_l = jnp.concatenate([ya_l, yb_l, yc_l, yd_l], -1)
        h, u2, aff = outproj(y_l, h, mod_l, w_out_b, ln1, rw_t, alpha, tm=512)
        token_sets = [(u2, aff)]
        if full_ctx:
            y_c = jnp.concatenate([ya_c, yb_c, yc_c, yd_c], -1)
            hc, u2c, affc = outproj(y_c, hc, mod_c, w_out_b, ln1, rw_t, alpha, tm=lc)
            token_sets.append((u2c, affc))
        f = expert_choice_ffn(token_sets, w_gate, w_up, w_down, layer)
        h = ffn_residual_norm(h, f[0], mod_l, ln2, alpha, tm=512)
        if full_ctx:
            hc = ffn_residual_norm(hc, f[1], mod_c, ln2, alpha, tm=lc)
    return h
```

```python
import functools
import math

import numpy as np
import jax
import jax.numpy as jnp
from jax import lax
from jax.experimental import pallas as pl
from jax.experimental.pallas import tpu as pltpu

F32 = jnp.float32
BF16 = jnp.bfloat16

GRID_W = 64
HEAD_DIM = 64
N_MIXERS = 4
GROUP_HEADS = 4
GROUP_W = GROUP_HEADS * HEAD_DIM
ROPE_BASE = 10000.0
RET_CHUNK = 128
GDN_CHUNK = 64
SHORT_CONV = 3
WINDOW = 128
N_EXPERTS = 16
CAPACITY_FACTOR = 2
LN_EPS = 1e-5
RMS_EPS = 1e-6
LANES = 128

VMEM_LIMIT_BYTES = 56 * 1024 * 1024

CB_RET_Q, CB_RET_K, CB_RET_V, CB_RET_G = 0, 2, 4, 6
CB_DIFF_Q, CB_DIFF_K, CB_DIFF_V = 8, 10, 12
CB_GDN_QKV, CB_GDN_G = 14, 20
CB_SWA_Q, CB_SWA_K, CB_SWA_V = 22, 24, 25
MAIN_W = 26 * LANES


def _cparams(*sem):
    return pltpu.CompilerParams(dimension_semantics=sem, vmem_limit_bytes=VMEM_LIMIT_BYTES)


def _split(x, n):
    if x.dtype == BF16:
        return [x]
    terms, r = [], x
    for _ in range(n):
        t = r.astype(BF16)
        terms.append(t)
        r = r - t.astype(F32)
    return terms


def _mm(a, b, na=1, nb=1, dims=None):
    at, bt = _split(a, na), _split(b, nb)
    out = None
    for i, x in enumerate(at):
        for j, y in enumerate(bt):
            if i + j >= max(len(at), len(bt)):
                continue
            t = (jnp.dot(x, y, preferred_element_type=F32) if dims is None
                 else lax.dot_general(x, y, dims, preferred_element_type=F32))
            out = t if out is None else out + t
    return out


def _in_col_perm():
    gw, gh, hd = GROUP_W, GROUP_HEADS, HEAD_DIM
    off = {}
    o = 0
    for name, w in (('ret_q', gw), ('ret_k', gw), ('ret_v', gw), ('ret_g', gw), ('diff_q', gw), ('diff_k', gw),
                    ('diff_v', gw), ('gdn_qkv', 3 * gw), ('gdn_g', gw), ('gdn_a', 2 * gh), ('gdn_b', 2 * gh),
                    ('swa_q', gw), ('swa_k', gw // 2), ('swa_v', gw // 2)):
        off[name] = o
        o += w
    ident = lambda name, w: off[name] + np.arange(w)
    ret = np.zeros(gw, np.int64)
    for h in range(gh):
        for j in range(hd):
            ret[(j // 32) * 128 + h * 32 + j % 32] = h * hd + j
    dif = np.zeros(gw, np.int64)
    for h in range(gh):
        for c in range(2):
            for j in range(32):
                dif[(j // 16) * 128 + (h * 2 + c) * 16 + j % 16] = h * hd + c * 32 + j
    swq = np.zeros(gw, np.int64)
    for hk in range(2):
        for g in range(2):
            for j in range(hd):
                swq[g * 128 + (j // 32) * 64 + hk * 32 + j % 32] = hk * 128 + g * hd + j
    swk = np.zeros(gw // 2, np.int64)
    for hk in range(2):
        for j in range(hd):
            swk[(j // 32) * 64 + hk * 32 + j % 32] = hk * hd + j
    main = np.concatenate([
        off['ret_q'] + ret, off['ret_k'] + ret, ident('ret_v', gw), ident('ret_g', gw),
        off['diff_q'] + dif, off['diff_k'] + dif, ident('diff_v', gw),
        ident('gdn_qkv', 3 * gw), ident('gdn_g', gw),
        off['swa_q'] + swq, off['swa_k'] + swk, ident('swa_v', gw // 2)])
    tail = np.concatenate([ident('gdn_a', 2 * gh), ident('gdn_b', 2 * gh)])
    return main, tail, off


def _inproj_kernel(h_ref, mod_ref, w_ref, o_ref, ab_ref, *, tn):
    u = (h_ref[...] * (1.0 + mod_ref[1:2, :]) + mod_ref[0:1, :]).astype(BF16)
    nmain = o_ref.shape[-1]
    for j in range(nmain // tn):
        o_ref[:, j * tn:(j + 1) * tn] = jnp.dot(
            u, w_ref[:, j * tn:(j + 1) * tn], preferred_element_type=F32).astype(BF16)
    ab_ref[...] = jnp.dot(u, w_ref[:, nmain:], preferred_element_type=F32)


def inproj(h, mod, w_bf16, tm):
    b, n, d = h.shape
    nw = w_bf16.shape[1]
    return pl.pallas_call(
        functools.partial(_inproj_kernel, tn=256),
        grid=(b, n // tm),
        in_specs=[
            pl.BlockSpec((None, tm, d), lambda i, j: (i, j, 0)),
            pl.BlockSpec((None, 6, d), lambda i, j: (i, 0, 0)),
            pl.BlockSpec((d, nw), lambda i, j: (0, 0)),
        ],
        out_specs=[pl.BlockSpec((None, tm, MAIN_W), lambda i, j: (i, j, 0)),
                   pl.BlockSpec((None, tm, LANES), lambda i, j: (i, j, 0))],
        out_shape=[jax.ShapeDtypeStruct((b, n, MAIN_W), BF16),
                   jax.ShapeDtypeStruct((b, n, LANES), F32)],
        compiler_params=_cparams("parallel", "parallel"),
    )(h, mod, w_bf16)


def _head_mean_matrix():
    r = lax.broadcasted_iota(jnp.int32, (GROUP_W, GROUP_W), 0) // HEAD_DIM
    c = lax.broadcasted_iota(jnp.int32, (GROUP_W, GROUP_W), 1) // HEAD_DIM
    return jnp.where(r == c, 1.0 / HEAD_DIM, 0.0).astype(F32)


def _head_rms_norm(o):
    ms = jnp.dot(o * o, _head_mean_matrix(), preferred_element_type=F32, precision=lax.Precision.HIGHEST)
    return o * lax.rsqrt(ms + RMS_EPS)


def _diff_kernel(*refs, latent, lam_init, scale, nl, nc):
    if latent:
        (q_ref, kl_ref, kc_ref, vl_ref, vc_ref, cq_ref, sq_ref, ck_ref, sk_ref, lp_ref, ng_ref,
         o_ref, kt_s, acc_s) = refs
    else:
        q_ref, kc_ref, vc_ref, lp_ref, ng_ref, o_ref, kt_s, acc_s = refs
    half = GROUP_W // 2
    lane = lax.broadcasted_iota(jnp.int32, (1, GROUP_W), 1)

    def rot(x, cos, sin):
        x1, x2 = x[:, :half], x[:, half:]
        return jnp.concatenate([x1 * cos - x2 * sin, x1 * sin + x2 * cos], axis=1)

    def init_keys():
        if latent:
            ck = 512 if nl % 512 == 0 else nl
            for c0 in range(0, nl, ck):
                kr = rot(kl_ref[c0:c0 + ck, :].astype(F32), ck_ref[c0:c0 + ck, :], sk_ref[c0:c0 + ck, :])
                kt_s[:, c0:c0 + ck] = kr.T.astype(BF16)
        off = nl if latent else 0
        kt_s[:, off:off + nc] = kc_ref[...].astype(F32).T.astype(BF16)

    if latent:
        pl.when(pl.program_id(1) == 0)(init_keys)
    else:
        init_keys()

    lp = lp_ref[...]
    lam = (jnp.exp(jnp.sum(lp[0:1] * lp[1:2], axis=1, keepdims=True))
           - jnp.exp(jnp.sum(lp[2:3] * lp[3:4], axis=1, keepdims=True)) + lam_init)
    q = q_ref[...].astype(F32)
    if latent:
        q = rot(q, cq_ref[...], sq_ref[...])
    q = q * (scale * math.log2(math.e))
    acc_s[...] = jnp.zeros_like(acc_s)

    def probs(hc):
        qm = jnp.where((lane % half) // 16 == hc, q, 0.0).astype(BF16)
        s = jnp.dot(qm, kt_s[...], preferred_element_type=F32)
        e = jnp.exp2(s - jnp.max(s, axis=1, keepdims=True))
        return e, jnp.sum(e, axis=1, keepdims=True)

    def body(h, carry):
        e0, l0 = probs(2 * h)
        e1, l1 = probs(2 * h + 1)
        a = (e0 - e1 * (lam * l0 / l1)).astype(BF16)
        if latent:
            r = (jnp.dot(a[:, :nl], vl_ref[...], preferred_element_type=F32)
                 + jnp.dot(a[:, nl:], vc_ref[...], preferred_element_type=F32))
        else:
            r = jnp.dot(a, vc_ref[...], preferred_element_type=F32)
        acc_s[...] += jnp.where(lane // HEAD_DIM == h, r / l0, 0.0)
        return carry

    lax.fori_loop(0, GROUP_HEADS, body, 0)
    y = _head_rms_norm(acc_s[...]) * ng_ref[...] * (1.0 - lam_init)
    o_ref[...] = y.astype(o_ref.dtype)


def diff_attention(p_l, p_c, cos_l, sin_l, lam_params, norm_g, layer_idx, with_ctx_out, tq):
    b, n, _ = p_l.shape
    lc = p_c.shape[1]
    lam_init = 0.8 - 0.6 * math.exp(-0.3 * layer_idx)
    scale = (HEAD_DIM // 2) ** -0.5
    ng = jnp.tile(norm_g, GROUP_HEADS)[None, :]
    gw = GROUP_W
    blk = gw // LANES
    kw = dict(lam_init=lam_init, scale=scale, nl=n, nc=lc)
    y_l = pl.pallas_call(
        functools.partial(_diff_kernel, latent=True, **kw),
        grid=(b, n // tq),
        in_specs=[
            pl.BlockSpec((None, tq, gw), lambda i, j: (i, j, CB_DIFF_Q // blk)),
            pl.BlockSpec((None, n, gw), lambda i, j: (i, 0, CB_DIFF_K // blk)),
            pl.BlockSpec((None, lc, gw), lambda i, j: (i, 0, CB_DIFF_K // blk)),
            pl.BlockSpec((None, n, gw), lambda i, j: (i, 0, CB_DIFF_V // blk)),
            pl.BlockSpec((None, lc, gw), lambda i, j: (i, 0, CB_DIFF_V // blk)),
            pl.BlockSpec((tq, LANES), lambda i, j: (j, 0)),
            pl.BlockSpec((tq, LANES), lambda i, j: (j, 0)),
            pl.BlockSpec((n, LANES), lambda i, j: (0, 0)),
            pl.BlockSpec((n, LANES), lambda i, j: (0, 0)),
            pl.BlockSpec((4, HEAD_DIM // 2), lambda i, j: (0, 0)),
            pl.BlockSpec((1, gw), lambda i, j: (0, 0)),
        ],
        out_specs=pl.BlockSpec((None, tq, gw), lambda i, j: (i, j, 0)),
        out_shape=jax.ShapeDtypeStruct((b, n, gw), BF16),
        scratch_shapes=[pltpu.VMEM((gw, n + lc), BF16), pltpu.VMEM((tq, gw), F32)],
        compiler_params=_cparams("parallel", "arbitrary"),
    )(p_l, p_l, p_c, p_l, p_c, cos_l, sin_l, cos_l, sin_l, lam_params, ng)
    y_c = None
    if with_ctx_out:
        y_c = pl.pallas_call(
            functools.partial(_diff_kernel, latent=False, **kw),
            grid=(b,),
            in_specs=[
                pl.BlockSpec((None, lc, gw), lambda i: (i, 0, CB_DIFF_Q // blk)),
                pl.BlockSpec((None, lc, gw), lambda i: (i, 0, CB_DIFF_K // blk)),
                pl.BlockSpec((None, lc, gw), lambda i: (i, 0, CB_DIFF_V // blk)),
                pl.BlockSpec((4, HEAD_DIM // 2), lambda i: (0, 0)),
                pl.BlockSpec((1, gw), lambda i: (0, 0)),
            ],
            out_specs=pl.BlockSpec((None, lc, gw), lambda i: (i, 0, 0)),
            out_shape=jax.ShapeDtypeStruct((b, lc, gw), BF16),
            scratch_shapes=[pltpu.VMEM((gw, lc), BF16), pltpu.VMEM((lc, gw), F32)],
            compiler_params=_cparams("parallel"),
        )(p_c, p_c, p_c, lam_params, ng)
    return y_l, y_c


def _swa_kernel(*refs, latent, nl, tq, win):
    if latent:
        q_ref, kl_ref, kc_ref, vl_ref, vc_ref, cq_ref, sq_ref, ck_ref, sk_ref, sink_ref, o_ref, kr_s = refs
    else:
        q_ref, kc_ref, vc_ref, sink_ref, o_ref = refs
    hk_w = LANES // 2
    lane = lax.broadcasted_iota(jnp.int32, (1, LANES), 1)
    scale = HEAD_DIM ** -0.5

    def rot(x, cos, sin_signed):
        return x * cos + pltpu.roll(x, hk_w, 1) * sin_signed

    if latent:
        @pl.when(pl.program_id(1) == 0)
        def _():
            kr_s[...] = rot(kl_ref[...].astype(F32), ck_ref[...], sk_ref[...]).astype(BF16)

        j = pl.program_id(1)
        ws = pl.multiple_of(jnp.clip(j * tq - WINDOW, 0, nl - win), WINDOW)
        kwin = kr_s[pl.ds(ws, win), :]
        vwin = vl_ref[pl.ds(ws, win), :]
        qpos = j * tq + lax.broadcasted_iota(jnp.int32, (tq, 1), 0)
        kpos = ws + lax.broadcasted_iota(jnp.int32, (1, win), 1)
        valid = jnp.abs(qpos - kpos) <= WINDOW
    kc = kc_ref[...]
    vc = vc_ref[...]
    nt = (((1,), (1,)), ((), ()))
    for hk in range(2):
        pieces = []
        for g in range(2):
            qg = q_ref[:, g * LANES:(g + 1) * LANES].astype(F32)
            if latent:
                qg = rot(qg, cq_ref[...], sq_ref[...])
            qm = jnp.where((lane % hk_w) // 32 == hk, qg * scale, 0.0).astype(BF16)
            snk = sink_ref[0:1, hk * 2 + g:hk * 2 + g + 1]
            sc = lax.dot_general(qm, kc, nt, preferred_element_type=F32)
            m = jnp.maximum(jnp.max(sc, axis=1, keepdims=True), snk)
            if latent:
                sb = lax.dot_general(qm, kwin, nt, preferred_element_type=F32)
                sb = jnp.where(valid, sb, -jnp.inf)
                m = jnp.maximum(m, jnp.max(sb, axis=1, keepdims=True))
            ec = jnp.exp(sc - m)
            den = jnp.sum(ec, axis=1, keepdims=True) + jnp.exp(snk - m)
            o = jnp.dot(ec.astype(BF16), vc, preferred_element_type=F32)
            if latent:
                eb = jnp.exp(sb - m)
                den = den + jnp.sum(eb, axis=1, keepdims=True)
                o = o + jnp.dot(eb.astype(BF16), vwin, preferred_element_type=F32)
            o = o / den
            if hk != g:
                o = pltpu.roll(o, HEAD_DIM, 1)
            pieces.append(o)
        o_ref[:, hk * LANES:(hk + 1) * LANES] = jnp.where(lane < HEAD_DIM, pieces[0], pieces[1]).astype(o_ref.dtype)


def swa_attention(p_l, p_c, cos_l, sin_l, sink, with_ctx_out, tq):
    b, n, _ = p_l.shape
    lc = p_c.shape[1]
    gw = GROUP_W
    win = tq + 2 * WINDOW
    snk = jnp.zeros((1, LANES), F32).at[0, :GROUP_HEADS].set(sink)
    y_l = pl.pallas_call(
        functools.partial(_swa_kernel, latent=True, nl=n, tq=tq, win=win),
        grid=(b, n // tq),
        in_specs=[
            pl.BlockSpec((None, tq, gw), lambda i, j: (i, j, CB_SWA_Q // 2)),
            pl.BlockSpec((None, n, LANES), lambda i, j: (i, 0, CB_SWA_K)),
            pl.BlockSpec((None, lc, LANES), lambda i, j: (i, 0, CB_SWA_K)),
            pl.BlockSpec((None, n, LANES), lambda i, j: (i, 0, CB_SWA_V)),
            pl.BlockSpec((None, lc, LANES), lambda i, j: (i, 0, CB_SWA_V)),
            pl.BlockSpec((tq, LANES), lambda i, j: (j, 0)),
            pl.BlockSpec((tq, LANES), lambda i, j: (j, 0)),
            pl.BlockSpec((n, LANES), lambda i, j: (0, 0)),
            pl.BlockSpec((n, LANES), lambda i, j: (0, 0)),
            pl.BlockSpec((1, LANES), lambda i, j: (0, 0)),
        ],
        out_specs=pl.BlockSpec((None, tq, gw), lambda i, j: (i, j, 0)),
        out_shape=jax.ShapeDtypeStruct((b, n, gw), BF16),
        scratch_shapes=[pltpu.VMEM((n, LANES), BF16)],
        compiler_params=_cparams("parallel", "arbitrary"),
    )(p_l, p_l, p_c, p_l, p_c, cos_l, sin_l, cos_l, sin_l, snk)
    y_c = None
    if with_ctx_out:
        y_c = pl.pallas_call(
            functools.partial(_swa_kernel, latent=False, nl=n, tq=lc, win=win),
            grid=(b,),
            in_specs=[
                pl.BlockSpec((None, lc, gw), lambda i: (i, 0, CB_SWA_Q // 2)),
                pl.BlockSpec((None, lc, LANES), lambda i: (i, 0, CB_SWA_K)),
                pl.BlockSpec((None, lc, LANES), lambda i: (i, 0, CB_SWA_V)),
                pl.BlockSpec((1, LANES), lambda i: (0, 0)),
            ],
            out_specs=pl.BlockSpec((None, lc, gw), lambda i: (i, 0, 0)),
            out_shape=jax.ShapeDtypeStruct((b, lc, gw), BF16),
            compiler_params=_cparams("parallel"),
        )(p_c, p_c, p_c, snk)
    return y_l, y_c


def _ret_tables(ret_decay):
    c = RET_CHUNK
    lg = jax.nn.log_sigmoid(ret_decay.astype(F32))
    pos = jnp.arange(c, dtype=F32)
    head_k = (np.arange(GROUP_W) % LANES) // (HEAD_DIM // 2)
    head_v = np.arange(GROUP_W) // HEAD_DIM
    qd, kd, dm, cd = [], [], [], []
    for d in range(2):
        l = lg[d]
        qpow = (pos + 1.0) if d == 0 else (c - pos)
        kpow = (c - 1.0 - pos) if d == 0 else pos
        qd.append(jnp.exp(qpow[:, None] * l[head_k][None, :]))
        kd.append(jnp.exp(kpow[:, None] * l[head_k][None, :]))
        rel = pos[:, None] - pos[None, :]
        rel = rel if d == 0 else -rel
        dm.append(jnp.where(rel[None] >= 0, jnp.exp(l[:, None, None] * jnp.maximum(rel, 0.0)[None]), 0.0))
        cd.append(jnp.exp(l * c)[head_v][None, :])
    return jnp.stack(qd), jnp.stack(kd), jnp.stack(dm), jnp.stack(cd)


def _ret_kernel(ql_ref, kl_ref, vl_ref, gl_ref, qc_ref, kc_ref, vc_ref, gc_ref, cos_ref, sin_ref,
                qd_ref, kd_ref, dm_ref, cd_ref, ol_ref, oc_ref, s_s, accl_s, accc_s, *, nl, nc, with_ctx_out):
    c = RET_CHUNK
    half = GROUP_W // 2
    scale = HEAD_DIM ** -0.5
    lane = lax.broadcasted_iota(jnp.int32, (1, GROUP_W), 1)
    head_k = (lane % LANES) // (HEAD_DIM // 2)
    head_v = lane // HEAD_DIM
    row_hk = (lax.broadcasted_iota(jnp.int32, (GROUP_W, 1), 0) % LANES) // (HEAD_DIM // 2)
    same_head = row_hk == head_v
    nt = (((1,), (1,)), ((), ()))

    def rot(x, cos, sin):
        x1, x2 = x[:, :half], x[:, half:]
        return jnp.concatenate([x1 * cos - x2 * sin, x1 * sin + x2 * cos], axis=1)

    def step(d, refs, row, rotate, want_out):
        q_ref, k_ref, v_ref = refs
        q = q_ref[pl.ds(row, c), :].astype(F32)
        k = k_ref[pl.ds(row, c), :].astype(F32)
        v = v_ref[pl.ds(row, c), :]
        if rotate:
            cos, sin = cos_ref[pl.ds(row, c), :], sin_ref[pl.ds(row, c), :]
            q, k = rot(q, cos, sin), rot(k, cos, sin)
        q = q * scale
        s = s_s[d]
        o = None
        if want_out:
            o = _mm(q * qd_ref[d], s, 2, 2)
            for h in range(GROUP_HEADS):
                qm = jnp.where(head_k == h, q, 0.0)
                sc = _mm(qm, k, 1, 1, nt) * dm_ref[d, h]
                o = o + jnp.where(head_v == h, _mm(sc, v, 1, 1), 0.0)
        kv = _mm((k * kd_ref[d]).T, v, 2, 1)
        s_s[d] = s * cd_ref[d] + jnp.where(same_head, kv, 0.0)
        return o

    def finish(o, g_ref, row, o_ref):
        y = _head_rms_norm(o)
        g = g_ref[pl.ds(row, c), :].astype(F32)
        o_ref[pl.ds(row, c), :] = (g * jax.nn.sigmoid(g) * y).astype(o_ref.dtype)

    lat = (ql_ref, kl_ref, vl_ref)
    ctx = (qc_ref, kc_ref, vc_ref)
    ncl, ncc = nl // c, nc // c

    s_s[...] = jnp.zeros_like(s_s)
    for t in range(ncc):
        rf, rb = t * c, (ncc - 1 - t) * c
        of = step(0, ctx, rf, False, with_ctx_out)
        ob = step(1, ctx, rb, False, with_ctx_out)
        if with_ctx_out:
            accc_s[0, rf:rf + c, :] = of
            accc_s[1, rb:rb + c, :] = ob
    if with_ctx_out:
        for t in range(ncc):
            finish(accc_s[0, t * c:(t + 1) * c, :] + accc_s[1, t * c:(t + 1) * c, :], gc_ref, t * c, oc_ref)
    else:
        oc_ref[...] = jnp.zeros_like(oc_ref)

    def both(t, carry):
        rf = pl.multiple_of(t * c, c)
        rb = pl.multiple_of((ncl - 1 - t) * c, c)
        accl_s[0, pl.ds(rf, c), :] = step(0, lat, rf, True, True)
        accl_s[1, pl.ds(rb, c), :] = step(1, lat, rb, True, True)
        return carry

    lax.fori_loop(0, ncl, both, 0, unroll=2)

    def fin(t, carry):
        row = pl.multiple_of(t * c, c)
        finish(accl_s[0, pl.ds(row, c), :] + accl_s[1, pl.ds(row, c), :], gl_ref, row, ol_ref)
        return carry

    lax.fori_loop(0, ncl, fin, 0)


def retention(p_l, p_c, cos_l, sin_l, ret_decay, with_ctx_out):
    b, n, _ = p_l.shape
    lc = p_c.shape[1]
    gw = GROUP_W
    qd, kd, dm, cd = _ret_tables(ret_decay)
    c = RET_CHUNK
    lat = lambda cb: pl.BlockSpec((None, n, gw), lambda i: (i, 0, cb // 2))
    ctx = lambda cb: pl.BlockSpec((None, lc, gw), lambda i: (i, 0, cb // 2))
    full = lambda shape: pl.BlockSpec(shape, lambda i: (0,) * len(shape))
    y_l, y_c = pl.pallas_call(
        functools.partial(_ret_kernel, nl=n, nc=lc, with_ctx_out=with_ctx_out),
        grid=(b,),
        in_specs=[lat(CB_RET_Q), lat(CB_RET_K), lat(CB_RET_V), lat(CB_RET_G),
                  ctx(CB_RET_Q), ctx(CB_RET_K), ctx(CB_RET_V), ctx(CB_RET_G),
                  full((n, LANES)), full((n, LANES)),
                  full((2, c, gw)), full((2, c, gw)), full((2, GROUP_HEADS, c, c)), full((2, 1, gw))],
        out_specs=[pl.BlockSpec((None, n, gw), lambda i: (i, 0, 0)),
                   pl.BlockSpec((None, lc, gw), lambda i: (i, 0, 0))],
        out_shape=[jax.ShapeDtypeStruct((b, n, gw), BF16), jax.ShapeDtypeStruct((b, lc, gw), BF16)],
        scratch_shapes=[pltpu.VMEM((2, gw, gw), F32), pltpu.VMEM((2, n, gw), F32), pltpu.VMEM((2, lc, gw), F32)],
        compiler_params=_cparams("parallel"),
    )(p_l, p_l, p_l, p_l, p_c, p_c, p_c, p_c, cos_l, sin_l, qd, kd, dm, cd)
    return y_l, (y_c if with_ctx_out else None)


def _gdn_prep_kernel(x_ref, w_ref, o_ref):
    j = pl.program_id(1)
    x = x_ref[...].astype(F32)
    n = x.shape[0]
    row = lax.broadcasted_iota(jnp.int32, (n, 1), 0)
    xp = jnp.where(row == 0, 0.0, pltpu.roll(x, 1, 0))
    xn = jnp.where(row == n - 1, 0.0, pltpu.roll(x, n - 1, 0))
    y = xp * w_ref[0:1, :] + x * w_ref[1:2, :] + xn * w_ref[2:3, :]
    y = y * jax.nn.sigmoid(y)
    ss = jnp.dot(y * y, _head_mean_matrix() * HEAD_DIM, preferred_element_type=F32, precision=lax.Precision.HIGHEST)
    nrm = lax.rsqrt(ss + 1e-6) * jnp.where(j == 0, HEAD_DIM ** -0.5, 1.0)
    y = jnp.where(j < 2, y * nrm, y)
    o_ref[...] = y.astype(o_ref.dtype)


def gdn_prep(p, conv_w):
    b, n, _ = p.shape
    gw = GROUP_W
    return pl.pallas_call(
        _gdn_prep_kernel,
        grid=(b, 3),
        in_specs=[pl.BlockSpec((None, n, gw), lambda i, j: (i, 0, CB_GDN_QKV // 2 + j)),
                  pl.BlockSpec((SHORT_CONV, gw), lambda i, j: (0, j))],
        out_specs=pl.BlockSpec((None, None, n, gw), lambda i, j: (i, j, 0, 0)),
        out_shape=jax.ShapeDtypeStruct((b, 3, n, gw), BF16),
        compiler_params=_cparams("parallel", "parallel"),
    )(p, conv_w)


def _gdn_kernel(xl_ref, gl_ref, abl_ref, xc_ref, gc_ref, abc_ref, par_ref, ng_ref, ol_ref, oc_ref,
                s_s, accl_s, accc_s, *, nl, nc, with_ctx_out):
    c = GDN_CHUNK
    gh, gw = GROUP_HEADS, GROUP_W
    hp = lax.Precision.HIGHEST
    lane = lax.broadcasted_iota(jnp.int32, (1, gw), 1)
    head_v = lane // HEAD_DIM
    lane128 = lax.broadcasted_iota(jnp.int32, (1, LANES), 1)
    r_i = lax.broadcasted_iota(jnp.int32, (gw, 1), 0)
    c_i = lax.broadcasted_iota(jnp.int32, (1, gw), 1)
    same_head = (r_i // HEAD_DIM) == (c_i // HEAD_DIM)
    ti = lax.broadcasted_iota(jnp.int32, (c, 1), 0)
    tj = lax.broadcasted_iota(jnp.int32, (1, c), 1)
    nt = (((1,), (1,)), ((), ()))

    def stack(x):
        return jnp.concatenate([jnp.where(head_v == h, x, 0.0) for h in range(gh)], axis=0)

    def unstack(y):
        return y[0:c] + y[c:2 * c] + y[2 * c:3 * c] + y[3 * c:4 * c]

    def by_head(cols):
        out = cols[gh - 1]
        for h in reversed(range(gh - 1)):
            out = jnp.where(head_v == h, cols[h], out)
        return out

    def step(d, refs, row, want_out):
        x_ref, ab_ref = refs
        q = x_ref[0, pl.ds(row, c), :].astype(F32)
        k = x_ref[1, pl.ds(row, c), :].astype(F32)
        v = x_ref[2, pl.ds(row, c), :].astype(F32)
        ab = ab_ref[pl.ds(row, c), :]
        xa = ab + par_ref[1:2, :]
        la = -par_ref[0:1, :] * (jnp.maximum(xa, 0.0) + jnp.log(1.0 + jnp.exp(-jnp.abs(xa))))
        bt = jax.nn.sigmoid(ab)
        tri = jnp.where((ti >= tj) if d == 0 else (ti <= tj), 1.0, 0.0)
        g = jnp.dot(tri, la, preferred_element_type=F32, precision=hp)
        gcols = [jnp.sum(jnp.where(lane128 == d * gh + h, g, 0.0), axis=1, keepdims=True) for h in range(gh)]
        bcols = [jnp.sum(jnp.where(lane128 == 2 * gh + d * gh + h, bt, 0.0), axis=1, keepdims=True) for h in range(gh)]
        gexp, bexp = by_head(gcols), by_head(bcols)
        glast = gexp[c - 1:c, :] if d == 0 else gexp[0:1, :]
        gc_mat = jnp.concatenate([jnp.broadcast_to(gcols[h], (c, gw)) for h in range(gh)], axis=0)
        b_col = jnp.concatenate(bcols, axis=0)
        i_loc, j_loc = r_i % c, c_i % c
        incl = same_head & ((i_loc >= j_loc) if d == 0 else (i_loc <= j_loc))
        strict = same_head & ((i_loc > j_loc) if d == 0 else (i_loc < j_loc))
        decay = jnp.exp(jnp.where(incl, gc_mat - gc_mat.T, -jnp.inf))
        ks = stack(k).astype(BF16)
        kk = lax.dot_general(ks, ks, nt, preferred_element_type=F32)
        a_mat = jnp.where(strict, b_col * kk * decay, 0.0)
        y = jnp.concatenate([stack(bexp * v), stack(bexp * jnp.exp(gexp) * k)], axis=1)
        pair = (i_loc // 2 == j_loc // 2) & (i_loc != j_loc)
        t_inv = jnp.where(r_i == c_i, 1.0, 0.0) - jnp.where(pair, a_mat, 0.0)
        sz = 2
        while sz < c:
            off = (i_loc // (2 * sz) == j_loc // (2 * sz)) & (i_loc // sz != j_loc // sz)
            tb16 = t_inv.astype(BF16)
            ta = jnp.dot(tb16, jnp.where(off, a_mat, 0.0).astype(BF16), preferred_element_type=F32)
            t_inv = t_inv - jnp.dot(ta.astype(BF16), tb16, preferred_element_type=F32)
            sz *= 2
        y = _mm(t_inv, y, 2, 1)
        u_base, w = unstack(y[:, :gw]), unstack(y[:, gw:])
        s = s_s[d]
        sb = s.astype(BF16)
        u = u_base - jnp.dot(w.astype(BF16), sb, preferred_element_type=F32)
        k_tail = k * jnp.exp(glast - gexp)
        pad = jnp.zeros((LANES - c, gw), F32)
        kt_t = jnp.concatenate([k_tail, pad], axis=0).T.astype(BF16)
        u_pad = jnp.concatenate([u, pad], axis=0).astype(BF16)
        s_s[d] = s * jnp.exp(glast) + jnp.where(same_head, jnp.dot(kt_t, u_pad, preferred_element_type=F32), 0.0)
        if not want_out:
            return None
        qs = stack(q).astype(BF16)
        qk = lax.dot_general(qs, ks, nt, preferred_element_type=F32) * decay
        o = jnp.dot((q * jnp.exp(gexp)).astype(BF16), sb, preferred_element_type=F32)
        return o + unstack(_mm(qk, stack(u), 2, 1))

    def finish(o, g_ref, row, o_ref):
        y = _head_rms_norm(o) * ng_ref[...]
        g = g_ref[pl.ds(row, c), :].astype(F32)
        o_ref[pl.ds(row, c), :] = (y * g * jax.nn.sigmoid(g)).astype(o_ref.dtype)

    lat = (xl_ref, abl_ref)
    ctx = (xc_ref, abc_ref)
    ncl, ncc = nl // c, nc // c

    s_s[...] = jnp.zeros_like(s_s)
    for t in range(ncc):
        rf, rb = t * c, (ncc - 1 - t) * c
        of = step(0, ctx, rf, with_ctx_out)
        ob = step(1, ctx, rb, with_ctx_out)
        if with_ctx_out:
            accc_s[0, rf:rf + c, :] = of
            accc_s[1, rb:rb + c, :] = ob
    if with_ctx_out:
        for t in range(ncc):
            finish(accc_s[0, t * c:(t + 1) * c, :] + accc_s[1, t * c:(t + 1) * c, :], gc_ref, t * c, oc_ref)
    else:
        oc_ref[...] = jnp.zeros_like(oc_ref)

    def both(t, carry):
        rf = pl.multiple_of(t * c, c)
        rb = pl.multiple_of((ncl - 1 - t) * c, c)
        accl_s[0, pl.ds(rf, c), :] = step(0, lat, rf, True)
        accl_s[1, pl.ds(rb, c), :] = step(1, lat, rb, True)
        return carry

    lax.fori_loop(0, ncl, both, 0, unroll=2)

    def fin(t, carry):
        row = pl.multiple_of(t * c, c)
        finish(accl_s[0, pl.ds(row, c), :] + accl_s[1, pl.ds(row, c), :], gl_ref, row, ol_ref)
        return carry

    lax.fori_loop(0, ncl, fin, 0)


def gdn(p_l, ab_l, p_c, ab_c, conv_w, a_log, dt_bias, norm_g, with_ctx_out):
    b, n, _ = p_l.shape
    lc = p_c.shape[1]
    gw = GROUP_W
    x_l, x_c = gdn_prep(p_l, conv_w), gdn_prep(p_c, conv_w)
    par = jnp.zeros((8, LANES), F32)
    par = par.at[0, :2 * GROUP_HEADS].set(jnp.exp(a_log.astype(F32)).reshape(-1))
    par = par.at[1, :2 * GROUP_HEADS].set(dt_bias.astype(F32).reshape(-1))
    ng = jnp.tile(norm_g, GROUP_HEADS)[None, :]
    full = lambda shape: pl.BlockSpec(shape, lambda i: (0,) * len(shape))
    y_l, y_c = pl.pallas_call(
        functools.partial(_gdn_kernel, nl=n, nc=lc, with_ctx_out=with_ctx_out),
        grid=(b,),
        in_specs=[pl.BlockSpec((None, 3, n, gw), lambda i: (i, 0, 0, 0)),
                  pl.BlockSpec((None, n, gw), lambda i: (i, 0, CB_GDN_G // 2)),
                  pl.BlockSpec((None, n, LANES), lambda i: (i, 0, 0)),
                  pl.BlockSpec((None, 3, lc, gw), lambda i: (i, 0, 0, 0)),
                  pl.BlockSpec((None, lc, gw), lambda i: (i, 0, CB_GDN_G // 2)),
                  pl.BlockSpec((None, lc, LANES), lambda i: (i, 0, 0)),
                  full((8, LANES)), full((1, gw))],
        out_specs=[pl.BlockSpec((None, n, gw), lambda i: (i, 0, 0)),
                   pl.BlockSpec((None, lc, gw), lambda i: (i, 0, 0))],
        out_shape=[jax.ShapeDtypeStruct((b, n, gw), BF16), jax.ShapeDtypeStruct((b, lc, gw), BF16)],
        scratch_shapes=[pltpu.VMEM((2, gw, gw), F32), pltpu.VMEM((2, n, gw), F32), pltpu.VMEM((2, lc, gw), F32)],
        compiler_params=_cparams("parallel"),
    )(x_l, p_l, ab_l, x_c, p_c, ab_c, par, ng)
    return y_l, (y_c if with_ctx_out else None)


def _layer_norm(x, g, b):
    mu = jnp.mean(x, -1, keepdims=True)
    xc = x - mu
    var = jnp.mean(xc * xc, -1, keepdims=True)
    return xc * lax.rsqrt(var + LN_EPS) * g + b


def _outproj_kernel(y_ref, h_ref, mod_ref, w_ref, ln_ref, rw_ref, hn_ref, u2_ref, aff_ref, *, alpha):
    mix = jnp.dot(y_ref[...], w_ref[...], preferred_element_type=F32)
    hn = _layer_norm(alpha * h_ref[...] + mod_ref[2:3, :] * mix, ln_ref[0:1, :], ln_ref[1:2, :])
    hn_ref[...] = hn
    u2 = hn * (1.0 + mod_ref[4:5, :]) + mod_ref[3:4, :]
    u2_ref[...] = u2.astype(BF16)
    logits = _mm(rw_ref[...], u2, 2, 2, (((1,), (1,)), ((), ())))
    m = jnp.max(logits, 0, keepdims=True)
    e = jnp.exp(logits - m)
    aff_ref[...] = e / jnp.sum(e, 0, keepdims=True)


def outproj(y, h, mod, w_bf16, ln, rw_t_bf16, alpha, tm):
    b, n, d = h.shape
    mixw = y.shape[-1]
    ne = rw_t_bf16.shape[0]
    return pl.pallas_call(
        functools.partial(_outproj_kernel, alpha=alpha),
        grid=(b, n // tm),
        in_specs=[
            pl.BlockSpec((None, tm, mixw), lambda i, j: (i, j, 0)),
            pl.BlockSpec((None, tm, d), lambda i, j: (i, j, 0)),
            pl.BlockSpec((None, 6, d), lambda i, j: (i, 0, 0)),
            pl.BlockSpec((mixw, d), lambda i, j: (0, 0)),
            pl.BlockSpec((2, d), lambda i, j: (0, 0)),
            pl.BlockSpec((ne, d), lambda i, j: (0, 0)),
        ],
        out_specs=[
            pl.BlockSpec((None, tm, d), lambda i, j: (i, j, 0)),
            pl.BlockSpec((None, tm, d), lambda i, j: (i, j, 0)),
            pl.BlockSpec((None, ne, tm), lambda i, j: (i, 0, j)),
        ],
        out_shape=[
            jax.ShapeDtypeStruct((b, n, d), F32),
            jax.ShapeDtypeStruct((b, n, d), BF16),
            jax.ShapeDtypeStruct((b, ne, n), F32),
        ],
        compiler_params=_cparams("parallel", "parallel"),
    )(y, h, mod, w_bf16, ln, rw_t_bf16)


def _ffn_kernel(x_ref, wg_ref, wu_ref, wd_ref, o_ref, acc_s, *, rows):
    f = pl.program_id(2)
    tb, cap, d = x_ref.shape

    @pl.when(f == 0)
    def _():
        acc_s[...] = jnp.zeros_like(acc_s)

    wg = wg_ref[...].astype(BF16)
    wu = wu_ref[...].astype(BF16)
    wd = wd_ref[...].astype(BF16)
    for bi in range(tb):
        for r in range(cap // rows):
            x = x_ref[bi, r * rows:(r + 1) * rows, :]
            g = jnp.dot(x, wg, preferred_element_type=F32)
            u = jnp.dot(x, wu, preferred_element_type=F32)
            hid = (g * jax.nn.sigmoid(g) * u).astype(BF16)
            acc_s[bi, r * rows:(r + 1) * rows, :] += jnp.dot(hid, wd, preferred_element_type=F32)

    @pl.when(f == pl.num_programs(2) - 1)
    def _():
        o_ref[...] = acc_s[...].astype(o_ref.dtype)


def expert_ffn(xin, w_gate, w_up, w_down, layer, tb, tf, rows):
    b, ne, cap, d = xin.shape
    ff = w_gate.shape[-1]
    return pl.pallas_call(
        functools.partial(_ffn_kernel, rows=rows),
        grid=(ne, b // tb, ff // tf),
        in_specs=[
            pl.BlockSpec((tb, None, cap, d), lambda e, i, f: (i, e, 0, 0)),
            pl.BlockSpec((None, None, d, tf), lambda e, i, f: (layer, e, 0, f)),
            pl.BlockSpec((None, None, d, tf), lambda e, i, f: (layer, e, 0, f)),
            pl.BlockSpec((None, None, tf, d), lambda e, i, f: (layer, e, f, 0)),
        ],
        out_specs=pl.BlockSpec((tb, None, cap, d), lambda e, i, f: (i, e, 0, 0)),
        out_shape=jax.ShapeDtypeStruct((b, ne, cap, d), BF16),
        scratch_shapes=[pltpu.VMEM((tb, cap, d), F32)],
        compiler_params=_cparams("parallel", "parallel", "arbitrary"),
    )(xin, w_gate, w_up, w_down)


def _route_kernel(aff_ref, pos_ref, *, cap):
    aff = aff_ref[...]
    ne, n = aff.shape
    bits = pltpu.bitcast(aff, jnp.int32)

    def count(mask):
        return jnp.sum(jnp.where(mask, 1.0, 0.0), axis=1, keepdims=True)

    def search(i, t):
        cand = t | (jnp.int32(1) << (30 - i))
        return jnp.where(count(bits >= cand) >= cap, cand, t)

    thr = lax.fori_loop(0, 31, search, jnp.zeros((ne, 1), jnp.int32))
    gt = bits > thr
    eq = bits == thr
    need = cap - count(gt)

    tile = 256 if n % 256 == 0 else n
    r = lax.broadcasted_iota(jnp.int32, (tile, tile), 0)
    cc = lax.broadcasted_iota(jnp.int32, (tile, tile), 1)
    upper = jnp.where(r <= cc, 1.0, 0.0).astype(BF16)

    def prefix_excl(m):
        outs, carry = [], jnp.zeros((ne, 1), F32)
        for j in range(n // tile):
            mt = m[:, j * tile:(j + 1) * tile]
            inc = jnp.dot(mt.astype(BF16), upper, preferred_element_type=F32) + carry
            outs.append(inc - mt)
            carry = inc[:, tile - 1:tile]
        return jnp.concatenate(outs, axis=1)

    eqf = jnp.where(eq, 1.0, 0.0)
    sel = gt | (eq & (prefix_excl(eqf) < need))
    pos_ref[...] = jnp.where(sel, prefix_excl(jnp.where(sel, 1.0, 0.0)), -1.0)


def route(aff_t, cap):
    b, ne, n = aff_t.shape
    return pl.pallas_call(
        functools.partial(_route_kernel, cap=cap),
        grid=(b,),
        in_specs=[pl.BlockSpec((None, ne, n), lambda i: (i, 0, 0))],
        out_specs=pl.BlockSpec((None, ne, n), lambda i: (i, 0, 0)),
        out_shape=jax.ShapeDtypeStruct((b, ne, n), F32),
        compiler_params=_cparams("parallel"),
    )(aff_t)


def _gather_kernel(pos_ref, u_ref, x_ref):
    cap = x_ref.shape[0]
    slot = lax.broadcasted_iota(jnp.int32, (cap, 1), 0).astype(F32)
    onehot = jnp.where(pos_ref[...] == slot, 1.0, 0.0).astype(BF16)
    x_ref[...] = jnp.dot(onehot, u_ref[...], preferred_element_type=F32).astype(x_ref.dtype)


def gather_tokens(pos, u2, cap):
    b, ne, n = pos.shape
    d = u2.shape[-1]
    return pl.pallas_call(
        _gather_kernel,
        grid=(b, ne),
        in_specs=[pl.BlockSpec((None, None, 1, n), lambda i, e: (i, e, 0, 0)),
                  pl.BlockSpec((None, n, d), lambda i, e: (i, 0, 0))],
        out_specs=pl.BlockSpec((None, None, cap, d), lambda i, e: (i, e, 0, 0)),
        out_shape=jax.ShapeDtypeStruct((b, ne, cap, d), BF16),
        compiler_params=_cparams("parallel", "arbitrary"),
    )(pos.reshape(b, ne, 1, n), u2)


def _scatter_kernel(post_ref, afft_ref, y_ref, f_ref):
    e = pl.program_id(2)
    tn, ne = post_ref.shape
    cap = y_ref.shape[0]

    @pl.when(e == 0)
    def _():
        f_ref[...] = jnp.zeros_like(f_ref)

    lane_e = lax.broadcasted_iota(jnp.int32, (1, ne), 1)
    pos_col = jnp.sum(jnp.where(lane_e == e, post_ref[...], 0.0), axis=1, keepdims=True)
    w_col = jnp.sum(jnp.where(lane_e == e, afft_ref[...], 0.0), axis=1, keepdims=True)
    slot = lax.broadcasted_iota(jnp.int32, (1, cap), 1).astype(F32)
    onehot = jnp.where(pos_col == slot, 1.0, 0.0).astype(BF16)
    f_ref[...] += w_col * jnp.dot(onehot, y_ref[...], preferred_element_type=F32)


def scatter_tokens(pos_t, aff_tm, y, cap, slot0, tn):
    b, n, ne = pos_t.shape
    d = y.shape[3]
    assert slot0 % cap == 0
    return pl.pallas_call(
        _scatter_kernel,
        grid=(b, n // tn, ne),
        in_specs=[pl.BlockSpec((None, tn, ne), lambda i, j, e: (i, j, 0)),
                  pl.BlockSpec((None, tn, ne), lambda i, j, e: (i, j, 0)),
                  pl.BlockSpec((None, None, cap, d), lambda i, j, e: (i, e, slot0 // cap, 0))],
        out_specs=pl.BlockSpec((None, tn, d), lambda i, j, e: (i, j, 0)),
        out_shape=jax.ShapeDtypeStruct((b, n, d), F32),
        compiler_params=_cparams("parallel", "parallel", "arbitrary"),
    )(pos_t, aff_tm, y)


def _ln2_kernel(h_ref, f_ref, mod_ref, ln_ref, o_ref, *, alpha):
    o_ref[...] = _layer_norm(alpha * h_ref[...] + mod_ref[5:6, :] * f_ref[...], ln_ref[0:1, :], ln_ref[1:2, :])


def ffn_residual_norm(h, f, mod, ln, alpha, tm):
    b, n, d = h.shape
    return pl.pallas_call(
        functools.partial(_ln2_kernel, alpha=alpha),
        grid=(b, n // tm),
        in_specs=[pl.BlockSpec((None, tm, d), lambda i, j: (i, j, 0)),
                  pl.BlockSpec((None, tm, d), lambda i, j: (i, j, 0)),
                  pl.BlockSpec((None, 6, d), lambda i, j: (i, 0, 0)),
                  pl.BlockSpec((2, d), lambda i, j: (0, 0))],
        out_specs=pl.BlockSpec((None, tm, d), lambda i, j: (i, j, 0)),
        out_shape=jax.ShapeDtypeStruct((b, n, d), F32),
        compiler_params=_cparams("parallel", "parallel"),
    )(h, f, mod, ln)


def expert_choice_ffn(token_sets, w_gate, w_up, w_down, layer):
    b = token_sets[0][0].shape[0]
    routed = []
    for u2, aff_t in token_sets:
        n, ne = u2.shape[1], aff_t.shape[1]
        cap = CAPACITY_FACTOR * n // ne
        pos = route(aff_t, cap)
        routed.append((pos, gather_tokens(pos, u2, cap), cap))
    xin = routed[0][1] if len(routed) == 1 else jnp.concatenate([r[1] for r in routed], axis=2)
    cap_all = xin.shape[2]
    rows = cap_all // 2 if cap_all >= 512 else cap_all
    tb = math.gcd(b, 4) if cap_all >= 512 else b
    y = expert_ffn(xin, w_gate, w_up, w_down, layer, tb=tb, tf=512, rows=rows)
    outs, start = [], 0
    for (u2, aff_t), (pos, _, cap) in zip(token_sets, routed):
        outs.append(scatter_tokens(jnp.swapaxes(pos, 1, 2), jnp.swapaxes(aff_t, 1, 2), y, cap, start,
                                   tn=min(u2.shape[1], 2048)))
        start += cap
    return outs


def axial_rope_angles(n, rot_dim):
    rows = n // GRID_W
    row = jnp.repeat(jnp.arange(rows, dtype=F32), GRID_W)
    col = jnp.tile(jnp.arange(GRID_W, dtype=F32), rows)
    n_freq = rot_dim // 4
    inv = ROPE_BASE ** (-jnp.arange(n_freq, dtype=F32) / n_freq)
    return jnp.concatenate([row[:, None] * inv, col[:, None] * inv], -1)


def retention_angles(n, dim):
    theta = 1.0 / (ROPE_BASE ** jnp.linspace(0.0, 1.0, dim // 2, dtype=F32))
    return jnp.arange(n, dtype=F32)[:, None] * theta


def _ada_kernel(c_ref, w_ref, b_ref, o_ref):
    c = c_ref[...]
    o_ref[...] = _mm(c * jax.nn.sigmoid(c), w_ref[...], 2, 2) + b_ref[...]


def ada_modulation(cond_rows, ada_w, ada_b, tn):
    depth, d, nw = ada_w.shape
    rows = cond_rows.shape[0]
    return pl.pallas_call(
        _ada_kernel,
        grid=(depth, nw // tn),
        in_specs=[pl.BlockSpec((rows, d), lambda l, j: (0, 0)),
                  pl.BlockSpec((None, d, tn), lambda l, j: (l, 0, j)),
                  pl.BlockSpec((None, 1, tn), lambda l, j: (l, 0, j))],
        out_specs=pl.BlockSpec((None, rows, tn), lambda l, j: (l, 0, j)),
        out_shape=jax.ShapeDtypeStruct((depth, rows, nw), F32),
        compiler_params=_cparams("parallel", "parallel"),
    )(cond_rows, ada_w, ada_b.reshape(depth, 1, nw))


def kernel(x, c, ctx, c_ctx, ada_w, ada_b, w_in, w_out, ret_decay, diff_lambda, diff_norm, gdn_conv,
           gdn_a_log, gdn_dt_bias, gdn_norm, swa_sink, ln_g, ln_b, router_w, w_gate, w_up, w_down):
    b, n, d = x.shape
    lc = ctx.shape[1]
    depth = ada_w.shape[0]
    alpha = (2 * depth) ** 0.25
    main_perm, tail_perm, _ = _in_col_perm()
    ang_diff = axial_rope_angles(n, HEAD_DIM // 2)
    ang_swa = axial_rope_angles(n, HEAD_DIM)
    cos_diff, sin_diff = jnp.tile(jnp.cos(ang_diff), (1, 8)), jnp.tile(jnp.sin(ang_diff), (1, 8))
    cos_swa = jnp.tile(jnp.cos(ang_swa), (1, 4))
    sin_swa = jnp.tile(jnp.sin(ang_swa), (1, 4)) * jnp.where(jnp.arange(LANES) < LANES // 2, -1.0, 1.0)
    ang_ret = retention_angles(n, HEAD_DIM)
    cos_ret, sin_ret = jnp.tile(jnp.cos(ang_ret), (1, 4)), jnp.tile(jnp.sin(ang_ret), (1, 4))
    sub = 8
    rows = -(-(b + 1) // sub) * sub
    cond_rows = jnp.concatenate([c, c_ctx[None, :], jnp.zeros((rows - b - 1, d), F32)], axis=0)
    mods = ada_modulation(cond_rows, ada_w, ada_b, tn=d)
    h, hc = x, ctx
    for layer in range(depth):
        full_ctx = layer < depth - 1
        mod_l = mods[layer, :b].reshape(b, 6, d)
        mod_c = jnp.broadcast_to(mods[layer, b].reshape(1, 6, d), (b, 6, d))
        wl = w_in[layer]
        w_in_b = jnp.concatenate([wl[:, main_perm], wl[:, tail_perm],
                                  jnp.zeros((d, LANES - tail_perm.size), F32)], axis=1).astype(BF16)
        w_out_b = w_out[layer].astype(BF16)
        rw_t = router_w[layer].T
        ln1 = jnp.stack([ln_g[layer, 0], ln_b[layer, 0]])
        ln2 = jnp.stack([ln_g[layer, 1], ln_b[layer, 1]])
        p_l, ab_l = inproj(h, mod_l, w_in_b, tm=512)
        p_c, ab_c = inproj(hc, mod_c, w_in_b, tm=lc)
        ya_l, ya_c = retention(p_l, p_c, cos_ret, sin_ret, ret_decay[layer], full_ctx)
        yb_l, yb_c = diff_attention(p_l, p_c, cos_diff, sin_diff, diff_lambda[layer], diff_norm[layer],
                                    layer, full_ctx, tq=256)
        yc_l, yc_c = gdn(p_l, ab_l, p_c, ab_c, gdn_conv[layer], gdn_a_log[layer], gdn_dt_bias[layer],
                         gdn_norm[layer], full_ctx)
        yd_l, yd_c = swa_attention(p_l, p_c, cos_swa, sin_swa, swa_sink[layer], full_ctx, tq=256)
        y_l = jnp.concatenate([ya_l, yb_l, yc_l, yd_l], -1)
        h, u2, aff = outproj(y_l, h, mod_l, w_out_b, ln1, rw_t, alpha, tm=512)
        token_sets = [(u2, aff)]
        if full_ctx:
            y_c = jnp.concatenate([ya_c, yb_c, yc_c, yd_c], -1)
            hc, u2c, affc = outproj(y_c, hc, mod_c, w_out_b, ln1, rw_t, alpha, tm=lc)
            token_sets.append((u2c, affc))
        f = expert_choice_ffn(token_sets, w_gate, w_up, w_down, layer)
        h = ffn_residual_norm(h, f[0], mod_l, ln2, alpha, tm=512)
        if full_ctx:
            hc = ffn_residual_norm(hc, f[1], mod_c, ln2, alpha, tm=lc)
    return h
```

```python
import functools
import math

import numpy as np
import jax
import jax.numpy as jnp
from jax import lax
from jax.experimental import pallas as pl
from jax.experimental.pallas import tpu as pltpu

F32 = jnp.float32
BF16 = jnp.bfloat16

GRID_W = 64
HEAD_DIM = 64
N_MIXERS = 4
GROUP_HEADS = 4
GROUP_W = GROUP_HEADS * HEAD_DIM
ROPE_BASE = 10000.0
RET_CHUNK = 128
GDN_CHUNK = 64
SHORT_CONV = 3
WINDOW = 128
N_EXPERTS = 16
CAPACITY_FACTOR = 2
LN_EPS = 1e-5
RMS_EPS = 1e-6
LANES = 128

VMEM_LIMIT_BYTES = 56 * 1024 * 1024

CB_RET_Q, CB_RET_K, CB_RET_V, CB_RET_G = 0, 2, 4, 6
CB_DIFF_Q, CB_DIFF_K, CB_DIFF_V = 8, 10, 12
CB_GDN_QKV, CB_GDN_G = 14, 20
CB_SWA_Q, CB_SWA_K, CB_SWA_V = 22, 24, 25
MAIN_W = 26 * LANES


def _cparams(*sem):
    return pltpu.CompilerParams(dimension_semantics=sem, vmem_limit_bytes=VMEM_LIMIT_BYTES)


def _split(x, n):
    if x.dtype == BF16:
        return [x]
    terms, r = [], x
    for _ in range(n):
        t = r.astype(BF16)
        terms.append(t)
        r = r - t.astype(F32)
    return terms


def _mm(a, b, na=1, nb=1, dims=None):
    at, bt = _split(a, na), _split(b, nb)
    out = None
    for i, x in enumerate(at):
        for j, y in enumerate(bt):
            if i + j >= max(len(at), len(bt)):
                continue
            t = (jnp.dot(x, y, preferred_element_type=F32) if dims is None
                 else lax.dot_general(x, y, dims, preferred_element_type=F32))
            out = t if out is None else out + t
    return out


def _in_col_perm():
    gw, gh, hd = GROUP_W, GROUP_HEADS, HEAD_DIM
    off = {}
    o = 0
    for name, w in (('ret_q', gw), ('ret_k', gw), ('ret_v', gw), ('ret_g', gw), ('diff_q', gw), ('diff_k', gw),
                    ('diff_v', gw), ('gdn_qkv', 3 * gw), ('gdn_g', gw), ('gdn_a', 2 * gh), ('gdn_b', 2 * gh),
                    ('swa_q', gw), ('swa_k', gw // 2), ('swa_v', gw // 2)):
        off[name] = o
        o += w
    ident = lambda name, w: off[name] + np.arange(w)
    ret = np.zeros(gw, np.int64)
    for h in range(gh):
        for j in range(hd):
            ret[(j // 32) * 128 + h * 32 + j % 32] = h * hd + j
    dif = np.zeros(gw, np.int64)
    for h in range(gh):
        for c in range(2):
            for j in range(32):
                dif[(j // 16) * 128 + (h * 2 + c) * 16 + j % 16] = h * hd + c * 32 + j
    swq = np.zeros(gw, np.int64)
    for hk in range(2):
        for g in range(2):
            for j in range(hd):
                swq[g * 128 + (j // 32) * 64 + hk * 32 + j % 32] = hk * 128 + g * hd + j
    swk = np.zeros(gw // 2, np.int64)
    for hk in range(2):
        for j in range(hd):
            swk[(j // 32) * 64 + hk * 32 + j % 32] = hk * hd + j
    main = np.concatenate([
        off['ret_q'] + ret, off['ret_k'] + ret, ident('ret_v', gw), ident('ret_g', gw),
        off['diff_q'] + dif, off['diff_k'] + dif, ident('diff_v', gw),
        ident('gdn_qkv', 3 * gw), ident('gdn_g', gw),
        off['swa_q'] + swq, off['swa_k'] + swk, ident('swa_v', gw // 2)])
    tail = np.concatenate([ident('gdn_a', 2 * gh), ident('gdn_b', 2 * gh)])
    return main, tail, off


def _inproj_kernel(h_ref, mod_ref, w_ref, o_ref, ab_ref, *, tn):
    u = (h_ref[...] * (1.0 + mod_ref[1:2, :]) + mod_ref[0:1, :]).astype(BF16)
    nmain = o_ref.shape[-1]
    for j in range(nmain // tn):
        o_ref[:, j * tn:(j + 1) * tn] = jnp.dot(
            u, w_ref[:, j * tn:(j + 1) * tn], preferred_element_type=F32).astype(BF16)
    ab_ref[...] = jnp.dot(u, w_ref[:, nmain:], preferred_element_type=F32)


def inproj(h, mod, w_bf16, tm):
    b, n, d = h.shape
    nw = w_bf16.shape[1]
    return pl.pallas_call(
        functools.partial(_inproj_kernel, tn=256),
        grid=(b, n // tm),
        in_specs=[
            pl.BlockSpec((None, tm, d), lambda i, j: (i, j, 0)),
            pl.BlockSpec((None, 6, d), lambda i, j: (i, 0, 0)),
            pl.BlockSpec((d, nw), lambda i, j: (0, 0)),
        ],
        out_specs=[pl.BlockSpec((None, tm, MAIN_W), lambda i, j: (i, j, 0)),
                   pl.BlockSpec((None, tm, LANES), lambda i, j: (i, j, 0))],
        out_shape=[jax.ShapeDtypeStruct((b, n, MAIN_W), BF16),
                   jax.ShapeDtypeStruct((b, n, LANES), F32)],
        compiler_params=_cparams("parallel", "parallel"),
    )(h, mod, w_bf16)


def _head_mean_matrix():
    r = lax.broadcasted_iota(jnp.int32, (GROUP_W, GROUP_W), 0) // HEAD_DIM
    c = lax.broadcasted_iota(jnp.int32, (GROUP_W, GROUP_W), 1) // HEAD_DIM
    return jnp.where(r == c, 1.0 / HEAD_DIM, 0.0).astype(F32)


def _head_rms_norm(o):
    ms = jnp.dot(o * o, _head_mean_matrix(), preferred_element_type=F32, precision=lax.Precision.HIGHEST)
    return o * lax.rsqrt(ms + RMS_EPS)


def _diff_kernel(*refs, latent, lam_init, scale, nl, nc):
    if latent:
        (q_ref, kl_ref, kc_ref, vl_ref, vc_ref, cq_ref, sq_ref, ck_ref, sk_ref, lp_ref, ng_ref,
         o_ref, kt_s, acc_s) = refs
    else:
        q_ref, kc_ref, vc_ref, lp_ref, ng_ref, o_ref, kt_s, acc_s = refs
    half = GROUP_W // 2
    lane = lax.broadcasted_iota(jnp.int32, (1, GROUP_W), 1)

    def rot(x, cos, sin):
        x1, x2 = x[:, :half], x[:, half:]
        return jnp.concatenate([x1 * cos - x2 * sin, x1 * sin + x2 * cos], axis=1)

    def init_keys():
        if latent:
            ck = 512 if nl % 512 == 0 else nl
            for c0 in range(0, nl, ck):
                kr = rot(kl_ref[c0:c0 + ck, :].astype(F32), ck_ref[c0:c0 + ck, :], sk_ref[c0:c0 + ck, :])
                kt_s[:, c0:c0 + ck] = kr.T.astype(BF16)
        off = nl if latent else 0
        kt_s[:, off:off + nc] = kc_ref[...].astype(F32).T.astype(BF16)

    if latent:
        pl.when(pl.program_id(1) == 0)(init_keys)
    else:
        init_keys()

    lp = lp_ref[...]
    lam = (jnp.exp(jnp.sum(lp[0:1] * lp[1:2], axis=1, keepdims=True))
           - jnp.exp(jnp.sum(lp[2:3] * lp[3:4], axis=1, keepdims=True)) + lam_init)
    q = q_ref[...].astype(F32)
    if latent:
        q = rot(q, cq_ref[...], sq_ref[...])
    q = q * (scale * math.log2(math.e))
    acc_s[...] = jnp.zeros_like(acc_s)

    def probs(hc):
        qm = jnp.where((lane % half) // 16 == hc, q, 0.0).astype(BF16)
        s = jnp.dot(qm, kt_s[...], preferred_element_type=F32)
        e = jnp.exp2(s - jnp.max(s, axis=1, keepdims=True))
        return e, jnp.sum(e, axis=1, keepdims=True)

    def body(h, carry):
        e0, l0 = probs(2 * h)
        e1, l1 = probs(2 * h + 1)
        a = (e0 - e1 * (lam * l0 / l1)).astype(BF16)
        if latent:
            r = (jnp.dot(a[:, :nl], vl_ref[...], preferred_element_type=F32)
                 + jnp.dot(a[:, nl:], vc_ref[...], preferred_element_type=F32))
        else:
            r = jnp.dot(a, vc_ref[...], preferred_element_type=F32)
        acc_s[...] += jnp.where(lane // HEAD_DIM == h, r / l0, 0.0)
        return carry

    lax.fori_loop(0, GROUP_HEADS, body, 0)
    y = _head_rms_norm(acc_s[...]) * ng_ref[...] * (1.0 - lam_init)
    o_ref[...] = y.astype(o_ref.dtype)


def diff_attention(p_l, p_c, cos_l, sin_l, lam_params, norm_g, layer_idx, with_ctx_out, tq):
    b, n, _ = p_l.shape
    lc = p_c.shape[1]
    lam_init = 0.8 - 0.6 * math.exp(-0.3 * layer_idx)
    scale = (HEAD_DIM // 2) ** -0.5
    ng = jnp.tile(norm_g, GROUP_HEADS)[None, :]
    gw = GROUP_W
    blk = gw // LANES
    kw = dict(lam_init=lam_init, scale=scale, nl=n, nc=lc)
    y_l = pl.pallas_call(
        functools.partial(_diff_kernel, latent=True, **kw),
        grid=(b, n // tq),
        in_specs=[
            pl.BlockSpec((None, tq, gw), lambda i, j: (i, j, CB_DIFF_Q // blk)),
            pl.BlockSpec((None, n, gw), lambda i, j: (i, 0, CB_DIFF_K // blk)),
            pl.BlockSpec((None, lc, gw), lambda i, j: (i, 0, CB_DIFF_K // blk)),
            pl.BlockSpec((None, n, gw), lambda i, j: (i, 0, CB_DIFF_V // blk)),
            pl.BlockSpec((None, lc, gw), lambda i, j: (i, 0, CB_DIFF_V // blk)),
            pl.BlockSpec((tq, LANES), lambda i, j: (j, 0)),
            pl.BlockSpec((tq, LANES), lambda i, j: (j, 0)),
            pl.BlockSpec((n, LANES), lambda i, j: (0, 0)),
            pl.BlockSpec((n, LANES), lambda i, j: (0, 0)),
            pl.BlockSpec((4, HEAD_DIM // 2), lambda i, j: (0, 0)),
            pl.BlockSpec((1, gw), lambda i, j: (0, 0)),
        ],
        out_specs=pl.BlockSpec((None, tq, gw), lambda i, j: (i, j, 0)),
        out_shape=jax.ShapeDtypeStruct((b, n, gw), BF16),
        scratch_shapes=[pltpu.VMEM((gw, n + lc), BF16), pltpu.VMEM((tq, gw), F32)],
        compiler_params=_cparams("parallel", "arbitrary"),
    )(p_l, p_l, p_c, p_l, p_c, cos_l, sin_l, cos_l, sin_l, lam_params, ng)
    y_c = None
    if with_ctx_out:
        y_c = pl.pallas_call(
            functools.partial(_diff_kernel, latent=False, **kw),
            grid=(b,),
            in_specs=[
                pl.BlockSpec((None, lc, gw), lambda i: (i, 0, CB_DIFF_Q // blk)),
                pl.BlockSpec((None, lc, gw), lambda i: (i, 0, CB_DIFF_K // blk)),
                pl.BlockSpec((None, lc, gw), lambda i: (i, 0, CB_DIFF_V // blk)),
                pl.BlockSpec((4, HEAD_DIM // 2), lambda i: (0, 0)),
                pl.BlockSpec((1, gw), lambda i: (0, 0)),
            ],
            out_specs=pl.BlockSpec((None, lc, gw), lambda i: (i, 0, 0)),
            out_shape=jax.ShapeDtypeStruct((b, lc, gw), BF16),
            scratch_shapes=[pltpu.VMEM((gw, lc), BF16), pltpu.VMEM((lc, gw), F32)],
            compiler_params=_cparams("parallel"),
        )(p_c, p_c, p_c, lam_params, ng)
    return y_l, y_c


def _swa_kernel(*refs, latent, nl, tq, win):
    if latent:
        q_ref, kl_ref, kc_ref, vl_ref, vc_ref, cq_ref, sq_ref, ck_ref, sk_ref, sink_ref, o_ref, kr_s = refs
    else:
        q_ref, kc_ref, vc_ref, sink_ref, o_ref = refs
    hk_w = LANES // 2
    lane = lax.broadcasted_iota(jnp.int32, (1, LANES), 1)
    scale = HEAD_DIM ** -0.5

    def rot(x, cos, sin_signed):
        return x * cos + pltpu.roll(x, hk_w, 1) * sin_signed

    if latent:
        @pl.when(pl.program_id(1) == 0)
        def _():
            kr_s[...] = rot(kl_ref[...].astype(F32), ck_ref[...], sk_ref[...]).astype(BF16)

        j = pl.program_id(1)
        ws = pl.multiple_of(jnp.clip(j * tq - WINDOW, 0, nl - win), WINDOW)
        kwin = kr_s[pl.ds(ws, win), :]
        vwin = vl_ref[pl.ds(ws, win), :]
        qpos = j * tq + lax.broadcasted_iota(jnp.int32, (tq, 1), 0)
        kpos = ws + lax.broadcasted_iota(jnp.int32, (1, win), 1)
        valid = jnp.abs(qpos - kpos) <= WINDOW
    kc = kc_ref[...]
    vc = vc_ref[...]
    nt = (((1,), (1,)), ((), ()))
    for hk in range(2):
        pieces = []
        for g in range(2):
            qg = q_ref[:, g * LANES:(g + 1) * LANES].astype(F32)
            if latent:
                qg = rot(qg, cq_ref[...], sq_ref[...])
            qm = jnp.where((lane % hk_w) // 32 == hk, qg * scale, 0.0).astype(BF16)
            snk = sink_ref[0:1, hk * 2 + g:hk * 2 + g + 1]
            sc = lax.dot_general(qm, kc, nt, preferred_element_type=F32)
            m = jnp.maximum(jnp.max(sc, axis=1, keepdims=True), snk)
            if latent:
                sb = lax.dot_general(qm, kwin, nt, preferred_element_type=F32)
                sb = jnp.where(valid, sb, -jnp.inf)
                m = jnp.maximum(m, jnp.max(sb, axis=1, keepdims=True))
            ec = jnp.exp(sc - m)
            den = jnp.sum(ec, axis=1, keepdims=True) + jnp.exp(snk - m)
            o = jnp.dot(ec.astype(BF16), vc, preferred_element_type=F32)
            if latent:
                eb = jnp.exp(sb - m)
                den = den + jnp.sum(eb, axis=1, keepdims=True)
                o = o + jnp.dot(eb.astype(BF16), vwin, preferred_element_type=F32)
            o = o / den
            if hk != g:
                o = pltpu.roll(o, HEAD_DIM, 1)
            pieces.append(o)
        o_ref[:, hk * LANES:(hk + 1) * LANES] = jnp.where(lane < HEAD_DIM, pieces[0], pieces[1]).astype(o_ref.dtype)


def swa_attention(p_l, p_c, cos_l, sin_l, sink, with_ctx_out, tq):
    b, n, _ = p_l.shape
    lc = p_c.shape[1]
    gw = GROUP_W
    win = tq + 2 * WINDOW
    snk = jnp.zeros((1, LANES), F32).at[0, :GROUP_HEADS].set(sink)
    y_l = pl.pallas_call(
        functools.partial(_swa_kernel, latent=True, nl=n, tq=tq, win=win),
        grid=(b, n // tq),
        in_specs=[
            pl.BlockSpec((None, tq, gw), lambda i, j: (i, j, CB_SWA_Q // 2)),
            pl.BlockSpec((None, n, LANES), lambda i, j: (i, 0, CB_SWA_K)),
            pl.BlockSpec((None, lc, LANES), lambda i, j: (i, 0, CB_SWA_K)),
            pl.BlockSpec((None, n, LANES), lambda i, j: (i, 0, CB_SWA_V)),
            pl.BlockSpec((None, lc, LANES), lambda i, j: (i, 0, CB_SWA_V)),
            pl.BlockSpec((tq, LANES), lambda i, j: (j, 0)),
            pl.BlockSpec((tq, LANES), lambda i, j: (j, 0)),
            pl.BlockSpec((n, LANES), lambda i, j: (0, 0)),
            pl.BlockSpec((n, LANES), lambda i, j: (0, 0)),
            pl.BlockSpec((1, LANES), lambda i, j: (0, 0)),
        ],
        out_specs=pl.BlockSpec((None, tq, gw), lambda i, j: (i, j, 0)),
        out_shape=jax.ShapeDtypeStruct((b, n, gw), BF16),
        scratch_shapes=[pltpu.VMEM((n, LANES), BF16)],
        compiler_params=_cparams("parallel", "arbitrary"),
    )(p_l, p_l, p_c, p_l, p_c, cos_l, sin_l, cos_l, sin_l, snk)
    y_c = None
    if with_ctx_out:
        y_c = pl.pallas_call(
            functools.partial(_swa_kernel, latent=False, nl=n, tq=lc, win=win),
            grid=(b,),
            in_specs=[
                pl.BlockSpec((None, lc, gw), lambda i: (i, 0, CB_SWA_Q // 2)),
                pl.BlockSpec((None, lc, LANES), lambda i: (i, 0, CB_SWA_K)),
                pl.BlockSpec((None, lc, LANES), lambda i: (i, 0, CB_SWA_V)),
                pl.BlockSpec((1, LANES), lambda i: (0, 0)),
            ],
            out_specs=pl.BlockSpec((None, lc, gw), lambda i: (i, 0, 0)),
            out_shape=jax.ShapeDtypeStruct((b, lc, gw), BF16),
            compiler_params=_cparams("parallel"),
        )(p_c, p_c, p_c, snk)
    return y_l, y_c


def _ret_tables(ret_decay):
    c = RET_CHUNK
    lg = jax.nn.log_sigmoid(ret_decay.astype(F32))
    pos = jnp.arange(c, dtype=F32)
    head_k = (np.arange(GROUP_W) % LANES) // (HEAD_DIM // 2)
    head_v = np.arange(GROUP_W) // HEAD_DIM
    qd, kd, dm, cd = [], [], [], []
    for d in range(2):
        l = lg[d]
        qpow = (pos + 1.0) if d == 0 else (c - pos)
        kpow = (c - 1.0 - pos) if d == 0 else pos
        qd.append(jnp.exp(qpow[:, None] * l[head_k][None, :]))
        kd.append(jnp.exp(kpow[:, None] * l[head_k][None, :]))
        rel = pos[:, None] - pos[None, :]
        rel = rel if d == 0 else -rel
        dm.append(jnp.where(rel[None] >= 0, jnp.exp(l[:, None, None] * jnp.maximum(rel, 0.0)[None]), 0.0))
        cd.append(jnp.exp(l * c)[head_v][None, :])
    return jnp.stack(qd), jnp.stack(kd), jnp.stack(dm), jnp.stack(cd)


def _ret_kernel(ql_ref, kl_ref, vl_ref, gl_ref, qc_ref, kc_ref, vc_ref, gc_ref, cos_ref, sin_ref,
                qd_ref, kd_ref, dm_ref, cd_ref, ol_ref, oc_ref, s_s, accl_s, accc_s, *, nl, nc, with_ctx_out):
    c = RET_CHUNK
    half = GROUP_W // 2
    scale = HEAD_DIM ** -0.5
    lane = lax.broadcasted_iota(jnp.int32, (1, GROUP_W), 1)
    head_k = (lane % LANES) // (HEAD_DIM // 2)
    head_v = lane // HEAD_DIM
    row_hk = (lax.broadcasted_iota(jnp.int32, (GROUP_W, 1), 0) % LANES) // (HEAD_DIM // 2)
    same_head = row_hk == head_v
    nt = (((1,), (1,)), ((), ()))

    def rot(x, cos, sin):
        x1, x2 = x[:, :half], x[:, half:]
        return jnp.concatenate([x1 * cos - x2 * sin, x1 * sin + x2 * cos], axis=1)

    def step(d, refs, row, rotate, want_out):
        q_ref, k_ref, v_ref = refs
        q = q_ref[pl.ds(row, c), :].astype(F32)
        k = k_ref[pl.ds(row, c), :].astype(F32)
        v = v_ref[pl.ds(row, c), :]
        if rotate:
            cos, sin = cos_ref[pl.ds(row, c), :], sin_ref[pl.ds(row, c), :]
            q, k = rot(q, cos, sin), rot(k, cos, sin)
        q = q * scale
        s = s_s[d]
        o = None
        if want_out:
            o = _mm(q * qd_ref[d], s, 2, 2)
            for h in range(GROUP_HEADS):
                qm = jnp.where(head_k == h, q, 0.0)
                sc = _mm(qm, k, 1, 1, nt) * dm_ref[d, h]
                o = o + jnp.where(head_v == h, _mm(sc, v, 1, 1), 0.0)
        kv = _mm((k * kd_ref[d]).T, v, 2, 1)
        s_s[d] = s * cd_ref[d] + jnp.where(same_head, kv, 0.0)
        return o

    def finish(o, g_ref, row, o_ref):
        y = _head_rms_norm(o)
        g = g_ref[pl.ds(row, c), :].astype(F32)
        o_ref[pl.ds(row, c), :] = (g * jax.nn.sigmoid(g) * y).astype(o_ref.dtype)

    lat = (ql_ref, kl_ref, vl_ref)
    ctx = (qc_ref, kc_ref, vc_ref)
    ncl, ncc = nl // c, nc // c

    s_s[...] = jnp.zeros_like(s_s)
    for t in range(ncc):
        rf, rb = t * c, (ncc - 1 - t) * c
        of = step(0, ctx, rf, False, with_ctx_out)
        ob = step(1, ctx, rb, False, with_ctx_out)
        if with_ctx_out:
            accc_s[0, rf:rf + c, :] = of
            accc_s[1, rb:rb + c, :] = ob
    if with_ctx_out:
        for t in range(ncc):
            finish(accc_s[0, t * c:(t + 1) * c, :] + accc_s[1, t * c:(t + 1) * c, :], gc_ref, t * c, oc_ref)
    else:
        oc_ref[...] = jnp.zeros_like(oc_ref)

    def both(t, carry):
        rf = pl.multiple_of(t * c, c)
        rb = pl.multiple_of((ncl - 1 - t) * c, c)
        accl_s[0, pl.ds(rf, c), :] = step(0, lat, rf, True, True)
        accl_s[1, pl.ds(rb, c), :] = step(1, lat, rb, True, True)
        return carry

    lax.fori_loop(0, ncl, both, 0, unroll=2)

    def fin(t, carry):
        row = pl.multiple_of(t * c, c)
        finish(accl_s[0, pl.ds(row, c), :] + accl_s[1, pl.ds(row, c), :], gl_ref, row, ol_ref)
        return carry

    lax.fori_loop(0, ncl, fin, 0)


def retention(p_l, p_c, cos_l, sin_l, ret_decay, with_ctx_out):
    b, n, _ = p_l.shape
    lc = p_c.shape[1]
    gw = GROUP_W
    qd, kd, dm, cd = _ret_tables(ret_decay)
    c = RET_CHUNK
    lat = lambda cb: pl.BlockSpec((None, n, gw), lambda i: (i, 0, cb // 2))
    ctx = lambda cb: pl.BlockSpec((None, lc, gw), lambda i: (i, 0, cb // 2))
    full = lambda shape: pl.BlockSpec(shape, lambda i: (0,) * len(shape))
    y_l, y_c = pl.pallas_call(
        functools.partial(_ret_kernel, nl=n, nc=lc, with_ctx_out=with_ctx_out),
        grid=(b,),
        in_specs=[lat(CB_RET_Q), lat(CB_RET_K), lat(CB_RET_V), lat(CB_RET_G),
                  ctx(CB_RET_Q), ctx(CB_RET_K), ctx(CB_RET_V), ctx(CB_RET_G),
                  full((n, LANES)), full((n, LANES)),
                  full((2, c, gw)), full((2, c, gw)), full((2, GROUP_HEADS, c, c)), full((2, 1, gw))],
        out_specs=[pl.BlockSpec((None, n, gw), lambda i: (i, 0, 0)),
                   pl.BlockSpec((None, lc, gw), lambda i: (i, 0, 0))],
        out_shape=[jax.ShapeDtypeStruct((b, n, gw), BF16), jax.ShapeDtypeStruct((b, lc, gw), BF16)],
        scratch_shapes=[pltpu.VMEM((2, gw, gw), F32), pltpu.VMEM((2, n, gw), F32), pltpu.VMEM((2, lc, gw), F32)],
        compiler_params=_cparams("parallel"),
    )(p_l, p_l, p_l, p_l, p_c, p_c, p_c, p_c, cos_l, sin_l, qd, kd, dm, cd)
    return y_l, (y_c if with_ctx_out else None)


def _gdn_prep_kernel(x_ref, w_ref, o_ref):
    j = pl.program_id(1)
    x = x_ref[...].astype(F32)
    n = x.shape[0]
    row = lax.broadcasted_iota(jnp.int32, (n, 1), 0)
    xp = jnp.where(row == 0, 0.0, pltpu.roll(x, 1, 0))
    xn = jnp.where(row == n - 1, 0.0, pltpu.roll(x, n - 1, 0))
    y = xp * w_ref[0:1, :] + x * w_ref[1:2, :] + xn * w_ref[2:3, :]
    y = y * jax.nn.sigmoid(y)
    ss = jnp.dot(y * y, _head_mean_matrix() * HEAD_DIM, preferred_element_type=F32, precision=lax.Precision.HIGHEST)
    nrm = lax.rsqrt(ss + 1e-6) * jnp.where(j == 0, HEAD_DIM ** -0.5, 1.0)
    y = jnp.where(j < 2, y * nrm, y)
    o_ref[...] = y.astype(o_ref.dtype)


def gdn_prep(p, conv_w):
    b, n, _ = p.shape
    gw = GROUP_W
    return pl.pallas_call(
        _gdn_prep_kernel,
        grid=(b, 3),
        in_specs=[pl.BlockSpec((None, n, gw), lambda i, j: (i, 0, CB_GDN_QKV // 2 + j)),
                  pl.BlockSpec((SHORT_CONV, gw), lambda i, j: (0, j))],
        out_specs=pl.BlockSpec((None, None, n, gw), lambda i, j: (i, j, 0, 0)),
        out_shape=jax.ShapeDtypeStruct((b, 3, n, gw), BF16),
        compiler_params=_cparams("parallel", "parallel"),
    )(p, conv_w)


def _gdn_kernel(xl_ref, gl_ref, abl_ref, xc_ref, gc_ref, abc_ref, par_ref, ng_ref, ol_ref, oc_ref,
                s_s, accl_s, accc_s, *, nl, nc, with_ctx_out):
    c = GDN_CHUNK
    gh, gw = GROUP_HEADS, GROUP_W
    hp = lax.Precision.HIGHEST
    lane = lax.broadcasted_iota(jnp.int32, (1, gw), 1)
    head_v = lane // HEAD_DIM
    lane128 = lax.broadcasted_iota(jnp.int32, (1, LANES), 1)
    r_i = lax.broadcasted_iota(jnp.int32, (gw, 1), 0)
    c_i = lax.broadcasted_iota(jnp.int32, (1, gw), 1)
    same_head = (r_i // HEAD_DIM) == (c_i // HEAD_DIM)
    ti = lax.broadcasted_iota(jnp.int32, (c, 1), 0)
    tj = lax.broadcasted_iota(jnp.int32, (1, c), 1)
    nt = (((1,), (1,)), ((), ()))

    def stack(x):
        return jnp.concatenate([jnp.where(head_v == h, x, 0.0) for h in range(gh)], axis=0)

    def unstack(y):
        return y[0:c] + y[c:2 * c] + y[2 * c:3 * c] + y[3 * c:4 * c]

    def by_head(cols):
        out = cols[gh - 1]
        for h in reversed(range(gh - 1)):
            out = jnp.where(head_v == h, cols[h], out)
        return out

    def step(d, refs, row, want_out):
        x_ref, ab_ref = refs
        q = x_ref[0, pl.ds(row, c), :].astype(F32)
        k = x_ref[1, pl.ds(row, c), :].astype(F32)
        v = x_ref[2, pl.ds(row, c), :].astype(F32)
        ab = ab_ref[pl.ds(row, c), :]
        xa = ab + par_ref[1:2, :]
        la = -par_ref[0:1, :] * (jnp.maximum(xa, 0.0) + jnp.log(1.0 + jnp.exp(-jnp.abs(xa))))
        bt = jax.nn.sigmoid(ab)
        tri = jnp.where((ti >= tj) if d == 0 else (ti <= tj), 1.0, 0.0)
        g = jnp.dot(tri, la, preferred_element_type=F32, precision=hp)
        gcols = [jnp.sum(jnp.where(lane128 == d * gh + h, g, 0.0), axis=1, keepdims=True) for h in range(gh)]
        bcols = [jnp.sum(jnp.where(lane128 == 2 * gh + d * gh + h, bt, 0.0), axis=1, keepdims=True) for h in range(gh)]
        gexp, bexp = by_head(gcols), by_head(bcols)
        glast = gexp[c - 1:c, :] if d == 0 else gexp[0:1, :]
        gc_mat = jnp.concatenate([jnp.broadcast_to(gcols[h], (c, gw)) for h in range(gh)], axis=0)
        b_col = jnp.concatenate(bcols, axis=0)
        i_loc, j_loc = r_i % c, c_i % c
        incl = same_head & ((i_loc >= j_loc) if d == 0 else (i_loc <= j_loc))
        strict = same_head & ((i_loc > j_loc) if d == 0 else (i_loc < j_loc))
        decay = jnp.exp(jnp.where(incl, gc_mat - gc_mat.T, -jnp.inf))
        ks = stack(k).astype(BF16)
        kk = lax.dot_general(ks, ks, nt, preferred_element_type=F32)
        a_mat = jnp.where(strict, b_col * kk * decay, 0.0)
        y = jnp.concatenate([stack(bexp * v), stack(bexp * jnp.exp(gexp) * k)], axis=1)
        pair = (i_loc // 2 == j_loc // 2) & (i_loc != j_loc)
        t_inv = jnp.where(r_i == c_i, 1.0, 0.0) - jnp.where(pair, a_mat, 0.0)
        sz = 2
        while sz < c:
            off = (i_loc // (2 * sz) == j_loc // (2 * sz)) & (i_loc // sz != j_loc // sz)
            tb16 = t_inv.astype(BF16)
            ta = jnp.dot(tb16, jnp.where(off, a_mat, 0.0).astype(BF16), preferred_element_type=F32)
            t_inv = t_inv - jnp.dot(ta.astype(BF16), tb16, preferred_element_type=F32)
            sz *= 2
        y = _mm(t_inv, y, 2, 1)
        u_base, w = unstack(y[:, :gw]), unstack(y[:, gw:])
        s = s_s[d]
        sb = s.astype(BF16)
        u = u_base - jnp.dot(w.astype(BF16), sb, preferred_element_type=F32)
        k_tail = k * jnp.exp(glast - gexp)
        pad = jnp.zeros((LANES - c, gw), F32)
        kt_t = jnp.concatenate([k_tail, pad], axis=0).T.astype(BF16)
        u_pad = jnp.concatenate([u, pad], axis=0).astype(BF16)
        s_s[d] = s * jnp.exp(glast) + jnp.where(same_head, jnp.dot(kt_t, u_pad, preferred_element_type=F32), 0.0)
        if not want_out:
            return None
        qs = stack(q).astype(BF16)
        qk = lax.dot_general(qs, ks, nt, preferred_element_type=F32) * decay
        o = jnp.dot((q * jnp.exp(gexp)).astype(BF16), sb, preferred_element_type=F32)
        return o + unstack(_mm(qk, stack(u), 2, 1))

    def finish(o, g_ref, row, o_ref):
        y = _head_rms_norm(o) * ng_ref[...]
        g = g_ref[pl.ds(row, c), :].astype(F32)
        o_ref[pl.ds(row, c), :] = (y * g * jax.nn.sigmoid(g)).astype(o_ref.dtype)

    lat = (xl_ref, abl_ref)
    ctx = (xc_ref, abc_ref)
    ncl, ncc = nl // c, nc // c

    s_s[...] = jnp.zeros_like(s_s)
    for t in range(ncc):
        rf, rb = t * c, (ncc - 1 - t) * c
        of = step(0, ctx, rf, with_ctx_out)
        ob = step(1, ctx, rb, with_ctx_out)
        if with_ctx_out:
            accc_s[0, rf:rf + c, :] = of
            accc_s[1, rb:rb + c, :] = ob
    if with_ctx_out:
        for t in range(ncc):
            finish(accc_s[0, t * c:(t + 1) * c, :] + accc_s[1, t * c:(t + 1) * c, :], gc_ref, t * c, oc_ref)
    else:
        oc_ref[...] = jnp.zeros_like(oc_ref)

    def both(t, carry):
        rf = pl.multiple_of(t * c, c)
        rb = pl.multiple_of((ncl - 1 - t) * c, c)
        accl_s[0, pl.ds(rf, c), :] = step(0, lat, rf, True)
        accl_s[1, pl.ds(rb, c), :] = step(1, lat, rb, True)
        return carry

    lax.fori_loop(0, ncl, both, 0, unroll=2)

    def fin(t, carry):
        row = pl.multiple_of(t * c, c)
        finish(accl_s[0, pl.ds(row, c), :] + accl_s[1, pl.ds(row, c), :], gl_ref, row, ol_ref)
        return carry

    lax.fori_loop(0, ncl, fin, 0)


def gdn(p_l, ab_l, p_c, ab_c, conv_w, a_log, dt_bias, norm_g, with_ctx_out):
    b, n, _ = p_l.shape
    lc = p_c.shape[1]
    gw = GROUP_W
    x_l, x_c = gdn_prep(p_l, conv_w), gdn_prep(p_c, conv_w)
    par = jnp.zeros((8, LANES), F32)
    par = par.at[0, :2 * GROUP_HEADS].set(jnp.exp(a_log.astype(F32)).reshape(-1))
    par = par.at[1, :2 * GROUP_HEADS].set(dt_bias.astype(F32).reshape(-1))
    ng = jnp.tile(norm_g, GROUP_HEADS)[None, :]
    full = lambda shape: pl.BlockSpec(shape, lambda i: (0,) * len(shape))
    y_l, y_c = pl.pallas_call(
        functools.partial(_gdn_kernel, nl=n, nc=lc, with_ctx_out=with_ctx_out),
        grid=(b,),
        in_specs=[pl.BlockSpec((None, 3, n, gw), lambda i: (i, 0, 0, 0)),
                  pl.BlockSpec((None, n, gw), lambda i: (i, 0, CB_GDN_G // 2)),
                  pl.BlockSpec((None, n, LANES), lambda i: (i, 0, 0)),
                  pl.BlockSpec((None, 3, lc, gw), lambda i: (i, 0, 0, 0)),
                  pl.BlockSpec((None, lc, gw), lambda i: (i, 0, CB_GDN_G // 2)),
                  pl.BlockSpec((None, lc, LANES), lambda i: (i, 0, 0)),
                  full((8, LANES)), full((1, gw))],
        out_specs=[pl.BlockSpec((None, n, gw), lambda i: (i, 0, 0)),
                   pl.BlockSpec((None, lc, gw), lambda i: (i, 0, 0))],
        out_shape=[jax.ShapeDtypeStruct((b, n, gw), BF16), jax.ShapeDtypeStruct((b, lc, gw), BF16)],
        scratch_shapes=[pltpu.VMEM((2, gw, gw), F32), pltpu.VMEM((2, n, gw), F32), pltpu.VMEM((2, lc, gw), F32)],
        compiler_params=_cparams("parallel"),
    )(x_l, p_l, ab_l, x_c, p_c, ab_c, par, ng)
    return y_l, (y_c if with_ctx_out else None)


def _layer_norm(x, g, b):
    mu = jnp.mean(x, -1, keepdims=True)
    xc = x - mu
    var = jnp.mean(xc * xc, -1, keepdims=True)
    return xc * lax.rsqrt(var + LN_EPS) * g + b


def _outproj_kernel(y_ref, h_ref, mod_ref, w_ref, ln_ref, rw_ref, hn_ref, u2_ref, aff_ref, *, alpha):
    mix = jnp.dot(y_ref[...], w_ref[...], preferred_element_type=F32)
    hn = _layer_norm(alpha * h_ref[...] + mod_ref[2:3, :] * mix, ln_ref[0:1, :], ln_ref[1:2, :])
    hn_ref[...] = hn
    u2 = hn * (1.0 + mod_ref[4:5, :]) + mod_ref[3:4, :]
    u2_ref[...] = u2.astype(BF16)
    logits = _mm(rw_ref[...], u2, 2, 2, (((1,), (1,)), ((), ())))
    m = jnp.max(logits, 0, keepdims=True)
    e = jnp.exp(logits - m)
    aff_ref[...] = e / jnp.sum(e, 0, keepdims=True)


def outproj(y, h, mod, w_bf16, ln, rw_t_bf16, alpha, tm):
    b, n, d = h.shape
    mixw = y.shape[-1]
    ne = rw_t_bf16.shape[0]
    return pl.pallas_call(
        functools.partial(_outproj_kernel, alpha=alpha),
        grid=(b, n // tm),
        in_specs=[
            pl.BlockSpec((None, tm, mixw), lambda i, j: (i, j, 0)),
            pl.BlockSpec((None, tm, d), lambda i, j: (i, j, 0)),
            pl.BlockSpec((None, 6, d), lambda i, j: (i, 0, 0)),
            pl.BlockSpec((mixw, d), lambda i, j: (0, 0)),
            pl.BlockSpec((2, d), lambda i, j: (0, 0)),
            pl.BlockSpec((ne, d), lambda i, j: (0, 0)),
        ],
        out_specs=[
            pl.BlockSpec((None, tm, d), lambda i, j: (i, j, 0)),
            pl.BlockSpec((None, tm, d), lambda i, j: (i, j, 0)),
            pl.BlockSpec((None, ne, tm), lambda i, j: (i, 0, j)),
        ],
        out_shape=[
            jax.ShapeDtypeStruct((b, n, d), F32),
            jax.ShapeDtypeStruct((b, n, d), BF16),
            jax.ShapeDtypeStruct((b, ne, n), F32),
        ],
        compiler_params=_cparams("parallel", "parallel"),
    )(y, h, mod, w_bf16, ln, rw_t_bf16)


def _ffn_kernel(x_ref, wg_ref, wu_ref, wd_ref, o_ref, acc_s, *, rows):
    f = pl.program_id(2)
    tb, cap, d = x_ref.shape

    @pl.when(f == 0)
    def _():
        acc_s[...] = jnp.zeros_like(acc_s)

    wg = wg_ref[...].astype(BF16)
    wu = wu_ref[...].astype(BF16)
    wd = wd_ref[...].astype(BF16)
    for bi in range(tb):
        for r in range(cap // rows):
            x = x_ref[bi, r * rows:(r + 1) * rows, :]
            g = jnp.dot(x, wg, preferred_element_type=F32)
            u = jnp.dot(x, wu, preferred_element_type=F32)
            hid = (g * jax.nn.sigmoid(g) * u).astype(BF16)
            acc_s[bi, r * rows:(r + 1) * rows, :] += jnp.dot(hid, wd, preferred_element_type=F32)

    @pl.when(f == pl.num_programs(2) - 1)
    def _():
        o_ref[...] = acc_s[...].astype(o_ref.dtype)


def expert_ffn(xin, w_gate, w_up, w_down, layer, tb, tf, rows):
    b, ne, cap, d = xin.shape
    ff = w_gate.shape[-1]
    return pl.pallas_call(
        functools.partial(_ffn_kernel, rows=rows),
        grid=(ne, b // tb, ff // tf),
        in_specs=[
            pl.BlockSpec((tb, None, cap, d), lambda e, i, f: (i, e, 0, 0)),
            pl.BlockSpec((None, None, d, tf), lambda e, i, f: (layer, e, 0, f)),
            pl.BlockSpec((None, None, d, tf), lambda e, i, f: (layer, e, 0, f)),
            pl.BlockSpec((None, None, tf, d), lambda e, i, f: (layer, e, f, 0)),
        ],
        out_specs=pl.BlockSpec((tb, None, cap, d), lambda e, i, f: (i, e, 0, 0)),
        out_shape=jax.ShapeDtypeStruct((b, ne, cap, d), BF16),
        scratch_shapes=[pltpu.VMEM((tb, cap, d), F32)],
        compiler_params=_cparams("parallel", "parallel", "arbitrary"),
    )(xin, w_gate, w_up, w_down)


def _route_kernel(aff_ref, pos_ref, *, cap):
    aff = aff_ref[...]
    ne, n = aff.shape
    bits = pltpu.bitcast(aff, jnp.int32)

    def count(mask):
        return jnp.sum(jnp.where(mask, 1.0, 0.0), axis=1, keepdims=True)

    def search(i, t):
        cand = t | (jnp.int32(1) << (30 - i))
        return jnp.where(count(bits >= cand) >= cap, cand, t)

    thr = lax.fori_loop(0, 31, search, jnp.zeros((ne, 1), jnp.int32))
    gt = bits > thr
    eq = bits == thr
    need = cap - count(gt)

    tile = 256 if n % 256 == 0 else n
    r = lax.broadcasted_iota(jnp.int32, (tile, tile), 0)
    cc = lax.broadcasted_iota(jnp.int32, (tile, tile), 1)
    upper = jnp.where(r <= cc, 1.0, 0.0).astype(BF16)

    def prefix_excl(m):
        outs, carry = [], jnp.zeros((ne, 1), F32)
        for j in range(n // tile):
            mt = m[:, j * tile:(j + 1) * tile]
            inc = jnp.dot(mt.astype(BF16), upper, preferred_element_type=F32) + carry
            outs.append(inc - mt)
            carry = inc[:, tile - 1:tile]
        return jnp.concatenate(outs, axis=1)

    eqf = jnp.where(eq, 1.0, 0.0)
    sel = gt | (eq & (prefix_excl(eqf) < need))
    pos_ref[...] = jnp.where(sel, prefix_excl(jnp.where(sel, 1.0, 0.0)), -1.0)


def route(aff_t, cap):
    b, ne, n = aff_t.shape
    return pl.pallas_call(
        functools.partial(_route_kernel, cap=cap),
        grid=(b,),
        in_specs=[pl.BlockSpec((None, ne, n), lambda i: (i, 0, 0))],
        out_specs=pl.BlockSpec((None, ne, n), lambda i: (i, 0, 0)),
        out_shape=jax.ShapeDtypeStruct((b, ne, n), F32),
        compiler_params=_cparams("parallel"),
    )(aff_t)


def _gather_kernel(pos_ref, u_ref, x_ref):
    cap = x_ref.shape[0]
    slot = lax.broadcasted_iota(jnp.int32, (cap, 1), 0).astype(F32)
    onehot = jnp.where(pos_ref[...] == slot, 1.0, 0.0).astype(BF16)
    x_ref[...] = jnp.dot(onehot, u_ref[...], preferred_element_type=F32).astype(x_ref.dtype)


def gather_tokens(pos, u2, cap):
    b, ne, n = pos.shape
    d = u2.shape[-1]
    return pl.pallas_call(
        _gather_kernel,
        grid=(b, ne),
        in_specs=[pl.BlockSpec((None, None, 1, n), lambda i, e: (i, e, 0, 0)),
                  pl.BlockSpec((None, n, d), lambda i, e: (i, 0, 0))],
        out_specs=pl.BlockSpec((None, None, cap, d), lambda i, e: (i, e, 0, 0)),
        out_shape=jax.ShapeDtypeStruct((b, ne, cap, d), BF16),
        compiler_params=_cparams("parallel", "arbitrary"),
    )(pos.reshape(b, ne, 1, n), u2)


def _scatter_kernel(post_ref, afft_ref, y_ref, h_ref, mod_ref, ln_ref, f_ref, *, alpha, rows):
    e = pl.program_id(2)
    tn, ne = post_ref.shape
    cap = y_ref.shape[0]

    @pl.when(e == 0)
    def _():
        f_ref[...] = jnp.zeros_like(f_ref)

    lane_e = lax.broadcasted_iota(jnp.int32, (1, ne), 1)
    pos_col = jnp.sum(jnp.where(lane_e == e, post_ref[...], 0.0), axis=1, keepdims=True)
    w_col = jnp.sum(jnp.where(lane_e == e, afft_ref[...], 0.0), axis=1, keepdims=True)
    slot = lax.broadcasted_iota(jnp.int32, (1, cap), 1).astype(F32)
    onehot = jnp.where(pos_col == slot, 1.0, 0.0).astype(BF16)
    f_ref[...] += w_col * jnp.dot(onehot, y_ref[...], preferred_element_type=F32)

    @pl.when(e == pl.num_programs(2) - 1)
    def _():
        for r0 in range(0, tn, rows):
            x = alpha * h_ref[r0:r0 + rows, :] + mod_ref[5:6, :] * f_ref[r0:r0 + rows, :]
            f_ref[r0:r0 + rows, :] = _layer_norm(x, ln_ref[0:1, :], ln_ref[1:2, :])


def scatter_tokens(pos_t, aff_tm, y, cap, slot0, h, mod, ln, alpha, tn):
    b, n, ne = pos_t.shape
    d = y.shape[3]
    assert slot0 % cap == 0
    return pl.pallas_call(
        functools.partial(_scatter_kernel, alpha=alpha, rows=min(tn, 256)),
        grid=(b, n // tn, ne),
        in_specs=[pl.BlockSpec((None, tn, ne), lambda i, j, e: (i, j, 0)),
                  pl.BlockSpec((None, tn, ne), lambda i, j, e: (i, j, 0)),
                  pl.BlockSpec((None, None, cap, d), lambda i, j, e: (i, e, slot0 // cap, 0)),
                  pl.BlockSpec((None, tn, d), lambda i, j, e: (i, j, 0)),
                  pl.BlockSpec((None, 6, d), lambda i, j, e: (i, 0, 0)),
                  pl.BlockSpec((2, d), lambda i, j, e: (0, 0))],
        out_specs=pl.BlockSpec((None, tn, d), lambda i, j, e: (i, j, 0)),
        out_shape=jax.ShapeDtypeStruct((b, n, d), F32),
        compiler_params=_cparams("parallel", "parallel", "arbitrary"),
    )(pos_t, aff_tm, y, h, mod, ln)


def _ln2_kernel(h_ref, f_ref, mod_ref, ln_ref, o_ref, *, alpha):
    o_ref[...] = _layer_norm(alpha * h_ref[...] + mod_ref[5:6, :] * f_ref[...], ln_ref[0:1, :], ln_ref[1:2, :])


def ffn_residual_norm(h, f, mod, ln, alpha, tm):
    b, n, d = h.shape
    return pl.pallas_call(
        functools.partial(_ln2_kernel, alpha=alpha),
        grid=(b, n // tm),
        in_specs=[pl.BlockSpec((None, tm, d), lambda i, j: (i, j, 0)),
                  pl.BlockSpec((None, tm, d), lambda i, j: (i, j, 0)),
                  pl.BlockSpec((None, 6, d), lambda i, j: (i, 0, 0)),
                  pl.BlockSpec((2, d), lambda i, j: (0, 0))],
        out_specs=pl.BlockSpec((None, tm, d), lambda i, j: (i, j, 0)),
        out_shape=jax.ShapeDtypeStruct((b, n, d), F32),
        compiler_params=_cparams("parallel", "parallel"),
    )(h, f, mod, ln)


def expert_choice_ffn(token_sets, w_gate, w_up, w_down, layer, ln, alpha):
    b = token_sets[0][0].shape[0]
    routed = []
    for u2, aff_t, _, _ in token_sets:
        n, ne = u2.shape[1], aff_t.shape[1]
        cap = CAPACITY_FACTOR * n // ne
        pos = route(aff_t, cap)
        routed.append((pos, gather_tokens(pos, u2, cap), cap))
    xin = routed[0][1] if len(routed) == 1 else jnp.concatenate([r[1] for r in routed], axis=2)
    cap_all = xin.shape[2]
    rows = cap_all // 2 if cap_all >= 512 else cap_all
    tb = math.gcd(b, 4) if cap_all >= 512 else b
    y = expert_ffn(xin, w_gate, w_up, w_down, layer, tb=tb, tf=512, rows=rows)
    outs, start = [], 0
    for (u2, aff_t, h, mod), (pos, _, cap) in zip(token_sets, routed):
        outs.append(scatter_tokens(jnp.swapaxes(pos, 1, 2), jnp.swapaxes(aff_t, 1, 2), y, cap, start,
                                   h, mod, ln, alpha, tn=min(u2.shape[1], 2048)))
        start += cap
    return outs


def axial_rope_angles(n, rot_dim):
    rows = n // GRID_W
    row = jnp.repeat(jnp.arange(rows, dtype=F32), GRID_W)
    col = jnp.tile(jnp.arange(GRID_W, dtype=F32), rows)
    n_freq = rot_dim // 4
    inv = ROPE_BASE ** (-jnp.arange(n_freq, dtype=F32) / n_freq)
    return jnp.concatenate([row[:, None] * inv, col[:, None] * inv], -1)


def retention_angles(n, dim):
    theta = 1.0 / (ROPE_BASE ** jnp.linspace(0.0, 1.0, dim // 2, dtype=F32))
    return jnp.arange(n, dtype=F32)[:, None] * theta


def _ada_kernel(c_ref, w_ref, b_ref, o_ref):
    c = c_ref[...]
    o_ref[...] = _mm(c * jax.nn.sigmoid(c), w_ref[...], 2, 2) + b_ref[...]


def ada_modulation(cond_rows, ada_w, ada_b, tn):
    depth, d, nw = ada_w.shape
    rows = cond_rows.shape[0]
    return pl.pallas_call(
        _ada_kernel,
        grid=(depth, nw // tn),
        in_specs=[pl.BlockSpec((rows, d), lambda l, j: (0, 0)),
                  pl.BlockSpec((None, d, tn), lambda l, j: (l, 0, j)),
                  pl.BlockSpec((None, 1, tn), lambda l, j: (l, 0, j))],
        out_specs=pl.BlockSpec((None, rows, tn), lambda l, j: (l, 0, j)),
        out_shape=jax.ShapeDtypeStruct((depth, rows, nw), F32),
        compiler_params=_cparams("parallel", "parallel"),
    )(cond_rows, ada_w, ada_b.reshape(depth, 1, nw))


def kernel(x, c, ctx, c_ctx, ada_w, ada_b, w_in, w_out, ret_decay, diff_lambda, diff_norm, gdn_conv,
           gdn_a_log, gdn_dt_bias, gdn_norm, swa_sink, ln_g, ln_b, router_w, w_gate, w_up, w_down):
    b, n, d = x.shape
    lc = ctx.shape[1]
    depth = ada_w.shape[0]
    alpha = (2 * depth) ** 0.25
    main_perm, tail_perm, _ = _in_col_perm()
    ang_diff = axial_rope_angles(n, HEAD_DIM // 2)
    ang_swa = axial_rope_angles(n, HEAD_DIM)
    cos_diff, sin_diff = jnp.tile(jnp.cos(ang_diff), (1, 8)), jnp.tile(jnp.sin(ang_diff), (1, 8))
    cos_swa = jnp.tile(jnp.cos(ang_swa), (1, 4))
    sin_swa = jnp.tile(jnp.sin(ang_swa), (1, 4)) * jnp.where(jnp.arange(LANES) < LANES // 2, -1.0, 1.0)
    ang_ret = retention_angles(n, HEAD_DIM)
    cos_ret, sin_ret = jnp.tile(jnp.cos(ang_ret), (1, 4)), jnp.tile(jnp.sin(ang_ret), (1, 4))
    sub = 8
    rows = -(-(b + 1) // sub) * sub
    cond_rows = jnp.concatenate([c, c_ctx[None, :], jnp.zeros((rows - b - 1, d), F32)], axis=0)
    mods = ada_modulation(cond_rows, ada_w, ada_b, tn=d)
    h, hc = x, ctx
    for layer in range(depth):
        full_ctx = layer < depth - 1
        mod_l = mods[layer, :b].reshape(b, 6, d)
        mod_c = jnp.broadcast_to(mods[layer, b].reshape(1, 6, d), (b, 6, d))
        wl = w_in[layer]
        w_in_b = jnp.concatenate([wl[:, main_perm], wl[:, tail_perm],
                                  jnp.zeros((d, LANES - tail_perm.size), F32)], axis=1).astype(BF16)
        w_out_b = w_out[layer].astype(BF16)
        rw_t = router_w[layer].T
        ln1 = jnp.stack([ln_g[layer, 0], ln_b[layer, 0]])
        ln2 = jnp.stack([ln_g[layer, 1], ln_b[layer, 1]])
        p_l, ab_l = inproj(h, mod_l, w_in_b, tm=512)
        p_c, ab_c = inproj(hc, mod_c, w_in_b, tm=lc)
        ya_l, ya_c = retention(p_l, p_c, cos_ret, sin_ret, ret_decay[layer], full_ctx)
        yb_l, yb_c = diff_attention(p_l, p_c, cos_diff, sin_diff, diff_lambda[layer], diff_norm[layer],
                                    layer, full_ctx, tq=256)
        yc_l, yc_c = gdn(p_l, ab_l, p_c, ab_c, gdn_conv[layer], gdn_a_log[layer], gdn_dt_bias[layer],
                         gdn_norm[layer], full_ctx)
        yd_l, yd_c = swa_attention(p_l, p_c, cos_swa, sin_swa, swa_sink[layer], full_ctx, tq=256)
        y_l = jnp.concatenate([ya_l, yb_l, yc_l, yd_l], -1)
        h, u2, aff = outproj(y_l, h, mod_l, w_out_b, ln1, rw_t, alpha, tm=512)
        token_sets = [(u2, aff, h, mod_l)]
        if full_ctx:
            y_c = jnp.concatenate([ya_c, yb_c, yc_c, yd_c], -1)
            hc, u2c, affc = outproj(y_c, hc, mod_c, w_out_b, ln1, rw_t, alpha, tm=lc)
            token_sets.append((u2c, affc, hc, mod_c))
        streams = expert_choice_ffn(token_sets, w_gate, w_up, w_down, layer, ln2, alpha)
        h = streams[0]
        if full_ctx:
            hc = streams[1]
    return h
```
